```python
import math, functools
import jax, jax.numpy as jnp
from jax import lax
import numpy as np

D_MODEL = 2048
BATCH = 4
SEQ = 2048
DEPTH = 1
DEC_BATCH = 32
DEC_SEQ = 8
PAST_LEN = 16384
PAGE_SIZE = 128

MIX_WIDTH = D_MODEL
LRU_WIDTH = MIX_WIDTH // 2
LRU_HEADS = 8
LRU_BLOCK = LRU_WIDTH // LRU_HEADS
CONV_W = 4
LRU_C = 8.0
ATTN_WIDTH = MIX_WIDTH - LRU_WIDTH
HEAD_DIM = 128
N_HEADS = ATTN_WIDTH // HEAD_DIM
N_KV_HEADS = 2
GROUP = N_HEADS // N_KV_HEADS
WINDOW = 128
ROPE_THETA = 10000.0
D_FF = 5504
N_MOD = 9
EPS = 1e-6
NEG = -1e30
IN_SPLITS = (LRU_WIDTH, 2 * LRU_WIDTH, 2 * LRU_WIDTH + N_HEADS * HEAD_DIM,
             2 * LRU_WIDTH + (N_HEADS + N_KV_HEADS) * HEAD_DIM)
IN_COLS = 2 * LRU_WIDTH + (N_HEADS + 2 * N_KV_HEADS) * HEAD_DIM

kernel_name = 'hymba_rglru_swa_macaron_step'


def _rmsnorm(x, g):
    xf = x.astype(jnp.float32)
    y = xf * lax.rsqrt(jnp.mean(xf * xf, axis=-1, keepdims=True) + EPS) * g.astype(jnp.float32)
    return y.astype(x.dtype)


def _swiglu(h, wg, wu, wd):
    return (jax.nn.silu(h @ wg) * (h @ wu)) @ wd


def _rope(x, pos):
    half = HEAD_DIM // 2
    inv = ROPE_THETA ** (-jnp.arange(half, dtype=jnp.float32) / half)
    ang = pos.astype(jnp.float32)[:, None] * inv[None, :]
    shape = (pos.shape[0],) + (1,) * (x.ndim - 3) + (half,)
    cos = jnp.cos(ang).reshape(shape)
    sin = jnp.sin(ang).reshape(shape)
    xf = x.astype(jnp.float32)
    x1, x2 = xf[..., :half], xf[..., half:]
    return jnp.concatenate([x1 * cos - x2 * sin, x2 * cos + x1 * sin], axis=-1).astype(x.dtype)


def _causal_conv(xb, buf, w, b):
    S = xb.shape[1]
    full = jnp.concatenate([buf.astype(xb.dtype), xb], axis=1)
    y = b
    for j in range(CONV_W):
        y = y + full[:, j:j + S] * w[j]
    return y, full[:, -(CONV_W - 1):]


def _combine(left, right):
    a1, b1 = left
    a2, b2 = right
    return a1 * a2, a2 * b1 + b2


def _rg_lru(xc, h0, w_a, b_a, w_x, b_x, lam):
    B, S, C = xc.shape
    xh = xc.reshape(B, S, LRU_HEADS, LRU_BLOCK)
    r = jax.nn.sigmoid(jnp.einsum('bshi,hij->bshj', xh, w_a) + b_a).reshape(B, S, C)
    gi = jax.nn.sigmoid(jnp.einsum('bshi,hij->bshj', xh, w_x) + b_x).reshape(B, S, C)
    log_a = -LRU_C * r.astype(jnp.float32) * jax.nn.softplus(-lam.astype(jnp.float32))
    a = jnp.exp(log_a)
    u = jnp.sqrt(-jnp.expm1(2.0 * log_a)) * (gi * xc).astype(jnp.float32)
    a_cum, h_cum = lax.associative_scan(_combine, (a, u), axis=1)
    h = a_cum * h0.astype(jnp.float32)[:, None, :] + h_cum
    return h.astype(xc.dtype), h[:, -1]


def _sink_attention(q, k, v, mask, sinks):
    s = jnp.einsum('...qkgd,...skd->...kgqs', q, k).astype(jnp.float32) * (HEAD_DIM ** -0.5)
    s = jnp.where(mask, s, NEG)
    sink = sinks.astype(jnp.float32).reshape(N_KV_HEADS, GROUP, 1, 1)
    m = jnp.maximum(jnp.max(s, axis=-1, keepdims=True), sink)
    p = jnp.exp(s - m)
    probs = p / (jnp.sum(p, axis=-1, keepdims=True) + jnp.exp(sink - m))
    return jnp.einsum('...kgqs,...skd->...qkgd', probs.astype(v.dtype), v)


def _attend_prompt(q, k, v, sinks):
    B, S = q.shape[0], q.shape[1]
    nb = S // WINDOW
    qb = q.reshape(B, nb, WINDOW, N_KV_HEADS, GROUP, HEAD_DIM)
    kb = k.reshape(B, nb, WINDOW, N_KV_HEADS, HEAD_DIM)
    vb = v.reshape(B, nb, WINDOW, N_KV_HEADS, HEAD_DIM)

    def band(t):
        prev = jnp.concatenate([jnp.zeros_like(t[:, :1]), t[:, :-1]], axis=1)
        return jnp.concatenate([prev, t], axis=2)

    qpos = jnp.arange(S).reshape(nb, WINDOW, 1)
    kpos = ((jnp.arange(nb)[:, None] - 1) * WINDOW + jnp.arange(2 * WINDOW)[None, :])[:, None, :]
    mask = (kpos <= qpos) & (kpos >= qpos - WINDOW) & (kpos >= 0)
    out = _sink_attention(qb, band(kb), band(vb), mask[None, :, None, None], sinks)
    return out.reshape(B, S, ATTN_WIDTH), k[:, -WINDOW:], v[:, -WINDOW:]


def _attend_sample(q, k, v, sinks, cache_k, cache_v):
    B, S = q.shape[0], q.shape[1]
    k_all = jnp.concatenate([cache_k.astype(k.dtype), k], axis=1)
    v_all = jnp.concatenate([cache_v.astype(v.dtype), v], axis=1)
    qpos = PAST_LEN + jnp.arange(S)[:, None]
    kpos = PAST_LEN - WINDOW + jnp.arange(WINDOW + S)[None, :]
    mask = (kpos <= qpos) & (kpos >= qpos - WINDOW)
    out = _sink_attention(q, k_all, v_all, mask[None, None, None], sinks)
    return out.reshape(B, S, ATTN_WIDTH), k_all[:, -WINDOW:], v_all[:, -WINDOW:]


def _layer(x, c, conv_buf, h0, pos, attend, p):
    (ada_w, ada_b, g1, g2, g3, f1g, f1u, f1d, w_in, conv_w, conv_b, wa, ba, wx, bx,
     lam, sinks, w_out, f2g, f2u, f2d) = p
    B, S, _ = x.shape
    mod = (jax.nn.silu(c) @ ada_w + ada_b).reshape(B, N_MOD, 1, D_MODEL)
    sh1, sc1, gt1, sh2, sc2, gt2, sh3, sc3, gt3 = [mod[:, i] for i in range(N_MOD)]
    h = _rmsnorm(x, g1) * (1 + sc1) + sh1
    x = x + 0.5 * gt1 * _swiglu(h, f1g, f1u, f1d)
    h = _rmsnorm(x, g2) * (1 + sc2) + sh2
    xb, yb, q, k, v = jnp.split(h @ w_in, IN_SPLITS, axis=-1)
    xc, conv_new = _causal_conv(xb, conv_buf, conv_w, conv_b)
    hs, h_last = _rg_lru(xc, h0, wa, ba, wx, bx, lam)
    lru_out = hs * jax.nn.gelu(yb)
    q = _rope(q.reshape(B, S, N_KV_HEADS, GROUP, HEAD_DIM), pos)
    k = _rope(k.reshape(B, S, N_KV_HEADS, HEAD_DIM), pos)
    v = v.reshape(B, S, N_KV_HEADS, HEAD_DIM)
    attn_out, k_new, v_new = attend(q, k, v, sinks)
    x = x + gt2 * (jnp.concatenate([lru_out, attn_out], axis=-1) @ w_out)
    h = _rmsnorm(x, g3) * (1 + sc3) + sh3
    x = x + 0.5 * gt3 * _swiglu(h, f2g, f2u, f2d)
    return x, conv_new, h_last, k_new, v_new


def setup_inputs(seed: int = 0) -> dict:
    key = jax.random.key(seed)
    ks = iter(jax.random.split(key, 40))
    f32 = jnp.float32
    L = DEPTH

    def nrm(shape, scale):
        return jax.random.normal(next(ks), shape, f32) * scale

    u = jax.random.uniform(next(ks), (L, LRU_WIDTH), f32, 0.9, 0.999)
    return {
        'x_prompt': nrm((BATCH, SEQ, D_MODEL), 1.0),
        'x_sample': nrm((DEC_BATCH, DEC_SEQ, D_MODEL), 1.0),
        'c_prompt': nrm((BATCH, D_MODEL), 1.0),
        'c_sample': nrm((DEC_BATCH, D_MODEL), 1.0),
        'state_conv': nrm((L, DEC_BATCH, CONV_W - 1, LRU_WIDTH), 1.0),
        'state_lru': nrm((L, DEC_BATCH, LRU_WIDTH), 0.5),
        'cache_k_win': nrm((L, DEC_BATCH, WINDOW, N_KV_HEADS, HEAD_DIM), 1.0),
        'cache_v_win': nrm((L, DEC_BATCH, WINDOW, N_KV_HEADS, HEAD_DIM), 1.0),
        'ada_w': nrm((L, D_MODEL, N_MOD * D_MODEL), 0.5 * D_MODEL ** -0.5),
        'ada_b': nrm((L, N_MOD * D_MODEL), 0.02),
        'norm_ffn1': 1.0 + nrm((L, D_MODEL), 0.05),
        'norm_mix': 1.0 + nrm((L, D_MODEL), 0.05),
        'norm_ffn2': 1.0 + nrm((L, D_MODEL), 0.05),
        'ffn1_w_gate': nrm((L, D_MODEL, D_FF), D_MODEL ** -0.5),
        'ffn1_w_up': nrm((L, D_MODEL, D_FF), D_MODEL ** -0.5),
        'ffn1_w_down': nrm((L, D_FF, D_MODEL), D_FF ** -0.5),
        'w_in': nrm((L, D_MODEL, IN_COLS), D_MODEL ** -0.5),
        'conv_w': nrm((L, CONV_W, LRU_WIDTH), CONV_W ** -0.5),
        'conv_b': nrm((L, LRU_WIDTH), 0.02),
        'lru_w_a': nrm((L, LRU_HEADS, LRU_BLOCK, LRU_BLOCK), LRU_BLOCK ** -0.5),
        'lru_b_a': nrm((L, LRU_HEADS, LRU_BLOCK), 0.02),
        'lru_w_x': nrm((L, LRU_HEADS, LRU_BLOCK, LRU_BLOCK), LRU_BLOCK ** -0.5),
        'lru_b_x': nrm((L, LRU_HEADS, LRU_BLOCK), 0.02),
        'lru_lambda': jnp.log(u) - jnp.log1p(-u),
        'attn_sinks': nrm((L, N_HEADS), 1.0),
        'w_out': nrm((L, MIX_WIDTH, D_MODEL), MIX_WIDTH ** -0.5),
        'ffn2_w_gate': nrm((L, D_MODEL, D_FF), D_MODEL ** -0.5),
        'ffn2_w_up': nrm((L, D_MODEL, D_FF), D_MODEL ** -0.5),
        'ffn2_w_down': nrm((L, D_FF, D_MODEL), D_FF ** -0.5),
        'norm_final': 1.0 + nrm((D_MODEL,), 0.05),
    }


def reference(x_prompt, x_sample, c_prompt, c_sample, state_conv, state_lru, cache_k_win, cache_v_win,
              ada_w, ada_b, norm_ffn1, norm_mix, norm_ffn2, ffn1_w_gate, ffn1_w_up, ffn1_w_down,
              w_in, conv_w, conv_b, lru_w_a, lru_b_a, lru_w_x, lru_b_x, lru_lambda, attn_sinks,
              w_out, ffn2_w_gate, ffn2_w_up, ffn2_w_down, norm_final):
    Bp, Sp = x_prompt.shape[0], x_prompt.shape[1]
    pos_p = jnp.arange(Sp)
    pos_s = PAST_LEN + jnp.arange(x_sample.shape[1])
    xp, xs = x_prompt, x_sample
    conv_p, lru_p, k_p, v_p = [], [], [], []
    conv_s, lru_s, k_s, v_s = [], [], [], []
    for l in range(DEPTH):
        p = (ada_w[l], ada_b[l], norm_ffn1[l], norm_mix[l], norm_ffn2[l], ffn1_w_gate[l], ffn1_w_up[l],
             ffn1_w_down[l], w_in[l], conv_w[l], conv_b[l], lru_w_a[l], lru_b_a[l], lru_w_x[l], lru_b_x[l],
             lru_lambda[l], attn_sinks[l], w_out[l], ffn2_w_gate[l], ffn2_w_up[l], ffn2_w_down[l])
        zero_conv = jnp.zeros((Bp, CONV_W - 1, LRU_WIDTH), x_prompt.dtype)
        zero_h = jnp.zeros((Bp, LRU_WIDTH), jnp.float32)
        xp, cp, hp, kp, vp = _layer(xp, c_prompt, zero_conv, zero_h, pos_p, _attend_prompt, p)
        attend_s = functools.partial(_attend_sample, cache_k=cache_k_win[l], cache_v=cache_v_win[l])
        xs, cs, hs, kss, vss = _layer(xs, c_sample, state_conv[l], state_lru[l], pos_s, attend_s, p)
        conv_p.append(cp); lru_p.append(hp); k_p.append(kp); v_p.append(vp)
        conv_s.append(cs); lru_s.append(hs); k_s.append(kss); v_s.append(vss)
    y_prompt = _rmsnorm(xp, norm_final)
    y_sample = _rmsnorm(xs, norm_final)
    return (y_prompt, y_sample,
            jnp.stack(conv_p), jnp.stack(lru_p), jnp.stack(k_p), jnp.stack(v_p),
            jnp.stack(conv_s), jnp.stack(lru_s), jnp.stack(k_s), jnp.stack(v_s))
```

```python
import functools

import jax
import jax.numpy as jnp
from jax import lax
from jax.experimental import pallas as pl
from jax.experimental.pallas import tpu as pltpu

F32 = jnp.float32
BF16 = jnp.bfloat16

D_MODEL = 2048
D_FF = 5504
LRU_WIDTH = 1024
LRU_HEADS = 8
LRU_BLOCK = 128
CONV_W = 4
LRU_C = 8.0
HEAD_DIM = 128
N_HEADS = 8
N_KV_HEADS = 2
GROUP = N_HEADS // N_KV_HEADS
ATTN_WIDTH = N_HEADS * HEAD_DIM
KV_WIDTH = N_KV_HEADS * HEAD_DIM
WINDOW = 128
ROPE_THETA = 10000.0
N_MOD = 9
EPS = 1e-6
NEG = -1e30
PAST_LEN = 16384
IN_COLS = 2 * LRU_WIDTH + ATTN_WIDTH + 2 * KV_WIDTH
Q_COL = 2 * LRU_WIDTH
K_COL = Q_COL + ATTN_WIDTH
V_COL = K_COL + KV_WIDTH
SCALE = HEAD_DIM ** -0.5

SUBLANES = 8
LANES = 128
V7X_VMEM_LIMIT_BYTES = 56 * 1024 * 1024

MOD_ROWS = 48
MOD_PROMPT_ROWS = 8
ADALN_TN = 1024
FFN_TF = 256
NORM_ROWS = 128
PROJ_TN = 512
MIX_TS = 256


def _params(sem, vmem=V7X_VMEM_LIMIT_BYTES):
    return pltpu.CompilerParams(dimension_semantics=sem, vmem_limit_bytes=vmem)


def _sigmoid(x):
    return 1.0 / (1.0 + jnp.exp(-x))


def _normmod(x, g, sc, sh):
    ms = jnp.mean(x * x, axis=-1, keepdims=True)
    xn = x * lax.rsqrt(ms + EPS) * g
    return xn * (1.0 + sc) + sh


def _mod_rows(ref, rows, batch, per_token):
    if per_token:
        return ref[rows, :]
    return ref[pl.ds(batch, 1), :]


def _adaln_body(c_ref, w_ref, b_ref, o_ref):
    c = c_ref[...]
    s = (c * _sigmoid(c)).astype(BF16)
    o_ref[...] = jnp.dot(s, w_ref[...].astype(BF16), preferred_element_type=F32) + b_ref[...]


def _adaln(c_all, ada_w, ada_b):
    n = ada_w.shape[1]
    return pl.pallas_call(
        _adaln_body,
        grid=(n // ADALN_TN,),
        in_specs=[
            pl.BlockSpec((MOD_ROWS, D_MODEL), lambda j: (0, 0)),
            pl.BlockSpec((D_MODEL, ADALN_TN), lambda j: (0, j)),
            pl.BlockSpec((1, ADALN_TN), lambda j: (0, j)),
        ],
        out_specs=pl.BlockSpec((MOD_ROWS, ADALN_TN), lambda j: (0, j)),
        out_shape=jax.ShapeDtypeStruct((MOD_ROWS, n), F32),
        compiler_params=_params(("arbitrary",)),
        name="adaln",
    )(c_all, ada_w, ada_b.reshape(1, n))


def _ffn_body(x_ref, sh_ref, sc_ref, gt_ref, g_ref, wg_ref, wu_ref, wd_ref, gf_ref,
              o_ref, h_ref, *, nj, tf, rem, tiles_per_batch, per_token, final_norm):
    i = pl.program_id(0)
    j = pl.program_id(1)
    tm = x_ref.shape[0]
    batch = i // tiles_per_batch
    nchunks = tm // NORM_ROWS

    def chunk(r):
        return pl.ds(pl.multiple_of(r * NORM_ROWS, NORM_ROWS), NORM_ROWS)

    @pl.when(j == 0)
    def _():
        g = g_ref[...]

        def body(r, carry):
            rows = chunk(r)
            h = _normmod(x_ref[rows, :], g,
                         _mod_rows(sc_ref, rows, batch, per_token),
                         _mod_rows(sh_ref, rows, batch, per_token))
            h_ref[rows, :] = h.astype(BF16)
            o_ref[rows, :] = jnp.zeros((NORM_ROWS, D_MODEL), F32)
            return carry

        lax.fori_loop(0, nchunks, body, 0)

    def step(w):
        h = h_ref[...]
        g = jnp.dot(h, wg_ref[:, :w].astype(BF16), preferred_element_type=F32)
        u = jnp.dot(h, wu_ref[:, :w].astype(BF16), preferred_element_type=F32)
        a = (g * _sigmoid(g) * u).astype(BF16)
        o_ref[...] += jnp.dot(a, wd_ref[:w, :].astype(BF16), preferred_element_type=F32)

    if rem == tf:
        step(tf)
    else:
        @pl.when(j < nj - 1)
        def _():
            step(tf)

        @pl.when(j == nj - 1)
        def _():
            step(rem)

    @pl.when(j == nj - 1)
    def _():
        gf = gf_ref[...]

        def body(r, carry):
            rows = chunk(r)
            gt = _mod_rows(gt_ref, rows, batch, per_token)
            y = x_ref[rows, :] + (0.5 * gt) * o_ref[rows, :]
            if final_norm:
                ms = jnp.mean(y * y, axis=-1, keepdims=True)
                y = y * lax.rsqrt(ms + EPS) * gf
            o_ref[rows, :] = y
            return carry

        lax.fori_loop(0, nchunks, body, 0)


def _ffn(x, mod, chunks, g, wg, wu, wd, gf, *, tm, tiles_per_batch, per_token, final_norm, name):
    t = x.shape[0]
    nj = pl.cdiv(D_FF, FFN_TF)
    rem = D_FF - (nj - 1) * FFN_TF
    sh_c, sc_c, gt_c = chunks
    body = functools.partial(_ffn_body, nj=nj, tf=FFN_TF, rem=rem, tiles_per_batch=tiles_per_batch,
                             per_token=per_token, final_norm=final_norm)

    def mspec(c):
        if per_token:
            return pl.BlockSpec((tm, D_MODEL), lambda i, j: (i, c))
        return pl.BlockSpec((MOD_PROMPT_ROWS, D_MODEL), lambda i, j: (0, c))

    return pl.pallas_call(
        body,
        grid=(t // tm, nj),
        in_specs=[
            pl.BlockSpec((tm, D_MODEL), lambda i, j: (i, 0)),
            mspec(sh_c), mspec(sc_c), mspec(gt_c),
            pl.BlockSpec((1, D_MODEL), lambda i, j: (0, 0)),
            pl.BlockSpec((D_MODEL, FFN_TF), lambda i, j: (0, j)),
            pl.BlockSpec((D_MODEL, FFN_TF), lambda i, j: (0, j)),
            pl.BlockSpec((FFN_TF, D_MODEL), lambda i, j: (j, 0)),
            pl.BlockSpec((1, D_MODEL), lambda i, j: (0, 0)),
        ],
        out_specs=pl.BlockSpec((tm, D_MODEL), lambda i, j: (i, 0)),
        out_shape=jax.ShapeDtypeStruct((t, D_MODEL), F32),
        scratch_shapes=[pltpu.VMEM((tm, D_MODEL), BF16)],
        compiler_params=_params(("parallel", "arbitrary")),
        name=name,
    )(x, mod, mod, mod, g.reshape(1, D_MODEL), wg, wu, wd, gf.reshape(1, D_MODEL))


def _inproj_body(x_ref, sh_ref, sc_ref, g_ref, w_ref, o_ref, h_ref, *, tiles_per_batch, per_token):
    i = pl.program_id(0)
    j = pl.program_id(1)
    tm = x_ref.shape[0]
    batch = i // tiles_per_batch

    @pl.when(j == 0)
    def _():
        g = g_ref[...]

        def body(r, carry):
            rows = pl.ds(pl.multiple_of(r * NORM_ROWS, NORM_ROWS), NORM_ROWS)
            h = _normmod(x_ref[rows, :], g,
                         _mod_rows(sc_ref, rows, batch, per_token),
                         _mod_rows(sh_ref, rows, batch, per_token))
            h_ref[rows, :] = h.astype(BF16)
            return carry

        lax.fori_loop(0, tm // NORM_ROWS, body, 0)

    o_ref[...] = jnp.dot(h_ref[...], w_ref[...].astype(BF16), preferred_element_type=F32)


def _inproj(x, mod, chunks, g, w_in, *, tm, tiles_per_batch, per_token, name):
    t = x.shape[0]
    sh_c, sc_c = chunks
    body = functools.partial(_inproj_body, tiles_per_batch=tiles_per_batch, per_token=per_token)

    def mspec(c):
        if per_token:
            return pl.BlockSpec((tm, D_MODEL), lambda i, j: (i, c))
        return pl.BlockSpec((MOD_PROMPT_ROWS, D_MODEL), lambda i, j: (0, c))

    return pl.pallas_call(
        body,
        grid=(t // tm, IN_COLS // PROJ_TN),
        in_specs=[
            pl.BlockSpec((tm, D_MODEL), lambda i, j: (i, 0)),
            mspec(sh_c), mspec(sc_c),
            pl.BlockSpec((1, D_MODEL), lambda i, j: (0, 0)),
            pl.BlockSpec((D_MODEL, PROJ_TN), lambda i, j: (0, j)),
        ],
        out_specs=pl.BlockSpec((tm, PROJ_TN), lambda i, j: (i, j)),
        out_shape=jax.ShapeDtypeStruct((t, IN_COLS), F32),
        scratch_shapes=[pltpu.VMEM((tm, D_MODEL), BF16)],
        compiler_params=_params(("parallel", "arbitrary")),
        name=name,
    )(x, mod, mod, g.reshape(1, D_MODEL), w_in)


def _outproj_body(m_ref, x_ref, gt_ref, w_ref, o_ref, *, tiles_per_batch, per_token):
    i = pl.program_id(0)
    batch = i // tiles_per_batch
    d = jnp.dot(m_ref[...].astype(BF16), w_ref[...].astype(BF16), preferred_element_type=F32)
    gt = gt_ref[...] if per_token else gt_ref[pl.ds(batch, 1), :]
    o_ref[...] = x_ref[...] + gt * d


def _outproj(mix, x, mod, gt_c, w_out, *, tm, tiles_per_batch, per_token, name):
    t = x.shape[0]
    nn = D_MODEL // PROJ_TN
    body = functools.partial(_outproj_body, tiles_per_batch=tiles_per_batch, per_token=per_token)
    if per_token:
        gspec = pl.BlockSpec((tm, PROJ_TN), lambda i, j: (i, gt_c * nn + j))
    else:
        gspec = pl.BlockSpec((MOD_PROMPT_ROWS, PROJ_TN), lambda i, j: (0, gt_c * nn + j))
    return pl.pallas_call(
        body,
        grid=(t // tm, nn),
        in_specs=[
            pl.BlockSpec((tm, D_MODEL), lambda i, j: (i, 0)),
            pl.BlockSpec((tm, PROJ_TN), lambda i, j: (i, j)),
            gspec,
            pl.BlockSpec((D_MODEL, PROJ_TN), lambda i, j: (0, j)),
        ],
        out_specs=pl.BlockSpec((tm, PROJ_TN), lambda i, j: (i, j)),
        out_shape=jax.ShapeDtypeStruct((t, D_MODEL), F32),
        compiler_params=_params(("parallel", "arbitrary")),
        name=name,
    )(mix, x, mod, w_out)


def _rope(x, cos, sin_signed):
    return x * cos + pltpu.roll(x, HEAD_DIM // 2, 1) * sin_signed


def _lru_gates(xc_ref, a_ref, wa_ref, ba_ref, wx_ref, bx_ref, lam_ref):
    nlam = -lam_ref[...]
    softplus = jnp.maximum(nlam, 0.0) + jnp.log1p(jnp.exp(-jnp.abs(nlam)))
    rate = -LRU_C * softplus
    for hh in range(LRU_HEADS):
        cols = slice(hh * LRU_BLOCK, (hh + 1) * LRU_BLOCK)
        xc = xc_ref[:, cols]
        xcb = xc.astype(BF16)
        ra = jnp.dot(xcb, wa_ref[hh].astype(BF16), preferred_element_type=F32) + ba_ref[:, cols]
        rx = jnp.dot(xcb, wx_ref[hh].astype(BF16), preferred_element_type=F32) + bx_ref[:, cols]
        r = _sigmoid(ra)
        gi = _sigmoid(rx)
        a = jnp.exp(r * rate[:, cols])
        a_ref[:, cols] = a
        xc_ref[:, cols] = jnp.sqrt(1.0 - a * a) * (gi * xc)


def _scan_group(a, u, carry, row):
    for s in (1, 2, 4):
        a_sh = pltpu.roll(a, s, 0)
        u_sh = pltpu.roll(u, s, 0)
        m = row >= s
        u = jnp.where(m, a * u_sh + u, u)
        a = jnp.where(m, a * a_sh, a)
    return a * carry + u


def _softmax_pv(s, mask, sink, v):
    s = jnp.where(mask, s, NEG)
    m = jnp.maximum(jnp.max(s, axis=-1, keepdims=True), sink)
    p = jnp.exp(s - m)
    den = jnp.sum(p, axis=-1, keepdims=True) + jnp.exp(sink - m)
    return jnp.dot(p.astype(BF16), v, preferred_element_type=F32) / den


def _sink_column(sink_ref, kh, rows_per_head):
    n = GROUP * rows_per_head
    ri = lax.broadcasted_iota(jnp.int32, (n, 1), 0)
    col = jnp.full((n, 1), sink_ref[kh * GROUP + GROUP - 1], F32)
    for g in range(GROUP - 2, -1, -1):
        col = jnp.where(ri < (g + 1) * rows_per_head, sink_ref[kh * GROUP + g], col)
    return col


def _mix_prompt_body(xb_ref, yb_ref, q_ref, kv_ref, cs_ref, sn_ref, cw_ref, cb_ref,
                     wa_ref, ba_ref, wx_ref, bx_ref, lam_ref, sink_ref,
                     mix_ref, lru_ref, k_ref,
                     xpad, hc, kpad, vpad, a_s, u_s, *, ts):
    t = pl.program_id(1)

    @pl.when(t == 0)
    def _():
        xpad[0:SUBLANES, :] = jnp.zeros((SUBLANES, LRU_WIDTH), F32)
        hc[...] = jnp.zeros((SUBLANES, LRU_WIDTH), F32)
        kpad[0:WINDOW, :] = jnp.zeros((WINDOW, KV_WIDTH), BF16)
        vpad[0:WINDOW, :] = jnp.zeros((WINDOW, KV_WIDTH), BF16)

    xpad[SUBLANES:SUBLANES + ts, :] = xb_ref[...]
    xc = cb_ref[...]
    for jj in range(CONV_W):
        off = SUBLANES - (CONV_W - 1) + jj
        xc = xc + xpad[off:off + ts, :] * cw_ref[jj:jj + 1, :]
    u_s[...] = xc
    xpad[0:SUBLANES, :] = xpad[ts:ts + SUBLANES, :]

    _lru_gates(u_s, a_s, wa_ref, ba_ref, wx_ref, bx_ref, lam_ref)

    row = lax.broadcasted_iota(jnp.int32, (SUBLANES, LRU_WIDTH), 0)

    def scan_body(g, carry):
        rows = pl.ds(pl.multiple_of(g * SUBLANES, SUBLANES), SUBLANES)
        h = _scan_group(a_s[rows, :], u_s[rows, :], carry, row)
        a_s[rows, :] = h
        return jnp.broadcast_to(h[SUBLANES - 1:SUBLANES, :], (SUBLANES, LRU_WIDTH))

    carry = lax.fori_loop(0, ts // SUBLANES, scan_body, hc[...], unroll=4)
    hc[...] = carry
    lru_ref[0] = carry[0:1, :]
    mix_ref[:, 0:LRU_WIDTH] = (a_s[...] * jax.nn.gelu(yb_ref[...])).astype(BF16)

    cos = cs_ref[...]
    sin = sn_ref[...]
    for kh in range(N_KV_HEADS):
        cols = slice(kh * HEAD_DIM, (kh + 1) * HEAD_DIM)
        kr = _rope(kv_ref[:, cols], cos, sin)
        kpad[WINDOW:WINDOW + ts, cols] = kr.astype(BF16)
        k_ref[0, :, cols] = kr[ts - WINDOW:, :]
    vpad[WINDOW:WINDOW + ts, :] = kv_ref[:, KV_WIDTH:2 * KV_WIDTH].astype(BF16)

    nq = GROUP * WINDOW
    r_i = lax.broadcasted_iota(jnp.int32, (nq, 2 * WINDOW), 0) & (WINDOW - 1)
    c_i = lax.broadcasted_iota(jnp.int32, (nq, 2 * WINDOW), 1)
    band = (c_i >= r_i) & (c_i <= r_i + WINDOW)
    for n in range(ts // WINDOW):
        qrows = slice(n * WINDOW, (n + 1) * WINDOW)
        kpos0 = t * ts + (n - 1) * WINDOW
        mask = band & (c_i + kpos0 >= 0)
        for kh in range(N_KV_HEADS):
            cols = slice(kh * HEAD_DIM, (kh + 1) * HEAD_DIM)
            qs = []
            for g in range(GROUP):
                hd = kh * GROUP + g
                qh = q_ref[qrows, hd * HEAD_DIM:(hd + 1) * HEAD_DIM]
                qs.append(_rope(qh, cos[qrows, :], sin[qrows, :]).astype(BF16))
            q4 = jnp.concatenate(qs, axis=0)
            kk = kpad[n * WINDOW:(n + 2) * WINDOW, cols]
            vv = vpad[n * WINDOW:(n + 2) * WINDOW, cols]
            s = lax.dot_general(q4, kk, (((1,), (1,)), ((), ())), preferred_element_type=F32) * SCALE
            o = _softmax_pv(s, mask, _sink_column(sink_ref, kh, WINDOW), vv)
            for g in range(GROUP):
                hd = kh * GROUP + g
                mix_ref[qrows, LRU_WIDTH + hd * HEAD_DIM:LRU_WIDTH + (hd + 1) * HEAD_DIM] = (
                    o[g * WINDOW:(g + 1) * WINDOW, :].astype(BF16))

    kpad[0:WINDOW, :] = kpad[ts:ts + WINDOW, :]
    vpad[0:WINDOW, :] = vpad[ts:ts + WINDOW, :]


def _mix_prompt(proj, cos, sin, p, *, batch, seq):
    ts = MIX_TS
    nt = seq // ts
    body = functools.partial(_mix_prompt_body, ts=ts)
    w = LRU_WIDTH

    def tile(width, col_block):
        return pl.BlockSpec((ts, width), lambda b, t: (b * nt + t, col_block))

    def whole(shape):
        return pl.BlockSpec(shape, lambda b, t: (0,) * len(shape))

    return pl.pallas_call(
        body,
        grid=(batch, nt),
        in_specs=[
            tile(w, 0), tile(w, 1), tile(ATTN_WIDTH, Q_COL // ATTN_WIDTH),
            tile(2 * KV_WIDTH, K_COL // (2 * KV_WIDTH)),
            pl.BlockSpec((ts, HEAD_DIM), lambda b, t: (t, 0)),
            pl.BlockSpec((ts, HEAD_DIM), lambda b, t: (t, 0)),
            whole((CONV_W, w)), whole((1, w)),
            whole((LRU_HEADS, LRU_BLOCK, LRU_BLOCK)), whole((1, w)),
            whole((LRU_HEADS, LRU_BLOCK, LRU_BLOCK)), whole((1, w)),
            whole((1, w)),
            pl.BlockSpec(memory_space=pltpu.SMEM),
        ],
        out_specs=[
            pl.BlockSpec((ts, D_MODEL), lambda b, t: (b * nt + t, 0)),
            pl.BlockSpec((1, 1, w), lambda b, t: (b, 0, 0)),
            pl.BlockSpec((1, WINDOW, KV_WIDTH), lambda b, t: (b, 0, 0)),
        ],
        out_shape=[
            jax.ShapeDtypeStruct((batch * seq, D_MODEL), BF16),
            jax.ShapeDtypeStruct((batch, 1, w), F32),
            jax.ShapeDtypeStruct((batch, WINDOW, KV_WIDTH), F32),
        ],
        scratch_shapes=[
            pltpu.VMEM((SUBLANES + ts, w), F32),
            pltpu.VMEM((SUBLANES, w), F32),
            pltpu.VMEM((WINDOW + ts, KV_WIDTH), BF16),
            pltpu.VMEM((WINDOW + ts, KV_WIDTH), BF16),
            pltpu.VMEM((ts, w), F32),
            pltpu.VMEM((ts, w), F32),
        ],
        compiler_params=_params(("parallel", "arbitrary")),
        name="mix_prompt",
    )(proj, proj, proj, proj, cos, sin, p["conv_w"], p["conv_b"], p["wa"], p["ba"], p["wx"], p["bx"],
      p["lam"], p["sinks"])


def _mix_sample_body(proj_ref, prev_ref, h0_ref, ck_ref, cv_ref, cs_ref, sn_ref, cw_ref, cb_ref,
                     wa_ref, ba_ref, wx_ref, bx_ref, lam_ref, sink_ref,
                     mix_ref, lru_ref, knew_ref,
                     a_s, u_s, q_s, *, batch, seq):
    w = LRU_WIDTH
    row = lax.broadcasted_iota(jnp.int32, (SUBLANES, w), 0)

    def group(b):
        return pl.ds(pl.multiple_of(b * SUBLANES, SUBLANES), SUBLANES)

    def conv_body(b, carry):
        rows = group(b)
        cur = proj_ref[rows, 0:w]
        prev = prev_ref[rows, :]
        xc = cb_ref[...]
        for jj in range(CONV_W):
            d = CONV_W - 1 - jj
            if d == 0:
                term = cur
            else:
                term = jnp.where(row >= d, pltpu.roll(cur, d, 0), pltpu.roll(prev, d, 0))
            xc = xc + term * cw_ref[jj:jj + 1, :]
        u_s[rows, :] = xc
        return carry

    lax.fori_loop(0, batch, conv_body, 0)

    _lru_gates(u_s, a_s, wa_ref, ba_ref, wx_ref, bx_ref, lam_ref)

    def scan_body(b, carry):
        rows = group(b)
        h0 = jnp.broadcast_to(h0_ref[pl.ds(b, 1), :], (SUBLANES, w))
        h = _scan_group(a_s[rows, :], u_s[rows, :], h0, row)
        a_s[rows, :] = h
        lru_ref[pl.ds(b, 1), :] = h[SUBLANES - 1:SUBLANES, :]
        return carry

    lax.fori_loop(0, batch, scan_body, 0)
    mix_ref[:, 0:w] = a_s[...] * jax.nn.gelu(proj_ref[:, w:2 * w])

    cos = cs_ref[...]
    sin = sn_ref[...]
    for hd in range(N_HEADS):
        cols = slice(hd * HEAD_DIM, (hd + 1) * HEAD_DIM)
        q_s[:, cols] = _rope(proj_ref[:, Q_COL + hd * HEAD_DIM:Q_COL + (hd + 1) * HEAD_DIM], cos, sin)
    for kh in range(N_KV_HEADS):
        cols = slice(kh * HEAD_DIM, (kh + 1) * HEAD_DIM)
        knew_ref[:, cols] = _rope(proj_ref[:, K_COL + kh * HEAD_DIM:K_COL + (kh + 1) * HEAD_DIM], cos, sin)

    nq = GROUP * seq
    nk = 2 * WINDOW
    r_i = lax.broadcasted_iota(jnp.int32, (nq, nk), 0) & (seq - 1)
    c_i = lax.broadcasted_iota(jnp.int32, (nq, nk), 1)
    mask = (c_i >= r_i) & (c_i <= r_i + WINDOW)
    pad = jnp.zeros((nk - WINDOW - seq, HEAD_DIM), F32)

    def attn_body(b, carry):
        rows = group(b)
        for kh in range(N_KV_HEADS):
            cols = slice(kh * HEAD_DIM, (kh + 1) * HEAD_DIM)
            q4 = jnp.concatenate(
                [q_s[rows, (kh * GROUP + g) * HEAD_DIM:(kh * GROUP + g + 1) * HEAD_DIM] for g in range(GROUP)],
                axis=0).astype(BF16)
            kk = jnp.concatenate([ck_ref[b, :, cols], knew_ref[rows, cols], pad], axis=0).astype(BF16)
            vv = jnp.concatenate(
                [cv_ref[b, :, cols], proj_ref[rows, V_COL + kh * HEAD_DIM:V_COL + (kh + 1) * HEAD_DIM], pad],
                axis=0).astype(BF16)
            s = lax.dot_general(q4, kk, (((1,), (1,)), ((), ())), preferred_element_type=F32) * SCALE
            o = _softmax_pv(s, mask, _sink_column(sink_ref, kh, seq), vv)
            for g in range(GROUP):
                hd = kh * GROUP + g
                mix_ref[rows, w + hd * HEAD_DIM:w + (hd + 1) * HEAD_DIM] = o[g * seq:(g + 1) * seq, :]
        return carry

    lax.fori_loop(0, batch, attn_body, 0)


def _mix_sample(proj, prev, h0, cache_k, cache_v, cos, sin, p, *, batch, seq):
    assert seq == SUBLANES, "each sample batch must be exactly one sublane group"
    t = batch * seq
    body = functools.partial(_mix_sample_body, batch=batch, seq=seq)
    vmem = pl.BlockSpec(memory_space=pltpu.VMEM)
    return pl.pallas_call(
        body,
        in_specs=[vmem] * 14 + [pl.BlockSpec(memory_space=pltpu.SMEM)],
        out_specs=[vmem, vmem, vmem],
        out_shape=[
            jax.ShapeDtypeStruct((t, D_MODEL), F32),
            jax.ShapeDtypeStruct((batch, LRU_WIDTH), F32),
            jax.ShapeDtypeStruct((t, KV_WIDTH), F32),
        ],
        scratch_shapes=[
            pltpu.VMEM((t, LRU_WIDTH), F32),
            pltpu.VMEM((t, LRU_WIDTH), F32),
            pltpu.VMEM((t, ATTN_WIDTH), F32),
        ],
        compiler_params=pltpu.CompilerParams(vmem_limit_bytes=V7X_VMEM_LIMIT_BYTES),
        name="mix_sample",
    )(proj, prev, h0, cache_k, cache_v, cos, sin, p["conv_w"], p["conv_b"], p["wa"], p["ba"], p["wx"], p["bx"],
      p["lam"], p["sinks"])


def _layer(w, l):
    return w.reshape(w.shape[1:]) if w.shape[0] == 1 else w[l]


def _rope_tables(pos):
    half = HEAD_DIM // 2
    inv = ROPE_THETA ** (-jnp.arange(half, dtype=F32) / half)
    ang = pos.astype(F32)[:, None] * inv[None, :]
    cos = jnp.cos(ang)
    sin = jnp.sin(ang)
    return jnp.concatenate([cos, cos], axis=-1), jnp.concatenate([-sin, sin], axis=-1)


def kernel(x_prompt, x_sample, c_prompt, c_sample, state_conv, state_lru, cache_k_win, cache_v_win, ada_w, ada_b, norm_ffn1, norm_mix, norm_ffn2, ffn1_w_gate, ffn1_w_up, ffn1_w_down, w_in, conv_w, conv_b, lru_w_a, lru_b_a, lru_w_x, lru_b_x, lru_lambda, attn_sinks, w_out, ffn2_w_gate, ffn2_w_up, ffn2_w_down, norm_final):
    bp, sp, _ = x_prompt.shape
    bs, ss, _ = x_sample.shape
    depth = ada_w.shape[0]
    tp, tsn = bp * sp, bs * ss
    assert bp <= MOD_PROMPT_ROWS and bp + bs <= MOD_ROWS
    tm_p = 1024
    assert sp % tm_p == 0
    tpb = sp // tm_p

    xp = x_prompt.reshape(tp, D_MODEL)
    xs = x_sample.reshape(tsn, D_MODEL)
    c_all = jnp.concatenate(
        [c_prompt, jnp.zeros((MOD_PROMPT_ROWS - bp, D_MODEL), F32), c_sample,
         jnp.zeros((MOD_ROWS - MOD_PROMPT_ROWS - bs, D_MODEL), F32)], axis=0)
    cos_p, sin_p = _rope_tables(jnp.arange(sp))
    cos_s, sin_s = _rope_tables(PAST_LEN + jnp.arange(ss))
    cos_s = jnp.tile(cos_s, (bs, 1))
    sin_s = jnp.tile(sin_s, (bs, 1))

    outs_p, outs_s = [], []
    for l in range(depth):
        last = l == depth - 1
        L = functools.partial(_layer, l=l)
        cache_k, cache_v = L(cache_k_win), L(cache_v_win)
        mod = _adaln(c_all, L(ada_w), L(ada_b))
        mod_s = jnp.repeat(mod[MOD_PROMPT_ROWS:MOD_PROMPT_ROWS + bs], ss, axis=0)
        p = dict(conv_w=L(conv_w), conv_b=L(conv_b).reshape(1, LRU_WIDTH), wa=L(lru_w_a),
                 ba=L(lru_b_a).reshape(1, LRU_WIDTH), wx=L(lru_w_x), bx=L(lru_b_x).reshape(1, LRU_WIDTH),
                 lam=L(lru_lambda).reshape(1, LRU_WIDTH), sinks=L(attn_sinks))
        ffn1 = (L(norm_ffn1), L(ffn1_w_gate), L(ffn1_w_up), L(ffn1_w_down), norm_final)
        ffn2 = (L(norm_ffn2), L(ffn2_w_gate), L(ffn2_w_up), L(ffn2_w_down), norm_final)
        prompt = dict(tm=tm_p, tiles_per_batch=tpb, per_token=False)
        sample = dict(tm=tsn, tiles_per_batch=1, per_token=True)

        xp = _ffn(xp, mod, (0, 1, 2), *ffn1, final_norm=False, name="ffn1_prompt", **prompt)
        xs = _ffn(xs, mod_s, (0, 1, 2), *ffn1, final_norm=False, name="ffn1_sample", **sample)

        proj_p = _inproj(xp, mod, (3, 4), L(norm_mix), L(w_in), name="inproj_prompt", **prompt)
        proj_s = _inproj(xs, mod_s, (3, 4), L(norm_mix), L(w_in), name="inproj_sample", **sample)
        mix_p, lru_p, k_p = _mix_prompt(proj_p, cos_p, sin_p, p, batch=bp, seq=sp)
        prev_s = jnp.pad(L(state_conv), ((0, 0), (SUBLANES - (CONV_W - 1), 0), (0, 0))).reshape(tsn, LRU_WIDTH)
        mix_s, lru_s, knew_s = _mix_sample(
            proj_s, prev_s, L(state_lru), cache_k.reshape(bs, WINDOW, KV_WIDTH),
            cache_v.reshape(bs, WINDOW, KV_WIDTH), cos_s, sin_s, p, batch=bs, seq=ss)
        xp = _outproj(mix_p, xp, mod, 5, L(w_out), name="outproj_prompt", **prompt)
        xs = _outproj(mix_s, xs, mod_s, 5, L(w_out), name="outproj_sample", **sample)

        xp = _ffn(xp, mod, (6, 7, 8), *ffn2, final_norm=last, name="ffn2_prompt", **prompt)
        xs = _ffn(xs, mod_s, (6, 7, 8), *ffn2, final_norm=last, name="ffn2_sample", **sample)

        pp = proj_p.reshape(bp, sp, IN_COLS)
        ps = proj_s.reshape(bs, ss, IN_COLS)
        outs_p.append((
            pp[:, sp - (CONV_W - 1):, 0:LRU_WIDTH],
            lru_p.reshape(bp, LRU_WIDTH),
            k_p.reshape(bp, WINDOW, N_KV_HEADS, HEAD_DIM),
            pp[:, sp - WINDOW:, V_COL:].reshape(bp, WINDOW, N_KV_HEADS, HEAD_DIM),
        ))
        k_all = jnp.concatenate([cache_k, knew_s.reshape(bs, ss, N_KV_HEADS, HEAD_DIM)], axis=1)
        v_all = jnp.concatenate([cache_v, ps[:, :, V_COL:].reshape(bs, ss, N_KV_HEADS, HEAD_DIM)], axis=1)
        outs_s.append((
            ps[:, ss - (CONV_W - 1):, 0:LRU_WIDTH],
            lru_s,
            k_all[:, -WINDOW:],
            v_all[:, -WINDOW:],
        ))

    y_prompt = xp.reshape(bp, sp, D_MODEL)
    y_sample = xs.reshape(bs, ss, D_MODEL)
    st_p = [jnp.stack([o[i] for o in outs_p]) for i in range(4)]
    st_s = [jnp.stack([o[i] for o in outs_s]) for i in range(4)]
    return (y_prompt, y_sample, st_p[0], st_p[1], st_p[2], st_p[3], st_s[0], st_s[1], st_s[2], st_s[3])
```

```python
import functools

import jax
import jax.numpy as jnp
from jax import lax
from jax.experimental import pallas as pl
from jax.experimental.pallas import tpu as pltpu

F32 = jnp.float32
BF16 = jnp.bfloat16

D_MODEL = 2048
D_FF = 5504
LRU_WIDTH = 1024
LRU_HEADS = 8
LRU_BLOCK = 128
CONV_W = 4
LRU_C = 8.0
HEAD_DIM = 128
N_HEADS = 8
N_KV_HEADS = 2
GROUP = N_HEADS // N_KV_HEADS
ATTN_WIDTH = N_HEADS * HEAD_DIM
KV_WIDTH = N_KV_HEADS * HEAD_DIM
WINDOW = 128
ROPE_THETA = 10000.0
N_MOD = 9
EPS = 1e-6
NEG = -1e30
PAST_LEN = 16384
IN_COLS = 2 * LRU_WIDTH + ATTN_WIDTH + 2 * KV_WIDTH
Q_COL = 2 * LRU_WIDTH
K_COL = Q_COL + ATTN_WIDTH
V_COL = K_COL + KV_WIDTH
SCALE = HEAD_DIM ** -0.5

SUBLANES = 8
LANES = 128
V7X_VMEM_LIMIT_BYTES = 56 * 1024 * 1024

MOD_ROWS = 48
MOD_PROMPT_ROWS = 8
ADALN_TN = 1024
FFN_TF = 512
FFN_ROWS = 16
NORM_ROWS = 128
PROJ_TN = 512
MIX_TS = 256


def _params(sem, vmem=V7X_VMEM_LIMIT_BYTES):
    return pltpu.CompilerParams(dimension_semantics=sem, vmem_limit_bytes=vmem)


def _sigmoid(x):
    return 1.0 / (1.0 + jnp.exp(-x))


def _normmod(x, g, sc, sh):
    ms = jnp.mean(x * x, axis=-1, keepdims=True)
    xn = x * lax.rsqrt(ms + EPS) * g
    return xn * (1.0 + sc) + sh


def _mod_rows(ref, rows, batch, per_token):
    if per_token:
        return ref[rows, :]
    return ref[pl.ds(batch, 1), :]


def _adaln_body(c_ref, w_ref, b_ref, o_ref):
    c = c_ref[...]
    s = (c * _sigmoid(c)).astype(BF16)
    o_ref[...] = jnp.dot(s, w_ref[...].astype(BF16), preferred_element_type=F32) + b_ref[...]


def _adaln(c_all, ada_w, ada_b):
    n = ada_w.shape[1]
    return pl.pallas_call(
        _adaln_body,
        grid=(n // ADALN_TN,),
        in_specs=[
            pl.BlockSpec((MOD_ROWS, D_MODEL), lambda j: (0, 0)),
            pl.BlockSpec((D_MODEL, ADALN_TN), lambda j: (0, j)),
            pl.BlockSpec((1, ADALN_TN), lambda j: (0, j)),
        ],
        out_specs=pl.BlockSpec((MOD_ROWS, ADALN_TN), lambda j: (0, j)),
        out_shape=jax.ShapeDtypeStruct((MOD_ROWS, n), F32),
        compiler_params=_params(("arbitrary",)),
        name="adaln",
    )(c_all, ada_w, ada_b.reshape(1, n))


def _ffn_body(x_ref, sh_ref, sc_ref, gt_ref, g_ref, wg_ref, wu_ref, wd_ref, gf_ref, *rest,
              nr, row0, batch_shift, final_norm, emit):
    if emit:
        o_ref, wgb_ref, wub_ref, wdb_ref, h_ref = rest
    else:
        o_ref, h_ref = rest
    i = pl.program_id(0)
    j = pl.program_id(1)
    nj = pl.num_programs(1)
    tm = x_ref.shape[0]
    tf = wd_ref.shape[0]
    nchunks = tm // nr

    def chunk(r):
        return pl.ds(pl.multiple_of(r * nr, nr), nr)

    def mod_row(ref, r):
        b = row0 + lax.shift_right_logical(i * tm + r * nr, batch_shift)
        return ref[pl.ds(b, 1), :]

    @pl.when(j == 0)
    def _():
        g = g_ref[...]

        def body(r, carry):
            rows = chunk(r)
            h = _normmod(x_ref[rows, :], g, mod_row(sc_ref, r), mod_row(sh_ref, r))
            h_ref[rows, :] = h.astype(h_ref.dtype)
            o_ref[rows, :] = jnp.zeros((nr, D_MODEL), F32)
            return carry

        lax.fori_loop(0, nchunks, body, 0, unroll=4)

    if emit:
        valid = D_FF - j * tf
        cmask = lax.broadcasted_iota(jnp.int32, (1, tf), 1) < valid
        rmask = lax.broadcasted_iota(jnp.int32, (tf, 1), 0) < valid
        wg = jnp.where(cmask, wg_ref[...], 0.0).astype(BF16)
        wu = jnp.where(cmask, wu_ref[...], 0.0).astype(BF16)
        wd = jnp.where(rmask, wd_ref[...], 0.0).astype(BF16)
        wgb_ref[...] = wg
        wub_ref[...] = wu
        wdb_ref[...] = wd
    else:
        wg, wu, wd = wg_ref[...], wu_ref[...], wd_ref[...]

    h = h_ref[...].astype(BF16)
    g = jnp.dot(h, wg, preferred_element_type=F32)
    u = jnp.dot(h, wu, preferred_element_type=F32)
    a = (g * _sigmoid(g) * u).astype(BF16)
    o_ref[...] += jnp.dot(a, wd, preferred_element_type=F32)

    @pl.when(j == nj - 1)
    def _():
        gf = gf_ref[...]

        def body(r, carry):
            rows = chunk(r)
            y = x_ref[rows, :] + (0.5 * mod_row(gt_ref, r)) * o_ref[rows, :]
            if final_norm:
                ms = jnp.mean(y * y, axis=-1, keepdims=True)
                y = y * lax.rsqrt(ms + EPS) * gf
            o_ref[rows, :] = y
            return carry

        lax.fori_loop(0, nchunks, body, 0)


def _ffn(x, mod, chunks, g, wg, wu, wd, gf, *, tm, nr, row0, rows_per_batch, final_norm, emit, name):
    t = x.shape[0]
    tf = FFN_TF
    nj = pl.cdiv(D_FF, tf)
    assert rows_per_batch & (rows_per_batch - 1) == 0 and rows_per_batch % nr == 0 and tm % nr == 0
    body = functools.partial(_ffn_body, nr=nr, row0=row0, batch_shift=rows_per_batch.bit_length() - 1,
                             final_norm=final_norm, emit=emit)

    def mspec(c):
        return pl.BlockSpec((MOD_ROWS, D_MODEL), lambda i, j: (0, c))

    row = pl.BlockSpec((1, D_MODEL), lambda i, j: (0, 0))
    if emit:
        wspecs = [pl.BlockSpec((D_MODEL, tf), lambda i, j: (0, j)),
                  pl.BlockSpec((D_MODEL, tf), lambda i, j: (0, j)),
                  pl.BlockSpec((tf, D_MODEL), lambda i, j: (j, 0))]
    else:
        wspecs = [pl.BlockSpec((None, D_MODEL, tf), lambda i, j: (j, 0, 0)),
                  pl.BlockSpec((None, D_MODEL, tf), lambda i, j: (j, 0, 0)),
                  pl.BlockSpec((tf, D_MODEL), lambda i, j: (j, 0))]
    out_specs = [pl.BlockSpec((tm, D_MODEL), lambda i, j: (i, 0))]
    out_shape = [jax.ShapeDtypeStruct((t, D_MODEL), F32)]
    if emit:
        assert t == tm, "bf16 weight tiles are written once, by a single token tile"
        out_specs += [pl.BlockSpec((None, D_MODEL, tf), lambda i, j: (j, 0, 0)),
                      pl.BlockSpec((None, D_MODEL, tf), lambda i, j: (j, 0, 0)),
                      pl.BlockSpec((tf, D_MODEL), lambda i, j: (j, 0))]
        out_shape += [jax.ShapeDtypeStruct((nj, D_MODEL, tf), BF16),
                      jax.ShapeDtypeStruct((nj, D_MODEL, tf), BF16),
                      jax.ShapeDtypeStruct((nj * tf, D_MODEL), BF16)]
    return pl.pallas_call(
        body,
        grid=(t // tm, nj),
        in_specs=[pl.BlockSpec((tm, D_MODEL), lambda i, j: (i, 0)),
                  mspec(chunks[0]), mspec(chunks[1]), mspec(chunks[2]), row] + wspecs + [row],
        out_specs=out_specs,
        out_shape=out_shape,
        scratch_shapes=[pltpu.VMEM((tm, D_MODEL), F32 if emit else BF16)],
        compiler_params=_params(("parallel", "arbitrary")),
        name=name,
    )(x, mod, mod, mod, g.reshape(1, D_MODEL), wg, wu, wd, gf.reshape(1, D_MODEL))


def _inproj_body(x_ref, sh_ref, sc_ref, g_ref, w_ref, o_ref, h_ref, *, tiles_per_batch, per_token):
    i = pl.program_id(0)
    j = pl.program_id(1)
    tm = x_ref.shape[0]
    batch = i // tiles_per_batch

    @pl.when(j == 0)
    def _():
        g = g_ref[...]

        def body(r, carry):
            rows = pl.ds(pl.multiple_of(r * NORM_ROWS, NORM_ROWS), NORM_ROWS)
            h = _normmod(x_ref[rows, :], g,
                         _mod_rows(sc_ref, rows, batch, per_token),
                         _mod_rows(sh_ref, rows, batch, per_token))
            h_ref[rows, :] = h.astype(BF16)
            return carry

        lax.fori_loop(0, tm // NORM_ROWS, body, 0)

    o_ref[...] = jnp.dot(h_ref[...], w_ref[...].astype(BF16), preferred_element_type=F32)


def _inproj(x, mod, chunks, g, w_in, *, tm, tiles_per_batch, per_token, name):
    t = x.shape[0]
    sh_c, sc_c = chunks
    body = functools.partial(_inproj_body, tiles_per_batch=tiles_per_batch, per_token=per_token)

    def mspec(c):
        if per_token:
            return pl.BlockSpec((tm, D_MODEL), lambda i, j: (i, c))
        return pl.BlockSpec((MOD_PROMPT_ROWS, D_MODEL), lambda i, j: (0, c))

    return pl.pallas_call(
        body,
        grid=(t // tm, IN_COLS // PROJ_TN),
        in_specs=[
            pl.BlockSpec((tm, D_MODEL), lambda i, j: (i, 0)),
            mspec(sh_c), mspec(sc_c),
            pl.BlockSpec((1, D_MODEL), lambda i, j: (0, 0)),
            pl.BlockSpec((D_MODEL, PROJ_TN), lambda i, j: (0, j)),
        ],
        out_specs=pl.BlockSpec((tm, PROJ_TN), lambda i, j: (i, j)),
        out_shape=jax.ShapeDtypeStruct((t, IN_COLS), F32),
        scratch_shapes=[pltpu.VMEM((tm, D_MODEL), BF16)],
        compiler_params=_params(("parallel", "arbitrary")),
        name=name,
    )(x, mod, mod, g.reshape(1, D_MODEL), w_in)


def _outproj_body(m_ref, x_ref, gt_ref, w_ref, o_ref, *, tiles_per_batch, per_token):
    i = pl.program_id(0)
    batch = i // tiles_per_batch
    d = jnp.dot(m_ref[...].astype(BF16), w_ref[...].astype(BF16), preferred_element_type=F32)
    gt = gt_ref[...] if per_token else gt_ref[pl.ds(batch, 1), :]
    o_ref[...] = x_ref[...] + gt * d


def _outproj(mix, x, mod, gt_c, w_out, *, tm, tiles_per_batch, per_token, name):
    t = x.shape[0]
    nn = D_MODEL // PROJ_TN
    body = functools.partial(_outproj_body, tiles_per_batch=tiles_per_batch, per_token=per_token)
    if per_token:
        gspec = pl.BlockSpec((tm, PROJ_TN), lambda i, j: (i, gt_c * nn + j))
    else:
        gspec = pl.BlockSpec((MOD_PROMPT_ROWS, PROJ_TN), lambda i, j: (0, gt_c * nn + j))
    return pl.pallas_call(
        body,
        grid=(t // tm, nn),
        in_specs=[
            pl.BlockSpec((tm, D_MODEL), lambda i, j: (i, 0)),
            pl.BlockSpec((tm, PROJ_TN), lambda i, j: (i, j)),
            gspec,
            pl.BlockSpec((D_MODEL, PROJ_TN), lambda i, j: (0, j)),
        ],
        out_specs=pl.BlockSpec((tm, PROJ_TN), lambda i, j: (i, j)),
        out_shape=jax.ShapeDtypeStruct((t, D_MODEL), F32),
        compiler_params=_params(("parallel", "arbitrary")),
        name=name,
    )(mix, x, mod, w_out)


def _rope(x, cos, sin_signed):
    return x * cos + pltpu.roll(x, HEAD_DIM // 2, 1) * sin_signed


def _lru_gates(xc_ref, a_ref, wa_ref, ba_ref, wx_ref, bx_ref, lam_ref):
    nlam = -lam_ref[...]
    softplus = jnp.maximum(nlam, 0.0) + jnp.log1p(jnp.exp(-jnp.abs(nlam)))
    rate = -LRU_C * softplus
    for hh in range(LRU_HEADS):
        cols = slice(hh * LRU_BLOCK, (hh + 1) * LRU_BLOCK)
        xc = xc_ref[:, cols]
        xcb = xc.astype(BF16)
        ra = jnp.dot(xcb, wa_ref[hh].astype(BF16), preferred_element_type=F32) + ba_ref[:, cols]
        rx = jnp.dot(xcb, wx_ref[hh].astype(BF16), preferred_element_type=F32) + bx_ref[:, cols]
        r = _sigmoid(ra)
        gi = _sigmoid(rx)
        a = jnp.exp(r * rate[:, cols])
        a_ref[:, cols] = a
        xc_ref[:, cols] = jnp.sqrt(1.0 - a * a) * (gi * xc)


def _scan_group(a, u, carry, row):
    for s in (1, 2, 4):
        a_sh = pltpu.roll(a, s, 0)
        u_sh = pltpu.roll(u, s, 0)
        m = row >= s
        u = jnp.where(m, a * u_sh + u, u)
        a = jnp.where(m, a * a_sh, a)
    return a * carry + u


def _softmax_pv(s, mask, sink, v):
    s = jnp.where(mask, s, NEG)
    m = jnp.maximum(jnp.max(s, axis=-1, keepdims=True), sink)
    p = jnp.exp(s - m)
    den = jnp.sum(p, axis=-1, keepdims=True) + jnp.exp(sink - m)
    return jnp.dot(p.astype(BF16), v, preferred_element_type=F32) / den


def _sink_column(sink_ref, kh, rows_per_head):
    n = GROUP * rows_per_head
    ri = lax.broadcasted_iota(jnp.int32, (n, 1), 0)
    col = jnp.full((n, 1), sink_ref[kh * GROUP + GROUP - 1], F32)
    for g in range(GROUP - 2, -1, -1):
        col = jnp.where(ri < (g + 1) * rows_per_head, sink_ref[kh * GROUP + g], col)
    return col


def _mix_prompt_body(xb_ref, yb_ref, q_ref, kv_ref, cs_ref, sn_ref, cw_ref, cb_ref,
                     wa_ref, ba_ref, wx_ref, bx_ref, lam_ref, sink_ref,
                     mix_ref, lru_ref, k_ref,
                     xpad, hc, kpad, vpad, a_s, u_s, *, ts):
    t = pl.program_id(1)

    @pl.when(t == 0)
    def _():
        xpad[0:SUBLANES, :] = jnp.zeros((SUBLANES, LRU_WIDTH), F32)
        hc[...] = jnp.zeros((SUBLANES, LRU_WIDTH), F32)
        kpad[0:WINDOW, :] = jnp.zeros((WINDOW, KV_WIDTH), BF16)
        vpad[0:WINDOW, :] = jnp.zeros((WINDOW, KV_WIDTH), BF16)

    xpad[SUBLANES:SUBLANES + ts, :] = xb_ref[...]
    xc = cb_ref[...]
    for jj in range(CONV_W):
        off = SUBLANES - (CONV_W - 1) + jj
        xc = xc + xpad[off:off + ts, :] * cw_ref[jj:jj + 1, :]
    u_s[...] = xc
    xpad[0:SUBLANES, :] = xpad[ts:ts + SUBLANES, :]

    _lru_gates(u_s, a_s, wa_ref, ba_ref, wx_ref, bx_ref, lam_ref)

    row = lax.broadcasted_iota(jnp.int32, (SUBLANES, LRU_WIDTH), 0)

    def scan_body(g, carry):
        rows = pl.ds(pl.multiple_of(g * SUBLANES, SUBLANES), SUBLANES)
        h = _scan_group(a_s[rows, :], u_s[rows, :], carry, row)
        a_s[rows, :] = h
        return jnp.broadcast_to(h[SUBLANES - 1:SUBLANES, :], (SUBLANES, LRU_WIDTH))

    carry = lax.fori_loop(0, ts // SUBLANES, scan_body, hc[...], unroll=4)
    hc[...] = carry
    lru_ref[0] = carry[0:1, :]
    mix_ref[:, 0:LRU_WIDTH] = (a_s[...] * jax.nn.gelu(yb_ref[...])).astype(BF16)

    cos = cs_ref[...]
    sin = sn_ref[...]
    for kh in range(N_KV_HEADS):
        cols = slice(kh * HEAD_DIM, (kh + 1) * HEAD_DIM)
        kr = _rope(kv_ref[:, cols], cos, sin)
        kpad[WINDOW:WINDOW + ts, cols] = kr.astype(BF16)
        k_ref[0, :, cols] = kr[ts - WINDOW:, :]
    vpad[WINDOW:WINDOW + ts, :] = kv_ref[:, KV_WIDTH:2 * KV_WIDTH].astype(BF16)

    nq = GROUP * WINDOW
    r_i = lax.broadcasted_iota(jnp.int32, (nq, 2 * WINDOW), 0) & (WINDOW - 1)
    c_i = lax.broadcasted_iota(jnp.int32, (nq, 2 * WINDOW), 1)
    band = (c_i >= r_i) & (c_i <= r_i + WINDOW)
    for n in range(ts // WINDOW):
        qrows = slice(n * WINDOW, (n + 1) * WINDOW)
        kpos0 = t * ts + (n - 1) * WINDOW
        mask = band & (c_i + kpos0 >= 0)
        for kh in range(N_KV_HEADS):
            cols = slice(kh * HEAD_DIM, (kh + 1) * HEAD_DIM)
            qs = []
            for g in range(GROUP):
                hd = kh * GROUP + g
                qh = q_ref[qrows, hd * HEAD_DIM:(hd + 1) * HEAD_DIM]
                qs.append(_rope(qh, cos[qrows, :], sin[qrows, :]).astype(BF16))
            q4 = jnp.concatenate(qs, axis=0)
            kk = kpad[n * WINDOW:(n + 2) * WINDOW, cols]
            vv = vpad[n * WINDOW:(n + 2) * WINDOW, cols]
            s = lax.dot_general(q4, kk, (((1,), (1,)), ((), ())), preferred_element_type=F32) * SCALE
            o = _softmax_pv(s, mask, _sink_column(sink_ref, kh, WINDOW), vv)
            for g in range(GROUP):
                hd = kh * GROUP + g
                mix_ref[qrows, LRU_WIDTH + hd * HEAD_DIM:LRU_WIDTH + (hd + 1) * HEAD_DIM] = (
                    o[g * WINDOW:(g + 1) * WINDOW, :].astype(BF16))

    kpad[0:WINDOW, :] = kpad[ts:ts + WINDOW, :]
    vpad[0:WINDOW, :] = vpad[ts:ts + WINDOW, :]


def _mix_prompt(proj, cos, sin, p, *, batch, seq):
    ts = MIX_TS
    nt = seq // ts
    body = functools.partial(_mix_prompt_body, ts=ts)
    w = LRU_WIDTH

    def tile(width, col_block):
        return pl.BlockSpec((ts, width), lambda b, t: (b * nt + t, col_block))

    def whole(shape):
        return pl.BlockSpec(shape, lambda b, t: (0,) * len(shape))

    return pl.pallas_call(
        body,
        grid=(batch, nt),
        in_specs=[
            tile(w, 0), tile(w, 1), tile(ATTN_WIDTH, Q_COL // ATTN_WIDTH),
            tile(2 * KV_WIDTH, K_COL // (2 * KV_WIDTH)),
            pl.BlockSpec((ts, HEAD_DIM), lambda b, t: (t, 0)),
            pl.BlockSpec((ts, HEAD_DIM), lambda b, t: (t, 0)),
            whole((CONV_W, w)), whole((1, w)),
            whole((LRU_HEADS, LRU_BLOCK, LRU_BLOCK)), whole((1, w)),
            whole((LRU_HEADS, LRU_BLOCK, LRU_BLOCK)), whole((1, w)),
            whole((1, w)),
            pl.BlockSpec(memory_space=pltpu.SMEM),
        ],
        out_specs=[
            pl.BlockSpec((ts, D_MODEL), lambda b, t: (b * nt + t, 0)),
            pl.BlockSpec((1, 1, w), lambda b, t: (b, 0, 0)),
            pl.BlockSpec((1, WINDOW, KV_WIDTH), lambda b, t: (b, 0, 0)),
        ],
        out_shape=[
            jax.ShapeDtypeStruct((batch * seq, D_MODEL), BF16),
            jax.ShapeDtypeStruct((batch, 1, w), F32),
            jax.ShapeDtypeStruct((batch, WINDOW, KV_WIDTH), F32),
        ],
        scratch_shapes=[
            pltpu.VMEM((SUBLANES + ts, w), F32),
            pltpu.VMEM((SUBLANES, w), F32),
            pltpu.VMEM((WINDOW + ts, KV_WIDTH), BF16),
            pltpu.VMEM((WINDOW + ts, KV_WIDTH), BF16),
            pltpu.VMEM((ts, w), F32),
            pltpu.VMEM((ts, w), F32),
        ],
        compiler_params=_params(("parallel", "arbitrary")),
        name="mix_prompt",
    )(proj, proj, proj, proj, cos, sin, p["conv_w"], p["conv_b"], p["wa"], p["ba"], p["wx"], p["bx"],
      p["lam"], p["sinks"])


def _mix_sample_body(proj_ref, prev_ref, h0_ref, ck_ref, cv_ref, cs_ref, sn_ref, cw_ref, cb_ref,
                     wa_ref, ba_ref, wx_ref, bx_ref, lam_ref, sink_ref,
                     mix_ref, lru_ref, knew_ref,
                     a_s, u_s, q_s, *, batch, seq):
    w = LRU_WIDTH
    row = lax.broadcasted_iota(jnp.int32, (SUBLANES, w), 0)

    def group(b):
        return pl.ds(pl.multiple_of(b * SUBLANES, SUBLANES), SUBLANES)

    def conv_body(b, carry):
        rows = group(b)
        cur = proj_ref[rows, 0:w]
        prev = prev_ref[rows, :]
        xc = cb_ref[...]
        for jj in range(CONV_W):
            d = CONV_W - 1 - jj
            if d == 0:
                term = cur
            else:
                term = jnp.where(row >= d, pltpu.roll(cur, d, 0), pltpu.roll(prev, d, 0))
            xc = xc + term * cw_ref[jj:jj + 1, :]
        u_s[rows, :] = xc
        return carry

    lax.fori_loop(0, batch, conv_body, 0)

    _lru_gates(u_s, a_s, wa_ref, ba_ref, wx_ref, bx_ref, lam_ref)

    def scan_body(b, carry):
        rows = group(b)
        h0 = jnp.broadcast_to(h0_ref[pl.ds(b, 1), :], (SUBLANES, w))
        h = _scan_group(a_s[rows, :], u_s[rows, :], h0, row)
        a_s[rows, :] = h
        lru_ref[pl.ds(b, 1), :] = h[SUBLANES - 1:SUBLANES, :]
        return carry

    lax.fori_loop(0, batch, scan_body, 0)
    mix_ref[:, 0:w] = a_s[...] * jax.nn.gelu(proj_ref[:, w:2 * w])

    cos = cs_ref[...]
    sin = sn_ref[...]
    for hd in range(N_HEADS):
        cols = slice(hd * HEAD_DIM, (hd + 1) * HEAD_DIM)
        q_s[:, cols] = _rope(proj_ref[:, Q_COL + hd * HEAD_DIM:Q_COL + (hd + 1) * HEAD_DIM], cos, sin)
    for kh in range(N_KV_HEADS):
        cols = slice(kh * HEAD_DIM, (kh + 1) * HEAD_DIM)
        knew_ref[:, cols] = _rope(proj_ref[:, K_COL + kh * HEAD_DIM:K_COL + (kh + 1) * HEAD_DIM], cos, sin)

    nq = GROUP * seq
    nk = 2 * WINDOW
    r_i = lax.broadcasted_iota(jnp.int32, (nq, nk), 0) & (seq - 1)
    c_i = lax.broadcasted_iota(jnp.int32, (nq, nk), 1)
    mask = (c_i >= r_i) & (c_i <= r_i + WINDOW)
    pad = jnp.zeros((nk - WINDOW - seq, HEAD_DIM), F32)

    def attn_body(b, carry):
        rows = group(b)
        for kh in range(N_KV_HEADS):
            cols = slice(kh * HEAD_DIM, (kh + 1) * HEAD_DIM)
            q4 = jnp.concatenate(
                [q_s[rows, (kh * GROUP + g) * HEAD_DIM:(kh * GROUP + g + 1) * HEAD_DIM] for g in range(GROUP)],
                axis=0).astype(BF16)
            kk = jnp.concatenate([ck_ref[b, :, cols], knew_ref[rows, cols], pad], axis=0).astype(BF16)
            vv = jnp.concatenate(
                [cv_ref[b, :, cols], proj_ref[rows, V_COL + kh * HEAD_DIM:V_COL + (kh + 1) * HEAD_DIM], pad],
                axis=0).astype(BF16)
            s = lax.dot_general(q4, kk, (((1,), (1,)), ((), ())), preferred_element_type=F32) * SCALE
            o = _softmax_pv(s, mask, _sink_column(sink_ref, kh, seq), vv)
            for g in range(GROUP):
                hd = kh * GROUP + g
                mix_ref[rows, w + hd * HEAD_DIM:w + (hd + 1) * HEAD_DIM] = o[g * seq:(g + 1) * seq, :]
        return carry

    lax.fori_loop(0, batch, attn_body, 0)


def _mix_sample(proj, prev, h0, cache_k, cache_v, cos, sin, p, *, batch, seq):
    assert seq == SUBLANES, "each sample batch must be exactly one sublane group"
    t = batch * seq
    body = functools.partial(_mix_sample_body, batch=batch, seq=seq)
    vmem = pl.BlockSpec(memory_space=pltpu.VMEM)
    return pl.pallas_call(
        body,
        in_specs=[vmem] * 14 + [pl.BlockSpec(memory_space=pltpu.SMEM)],
        out_specs=[vmem, vmem, vmem],
        out_shape=[
            jax.ShapeDtypeStruct((t, D_MODEL), F32),
            jax.ShapeDtypeStruct((batch, LRU_WIDTH), F32),
            jax.ShapeDtypeStruct((t, KV_WIDTH), F32),
        ],
        scratch_shapes=[
            pltpu.VMEM((t, LRU_WIDTH), F32),
            pltpu.VMEM((t, LRU_WIDTH), F32),
            pltpu.VMEM((t, ATTN_WIDTH), F32),
        ],
        compiler_params=pltpu.CompilerParams(vmem_limit_bytes=V7X_VMEM_LIMIT_BYTES),
        name="mix_sample",
    )(proj, prev, h0, cache_k, cache_v, cos, sin, p["conv_w"], p["conv_b"], p["wa"], p["ba"], p["wx"], p["bx"],
      p["lam"], p["sinks"])


def _layer(w, l):
    return w.reshape(w.shape[1:]) if w.shape[0] == 1 else w[l]


def _rope_tables(pos):
    half = HEAD_DIM // 2
    inv = ROPE_THETA ** (-jnp.arange(half, dtype=F32) / half)
    ang = pos.astype(F32)[:, None] * inv[None, :]
    cos = jnp.cos(ang)
    sin = jnp.sin(ang)
    return jnp.concatenate([cos, cos], axis=-1), jnp.concatenate([-sin, sin], axis=-1)


def kernel(x_prompt, x_sample, c_prompt, c_sample, state_conv, state_lru, cache_k_win, cache_v_win, ada_w, ada_b, norm_ffn1, norm_mix, norm_ffn2, ffn1_w_gate, ffn1_w_up, ffn1_w_down, w_in, conv_w, conv_b, lru_w_a, lru_b_a, lru_w_x, lru_b_x, lru_lambda, attn_sinks, w_out, ffn2_w_gate, ffn2_w_up, ffn2_w_down, norm_final):
    bp, sp, _ = x_prompt.shape
    bs, ss, _ = x_sample.shape
    depth = ada_w.shape[0]
    tp, tsn = bp * sp, bs * ss
    assert bp <= MOD_PROMPT_ROWS and bp + bs <= MOD_ROWS
    tm_p = 1024
    assert sp % tm_p == 0
    tpb = sp // tm_p

    xp = x_prompt.reshape(tp, D_MODEL)
    xs = x_sample.reshape(tsn, D_MODEL)
    c_all = jnp.concatenate(
        [c_prompt, jnp.zeros((MOD_PROMPT_ROWS - bp, D_MODEL), F32), c_sample,
         jnp.zeros((MOD_ROWS - MOD_PROMPT_ROWS - bs, D_MODEL), F32)], axis=0)
    cos_p, sin_p = _rope_tables(jnp.arange(sp))
    cos_s, sin_s = _rope_tables(PAST_LEN + jnp.arange(ss))
    cos_s = jnp.tile(cos_s, (bs, 1))
    sin_s = jnp.tile(sin_s, (bs, 1))

    outs_p, outs_s = [], []
    for l in range(depth):
        last = l == depth - 1
        L = functools.partial(_layer, l=l)
        cache_k, cache_v = L(cache_k_win), L(cache_v_win)
        mod = _adaln(c_all, L(ada_w), L(ada_b))
        mod_s = jnp.repeat(mod[MOD_PROMPT_ROWS:MOD_PROMPT_ROWS + bs], ss, axis=0)
        p = dict(conv_w=L(conv_w), conv_b=L(conv_b).reshape(1, LRU_WIDTH), wa=L(lru_w_a),
                 ba=L(lru_b_a).reshape(1, LRU_WIDTH), wx=L(lru_w_x), bx=L(lru_b_x).reshape(1, LRU_WIDTH),
                 lam=L(lru_lambda).reshape(1, LRU_WIDTH), sinks=L(attn_sinks))
        ffn1 = (L(norm_ffn1), L(ffn1_w_gate), L(ffn1_w_up), L(ffn1_w_down), norm_final)
        ffn2 = (L(norm_ffn2), L(ffn2_w_gate), L(ffn2_w_up), L(ffn2_w_down), norm_final)
        prompt = dict(tm=tm_p, tiles_per_batch=tpb, per_token=False)
        sample = dict(tm=tsn, tiles_per_batch=1, per_token=True)

        ffn_p = dict(tm=tm_p, nr=FFN_ROWS, row0=0, rows_per_batch=sp, emit=False)
        ffn_s = dict(tm=tsn, nr=ss, row0=MOD_PROMPT_ROWS, rows_per_batch=ss, emit=True)

        xs, wg_b, wu_b, wd_b = _ffn(xs, mod, (0, 1, 2), *ffn1, final_norm=False, name="ffn1_sample", **ffn_s)
        xp, = _ffn(xp, mod, (0, 1, 2), ffn1[0], wg_b, wu_b, wd_b, norm_final, final_norm=False,
                   name="ffn1_prompt", **ffn_p)

        proj_p = _inproj(xp, mod, (3, 4), L(norm_mix), L(w_in), name="inproj_prompt", **prompt)
        proj_s = _inproj(xs, mod_s, (3, 4), L(norm_mix), L(w_in), name="inproj_sample", **sample)
        mix_p, lru_p, k_p = _mix_prompt(proj_p, cos_p, sin_p, p, batch=bp, seq=sp)
        prev_s = jnp.pad(L(state_conv), ((0, 0), (SUBLANES - (CONV_W - 1), 0), (0, 0))).reshape(tsn, LRU_WIDTH)
        mix_s, lru_s, knew_s = _mix_sample(
            proj_s, prev_s, L(state_lru), cache_k.reshape(bs, WINDOW, KV_WIDTH),
            cache_v.reshape(bs, WINDOW, KV_WIDTH), cos_s, sin_s, p, batch=bs, seq=ss)
        xp = _outproj(mix_p, xp, mod, 5, L(w_out), name="outproj_prompt", **prompt)
        xs = _outproj(mix_s, xs, mod_s, 5, L(w_out), name="outproj_sample", **sample)

        xs, wg_b, wu_b, wd_b = _ffn(xs, mod, (6, 7, 8), *ffn2, final_norm=last, name="ffn2_sample", **ffn_s)
        xp, = _ffn(xp, mod, (6, 7, 8), ffn2[0], wg_b, wu_b, wd_b, norm_final, final_norm=last,
                   name="ffn2_prompt", **ffn_p)

        pp = proj_p.reshape(bp, sp, IN_COLS)
        ps = proj_s.reshape(bs, ss, IN_COLS)
        outs_p.append((
            pp[:, sp - (CONV_W - 1):, 0:LRU_WIDTH],
            lru_p.reshape(bp, LRU_WIDTH),
            k_p.reshape(bp, WINDOW, N_KV_HEADS, HEAD_DIM),
            pp[:, sp - WINDOW:, V_COL:].reshape(bp, WINDOW, N_KV_HEADS, HEAD_DIM),
        ))
        k_all = jnp.concatenate([cache_k, knew_s.reshape(bs, ss, N_KV_HEADS, HEAD_DIM)], axis=1)
        v_all = jnp.concatenate([cache_v, ps[:, :, V_COL:].reshape(bs, ss, N_KV_HEADS, HEAD_DIM)], axis=1)
        outs_s.append((
            ps[:, ss - (CONV_W - 1):, 0:LRU_WIDTH],
            lru_s,
            k_all[:, -WINDOW:],
            v_all[:, -WINDOW:],
        ))

    y_prompt = xp.reshape(bp, sp, D_MODEL)
    y_sample = xs.reshape(bs, ss, D_MODEL)
    st_p = [jnp.stack([o[i] for o in outs_p]) for i in range(4)]
    st_s = [jnp.stack([o[i] for o in outs_s]) for i in range(4)]
    return (y_prompt, y_sample, st_p[0], st_p[1], st_p[2], st_p[3], st_s[0], st_s[1], st_s[2], st_s[3])
```

```python
import functools

import jax
import jax.numpy as jnp
from jax import lax
from jax.experimental import pallas as pl
from jax.experimental.pallas import tpu as pltpu

F32 = jnp.float32
BF16 = jnp.bfloat16

D_MODEL = 2048
D_FF = 5504
LRU_WIDTH = 1024
LRU_HEADS = 8
LRU_BLOCK = 128
CONV_W = 4
LRU_C = 8.0
HEAD_DIM = 128
N_HEADS = 8
N_KV_HEADS = 2
GROUP = N_HEADS // N_KV_HEADS
ATTN_WIDTH = N_HEADS * HEAD_DIM
KV_WIDTH = N_KV_HEADS * HEAD_DIM
WINDOW = 128
ROPE_THETA = 10000.0
N_MOD = 9
EPS = 1e-6
NEG = -1e30
PAST_LEN = 16384
IN_COLS = 2 * LRU_WIDTH + ATTN_WIDTH + 2 * KV_WIDTH
Q_COL = 2 * LRU_WIDTH
K_COL = Q_COL + ATTN_WIDTH
V_COL = K_COL + KV_WIDTH
SCALE = HEAD_DIM ** -0.5

SUBLANES = 8
LANES = 128
V7X_VMEM_LIMIT_BYTES = 56 * 1024 * 1024

MOD_ROWS = 48
MOD_PROMPT_ROWS = 8
ADALN_TN = 1024
FFN_TF = 512
FFN_ROWS = 16
FFN_EPILOGUE_ROWS = 128
PROJ_TN = 512
OUT_TM = 512
MIX_TS = 256


def _params(sem, vmem=V7X_VMEM_LIMIT_BYTES):
    return pltpu.CompilerParams(dimension_semantics=sem, vmem_limit_bytes=vmem)


def _sigmoid(x):
    return 1.0 / (1.0 + jnp.exp(-x))


def _normmod(x, g, sc, sh):
    ms = jnp.mean(x * x, axis=-1, keepdims=True)
    xn = x * lax.rsqrt(ms + EPS) * g
    return xn * (1.0 + sc) + sh


def _mod_row(ref, token, row0, batch_shift):
    return ref[pl.ds(row0 + lax.shift_right_logical(token, batch_shift), 1), :]


def _mod_spec(chunk):
    return pl.BlockSpec((MOD_ROWS, D_MODEL), lambda i, j: (0, chunk))


def _adaln_body(c_ref, w_ref, b_ref, o_ref):
    c = c_ref[...]
    s = (c * _sigmoid(c)).astype(BF16)
    o_ref[...] = jnp.dot(s, w_ref[...].astype(BF16), preferred_element_type=F32) + b_ref[...]


def _adaln(c_all, ada_w, ada_b):
    n = ada_w.shape[1]
    return pl.pallas_call(
        _adaln_body,
        grid=(n // ADALN_TN,),
        in_specs=[
            pl.BlockSpec((MOD_ROWS, D_MODEL), lambda j: (0, 0)),
            pl.BlockSpec((D_MODEL, ADALN_TN), lambda j: (0, j)),
            pl.BlockSpec((1, ADALN_TN), lambda j: (0, j)),
        ],
        out_specs=pl.BlockSpec((MOD_ROWS, ADALN_TN), lambda j: (0, j)),
        out_shape=jax.ShapeDtypeStruct((MOD_ROWS, n), F32),
        compiler_params=_params(("arbitrary",)),
        name="adaln",
    )(c_all, ada_w, ada_b.reshape(1, n))


def _ffn_body(x_ref, sh_ref, sc_ref, gt_ref, g_ref, wg_ref, wu_ref, wd_ref, gf_ref, *rest,
              nr, er, row0, batch_shift, final_norm, emit):
    if emit:
        o_ref, wgb_ref, wub_ref, wdb_ref, h_ref = rest
    else:
        o_ref, h_ref = rest
    i = pl.program_id(0)
    j = pl.program_id(1)
    nj = pl.num_programs(1)
    tm = x_ref.shape[0]
    tf = wd_ref.shape[0]
    nchunks = tm // nr

    def chunk(r):
        return pl.ds(pl.multiple_of(r * nr, nr), nr)

    def mod_row(ref, r):
        return _mod_row(ref, i * tm + r * nr, row0, batch_shift)

    @pl.when(j == 0)
    def _():
        g = g_ref[...]

        def body(r, carry):
            rows = chunk(r)
            h = _normmod(x_ref[rows, :], g, mod_row(sc_ref, r), mod_row(sh_ref, r))
            h_ref[rows, :] = h.astype(h_ref.dtype)
            o_ref[rows, :] = jnp.zeros((nr, D_MODEL), F32)
            return carry

        lax.fori_loop(0, nchunks, body, 0, unroll=4)

    if emit:
        valid = D_FF - j * tf
        cmask = lax.broadcasted_iota(jnp.int32, (1, tf), 1) < valid
        rmask = lax.broadcasted_iota(jnp.int32, (tf, 1), 0) < valid
        wg = jnp.where(cmask, wg_ref[...], 0.0).astype(BF16)
        wu = jnp.where(cmask, wu_ref[...], 0.0).astype(BF16)
        wd = jnp.where(rmask, wd_ref[...], 0.0).astype(BF16)
        wgb_ref[...] = wg
        wub_ref[...] = wu
        wdb_ref[...] = wd
    else:
        wg, wu, wd = wg_ref[...], wu_ref[...], wd_ref[...]

    h = h_ref[...].astype(BF16)
    g = jnp.dot(h, wg, preferred_element_type=F32)
    u = jnp.dot(h, wu, preferred_element_type=F32)
    a = (g * _sigmoid(g) * u).astype(BF16)
    o_ref[...] += jnp.dot(a, wd, preferred_element_type=F32)

    @pl.when(j == nj - 1)
    def _():
        gf = gf_ref[...]

        def body(r, carry):
            rows = pl.ds(pl.multiple_of(r * er, er), er)
            gt = _mod_row(gt_ref, i * tm + r * er, row0, batch_shift)
            y = x_ref[rows, :] + (0.5 * gt) * o_ref[rows, :]
            if final_norm:
                ms = jnp.mean(y * y, axis=-1, keepdims=True)
                y = y * lax.rsqrt(ms + EPS) * gf
            o_ref[rows, :] = y
            return carry

        lax.fori_loop(0, tm // er, body, 0)


def _ffn(x, mod, chunks, g, wg, wu, wd, gf, *, tm, nr, row0, rows_per_batch, final_norm, emit, name):
    t = x.shape[0]
    tf = FFN_TF
    nj = pl.cdiv(D_FF, tf)
    er = min(FFN_EPILOGUE_ROWS, rows_per_batch)
    assert rows_per_batch & (rows_per_batch - 1) == 0 and rows_per_batch % nr == 0 and tm % nr == 0
    assert rows_per_batch % er == 0 and tm % er == 0
    body = functools.partial(_ffn_body, nr=nr, er=er, row0=row0, batch_shift=rows_per_batch.bit_length() - 1,
                             final_norm=final_norm, emit=emit)

    mspec = _mod_spec
    row = pl.BlockSpec((1, D_MODEL), lambda i, j: (0, 0))
    if emit:
        wspecs = [pl.BlockSpec((D_MODEL, tf), lambda i, j: (0, j)),
                  pl.BlockSpec((D_MODEL, tf), lambda i, j: (0, j)),
                  pl.BlockSpec((tf, D_MODEL), lambda i, j: (j, 0))]
    else:
        wspecs = [pl.BlockSpec((None, D_MODEL, tf), lambda i, j: (j, 0, 0)),
                  pl.BlockSpec((None, D_MODEL, tf), lambda i, j: (j, 0, 0)),
                  pl.BlockSpec((tf, D_MODEL), lambda i, j: (j, 0))]
    out_specs = [pl.BlockSpec((tm, D_MODEL), lambda i, j: (i, 0))]
    out_shape = [jax.ShapeDtypeStruct((t, D_MODEL), F32)]
    if emit:
        assert t == tm, "bf16 weight tiles are written once, by a single token tile"
        out_specs += [pl.BlockSpec((None, D_MODEL, tf), lambda i, j: (j, 0, 0)),
                      pl.BlockSpec((None, D_MODEL, tf), lambda i, j: (j, 0, 0)),
                      pl.BlockSpec((tf, D_MODEL), lambda i, j: (j, 0))]
        out_shape += [jax.ShapeDtypeStruct((nj, D_MODEL, tf), BF16),
                      jax.ShapeDtypeStruct((nj, D_MODEL, tf), BF16),
                      jax.ShapeDtypeStruct((nj * tf, D_MODEL), BF16)]
    return pl.pallas_call(
        body,
        grid=(t // tm, nj),
        in_specs=[pl.BlockSpec((tm, D_MODEL), lambda i, j: (i, 0)),
                  mspec(chunks[0]), mspec(chunks[1]), mspec(chunks[2]), row] + wspecs + [row],
        out_specs=out_specs,
        out_shape=out_shape,
        scratch_shapes=[pltpu.VMEM((tm, D_MODEL), F32 if emit else BF16)],
        compiler_params=_params(("parallel", "arbitrary")),
        name=name,
    )(x, mod, mod, mod, g.reshape(1, D_MODEL), wg, wu, wd, gf.reshape(1, D_MODEL))


def _inproj_body(x_ref, sh_ref, sc_ref, g_ref, w_ref, *rest, nr, row0, batch_shift, emit):
    if emit:
        o_ref, wb_ref, h_ref = rest
    else:
        o_ref, h_ref = rest
    i = pl.program_id(0)
    j = pl.program_id(1)
    tm = x_ref.shape[0]

    @pl.when(j == 0)
    def _():
        g = g_ref[...]

        def body(r, carry):
            rows = pl.ds(pl.multiple_of(r * nr, nr), nr)
            tok = i * tm + r * nr
            h = _normmod(x_ref[rows, :], g, _mod_row(sc_ref, tok, row0, batch_shift),
                         _mod_row(sh_ref, tok, row0, batch_shift))
            h_ref[rows, :] = h.astype(h_ref.dtype)
            return carry

        lax.fori_loop(0, tm // nr, body, 0, unroll=4)

    w = w_ref[...].astype(BF16)
    if emit:
        wb_ref[...] = w
    o_ref[...] = jnp.dot(h_ref[...].astype(BF16), w, preferred_element_type=F32)


def _inproj(x, mod, chunks, g, w_in, *, tm, tn, nr, row0, rows_per_batch, emit, name):
    t = x.shape[0]
    assert rows_per_batch & (rows_per_batch - 1) == 0 and rows_per_batch % nr == 0 and tm % nr == 0
    body = functools.partial(_inproj_body, nr=nr, row0=row0, batch_shift=rows_per_batch.bit_length() - 1,
                             emit=emit)
    wspec = pl.BlockSpec((D_MODEL, tn), lambda i, j: (0, j))
    out_specs = [pl.BlockSpec((tm, tn), lambda i, j: (i, j))]
    out_shape = [jax.ShapeDtypeStruct((t, IN_COLS), F32)]
    if emit:
        assert t == tm, "the bf16 weight copy is written once, by a single token tile"
        out_specs.append(wspec)
        out_shape.append(jax.ShapeDtypeStruct((D_MODEL, IN_COLS), BF16))
    return pl.pallas_call(
        body,
        grid=(t // tm, IN_COLS // tn),
        in_specs=[
            pl.BlockSpec((tm, D_MODEL), lambda i, j: (i, 0)),
            _mod_spec(chunks[0]), _mod_spec(chunks[1]),
            pl.BlockSpec((1, D_MODEL), lambda i, j: (0, 0)),
            wspec,
        ],
        out_specs=out_specs,
        out_shape=out_shape,
        scratch_shapes=[pltpu.VMEM((tm, D_MODEL), F32 if emit else BF16)],
        compiler_params=_params(("parallel", "arbitrary")),
        name=name,
    )(x, mod, mod, g.reshape(1, D_MODEL), w_in)


def _outproj_body(m_ref, x_ref, gt_ref, w_ref, *rest, nr, row0, batch_shift, emit):
    if emit:
        o_ref, wb_ref = rest
    else:
        o_ref, = rest
    i = pl.program_id(0)
    tm = x_ref.shape[0]
    w = w_ref[...].astype(BF16)
    if emit:
        wb_ref[...] = w
    d = jnp.dot(m_ref[...].astype(BF16), w, preferred_element_type=F32)
    if nr == tm:
        o_ref[...] = x_ref[...] + _mod_row(gt_ref, i * tm, row0, batch_shift) * d
    else:
        o_ref[...] = d

        def body(r, carry):
            rows = pl.ds(pl.multiple_of(r * nr, nr), nr)
            gt = _mod_row(gt_ref, i * tm + r * nr, row0, batch_shift)
            o_ref[rows, :] = x_ref[rows, :] + gt * o_ref[rows, :]
            return carry

        lax.fori_loop(0, tm // nr, body, 0)


def _outproj(mix, x, mod, gt_c, w_out, *, tm, tn, nr, row0, rows_per_batch, emit, name):
    t = x.shape[0]
    nn = D_MODEL // tn
    assert rows_per_batch & (rows_per_batch - 1) == 0 and rows_per_batch % nr == 0 and tm % nr == 0
    body = functools.partial(_outproj_body, nr=nr, row0=row0, batch_shift=rows_per_batch.bit_length() - 1,
                             emit=emit)
    wspec = pl.BlockSpec((D_MODEL, tn), lambda i, j: (0, j))
    out_specs = [pl.BlockSpec((tm, tn), lambda i, j: (i, j))]
    out_shape = [jax.ShapeDtypeStruct((t, D_MODEL), F32)]
    if emit:
        assert t == tm, "the bf16 weight copy is written once, by a single token tile"
        out_specs.append(wspec)
        out_shape.append(jax.ShapeDtypeStruct((D_MODEL, D_MODEL), BF16))
    return pl.pallas_call(
        body,
        grid=(t // tm, nn),
        in_specs=[
            pl.BlockSpec((tm, D_MODEL), lambda i, j: (i, 0)),
            pl.BlockSpec((tm, tn), lambda i, j: (i, j)),
            pl.BlockSpec((MOD_ROWS, tn), lambda i, j: (0, gt_c * nn + j)),
            wspec,
        ],
        out_specs=out_specs,
        out_shape=out_shape,
        compiler_params=_params(("parallel", "arbitrary")),
        name=name,
    )(mix, x, mod, w_out)


def _rope(x, cos, sin_signed):
    return x * cos + pltpu.roll(x, HEAD_DIM // 2, 1) * sin_signed


def _lru_gates(xc_ref, a_ref, wa_ref, ba_ref, wx_ref, bx_ref, lam_ref):
    nlam = -lam_ref[...]
    softplus = jnp.maximum(nlam, 0.0) + jnp.log1p(jnp.exp(-jnp.abs(nlam)))
    rate = -LRU_C * softplus
    for hh in range(LRU_HEADS):
        cols = slice(hh * LRU_BLOCK, (hh + 1) * LRU_BLOCK)
        xc = xc_ref[:, cols]
        xcb = xc.astype(BF16)
        ra = jnp.dot(xcb, wa_ref[hh].astype(BF16), preferred_element_type=F32) + ba_ref[:, cols]
        rx = jnp.dot(xcb, wx_ref[hh].astype(BF16), preferred_element_type=F32) + bx_ref[:, cols]
        r = _sigmoid(ra)
        gi = _sigmoid(rx)
        a = jnp.exp(r * rate[:, cols])
        a_ref[:, cols] = a
        xc_ref[:, cols] = jnp.sqrt(1.0 - a * a) * (gi * xc)


def _scan_group(a, u, carry, row):
    for s in (1, 2, 4):
        a_sh = pltpu.roll(a, s, 0)
        u_sh = pltpu.roll(u, s, 0)
        m = row >= s
        u = jnp.where(m, a * u_sh + u, u)
        a = jnp.where(m, a * a_sh, a)
    return a * carry + u


def _softmax_pv(s, mask, sink, v):
    s = jnp.where(mask, s, NEG)
    m = jnp.maximum(jnp.max(s, axis=-1, keepdims=True), sink)
    p = jnp.exp(s - m)
    den = jnp.sum(p, axis=-1, keepdims=True) + jnp.exp(sink - m)
    return jnp.dot(p.astype(BF16), v, preferred_element_type=F32) / den


def _sink_column(sink_ref, kh, rows_per_head):
    n = GROUP * rows_per_head
    ri = lax.broadcasted_iota(jnp.int32, (n, 1), 0)
    col = jnp.full((n, 1), sink_ref[kh * GROUP + GROUP - 1], F32)
    for g in range(GROUP - 2, -1, -1):
        col = jnp.where(ri < (g + 1) * rows_per_head, sink_ref[kh * GROUP + g], col)
    return col


def _mix_prompt_body(xb_ref, yb_ref, q_ref, kv_ref, cs_ref, sn_ref, cw_ref, cb_ref,
                     wa_ref, ba_ref, wx_ref, bx_ref, lam_ref, sink_ref,
                     mix_ref, lru_ref, k_ref,
                     xpad, hc, kpad, vpad, a_s, u_s, *, ts):
    t = pl.program_id(1)

    @pl.when(t == 0)
    def _():
        xpad[0:SUBLANES, :] = jnp.zeros((SUBLANES, LRU_WIDTH), F32)
        hc[...] = jnp.zeros((SUBLANES, LRU_WIDTH), F32)
        kpad[0:WINDOW, :] = jnp.zeros((WINDOW, KV_WIDTH), BF16)
        vpad[0:WINDOW, :] = jnp.zeros((WINDOW, KV_WIDTH), BF16)

    xpad[SUBLANES:SUBLANES + ts, :] = xb_ref[...]
    xc = cb_ref[...]
    for jj in range(CONV_W):
        off = SUBLANES - (CONV_W - 1) + jj
        xc = xc + xpad[off:off + ts, :] * cw_ref[jj:jj + 1, :]
    u_s[...] = xc
    xpad[0:SUBLANES, :] = xpad[ts:ts + SUBLANES, :]

    _lru_gates(u_s, a_s, wa_ref, ba_ref, wx_ref, bx_ref, lam_ref)

    row = lax.broadcasted_iota(jnp.int32, (SUBLANES, LRU_WIDTH), 0)

    def scan_body(g, carry):
        rows = pl.ds(pl.multiple_of(g * SUBLANES, SUBLANES), SUBLANES)
        h = _scan_group(a_s[rows, :], u_s[rows, :], carry, row)
        a_s[rows, :] = h
        return jnp.broadcast_to(h[SUBLANES - 1:SUBLANES, :], (SUBLANES, LRU_WIDTH))

    carry = lax.fori_loop(0, ts // SUBLANES, scan_body, hc[...], unroll=4)
    hc[...] = carry
    lru_ref[0] = carry[0:1, :]
    mix_ref[:, 0:LRU_WIDTH] = (a_s[...] * jax.nn.gelu(yb_ref[...])).astype(BF16)

    cos = cs_ref[...]
    sin = sn_ref[...]
    for kh in range(N_KV_HEADS):
        cols = slice(kh * HEAD_DIM, (kh + 1) * HEAD_DIM)
        kr = _rope(kv_ref[:, cols], cos, sin)
        kpad[WINDOW:WINDOW + ts, cols] = kr.astype(BF16)
        k_ref[0, :, cols] = kr[ts - WINDOW:, :]
    vpad[WINDOW:WINDOW + ts, :] = kv_ref[:, KV_WIDTH:2 * KV_WIDTH].astype(BF16)

    nq = GROUP * WINDOW
    r_i = lax.broadcasted_iota(jnp.int32, (nq, 2 * WINDOW), 0) & (WINDOW - 1)
    c_i = lax.broadcasted_iota(jnp.int32, (nq, 2 * WINDOW), 1)
    band = (c_i >= r_i) & (c_i <= r_i + WINDOW)
    for n in range(ts // WINDOW):
        qrows = slice(n * WINDOW, (n + 1) * WINDOW)
        kpos0 = t * ts + (n - 1) * WINDOW
        mask = band & (c_i + kpos0 >= 0)
        for kh in range(N_KV_HEADS):
            cols = slice(kh * HEAD_DIM, (kh + 1) * HEAD_DIM)
            qs = []
            for g in range(GROUP):
                hd = kh * GROUP + g
                qh = q_ref[qrows, hd * HEAD_DIM:(hd + 1) * HEAD_DIM]
                qs.append(_rope(qh, cos[qrows, :], sin[qrows, :]).astype(BF16))
            q4 = jnp.concatenate(qs, axis=0)
            kk = kpad[n * WINDOW:(n + 2) * WINDOW, cols]
            vv = vpad[n * WINDOW:(n + 2) * WINDOW, cols]
            s = lax.dot_general(q4, kk, (((1,), (1,)), ((), ())), preferred_element_type=F32) * SCALE
            o = _softmax_pv(s, mask, _sink_column(sink_ref, kh, WINDOW), vv)
            for g in range(GROUP):
                hd = kh * GROUP + g
                mix_ref[qrows, LRU_WIDTH + hd * HEAD_DIM:LRU_WIDTH + (hd + 1) * HEAD_DIM] = (
                    o[g * WINDOW:(g + 1) * WINDOW, :].astype(BF16))

    kpad[0:WINDOW, :] = kpad[ts:ts + WINDOW, :]
    vpad[0:WINDOW, :] = vpad[ts:ts + WINDOW, :]


def _mix_prompt(proj, cos, sin, p, *, batch, seq):
    ts = MIX_TS
    nt = seq // ts
    body = functools.partial(_mix_prompt_body, ts=ts)
    w = LRU_WIDTH

    def tile(width, col_block):
        return pl.BlockSpec((ts, width), lambda b, t: (b * nt + t, col_block))

    def whole(shape):
        return pl.BlockSpec(shape, lambda b, t: (0,) * len(shape))

    return pl.pallas_call(
        body,
        grid=(batch, nt),
        in_specs=[
            tile(w, 0), tile(w, 1), tile(ATTN_WIDTH, Q_COL // ATTN_WIDTH),
            tile(2 * KV_WIDTH, K_COL // (2 * KV_WIDTH)),
            pl.BlockSpec((ts, HEAD_DIM), lambda b, t: (t, 0)),
            pl.BlockSpec((ts, HEAD_DIM), lambda b, t: (t, 0)),
            whole((CONV_W, w)), whole((1, w)),
            whole((LRU_HEADS, LRU_BLOCK, LRU_BLOCK)), whole((1, w)),
            whole((LRU_HEADS, LRU_BLOCK, LRU_BLOCK)), whole((1, w)),
            whole((1, w)),
            pl.BlockSpec(memory_space=pltpu.SMEM),
        ],
        out_specs=[
            pl.BlockSpec((ts, D_MODEL), lambda b, t: (b * nt + t, 0)),
            pl.BlockSpec((1, 1, w), lambda b, t: (b, 0, 0)),
            pl.BlockSpec((1, WINDOW, KV_WIDTH), lambda b, t: (b, 0, 0)),
        ],
        out_shape=[
            jax.ShapeDtypeStruct((batch * seq, D_MODEL), BF16),
            jax.ShapeDtypeStruct((batch, 1, w), F32),
            jax.ShapeDtypeStruct((batch, WINDOW, KV_WIDTH), F32),
        ],
        scratch_shapes=[
            pltpu.VMEM((SUBLANES + ts, w), F32),
            pltpu.VMEM((SUBLANES, w), F32),
            pltpu.VMEM((WINDOW + ts, KV_WIDTH), BF16),
            pltpu.VMEM((WINDOW + ts, KV_WIDTH), BF16),
            pltpu.VMEM((ts, w), F32),
            pltpu.VMEM((ts, w), F32),
        ],
        compiler_params=_params(("parallel", "arbitrary")),
        name="mix_prompt",
    )(proj, proj, proj, proj, cos, sin, p["conv_w"], p["conv_b"], p["wa"], p["ba"], p["wx"], p["bx"],
      p["lam"], p["sinks"])


def _mix_sample_body(proj_ref, prev_ref, h0_ref, ck_ref, cv_ref, cs_ref, sn_ref, cw_ref, cb_ref,
                     wa_ref, ba_ref, wx_ref, bx_ref, lam_ref, sink_ref,
                     mix_ref, lru_ref, knew_ref,
                     a_s, u_s, q_s, *, batch, seq):
    w = LRU_WIDTH
    row = lax.broadcasted_iota(jnp.int32, (SUBLANES, w), 0)

    def group(b):
        return pl.ds(pl.multiple_of(b * SUBLANES, SUBLANES), SUBLANES)

    def conv_body(b, carry):
        rows = group(b)
        cur = proj_ref[rows, 0:w]
        prev = prev_ref[rows, :]
        xc = cb_ref[...]
        for jj in range(CONV_W):
            d = CONV_W - 1 - jj
            if d == 0:
                term = cur
            else:
                term = jnp.where(row >= d, pltpu.roll(cur, d, 0), pltpu.roll(prev, d, 0))
            xc = xc + term * cw_ref[jj:jj + 1, :]
        u_s[rows, :] = xc
        return carry

    lax.fori_loop(0, batch, conv_body, 0)

    _lru_gates(u_s, a_s, wa_ref, ba_ref, wx_ref, bx_ref, lam_ref)

    def scan_body(b, carry):
        rows = group(b)
        h0 = jnp.broadcast_to(h0_ref[pl.ds(b, 1), :], (SUBLANES, w))
        h = _scan_group(a_s[rows, :], u_s[rows, :], h0, row)
        a_s[rows, :] = h
        lru_ref[pl.ds(b, 1), :] = h[SUBLANES - 1:SUBLANES, :]
        return carry

    lax.fori_loop(0, batch, scan_body, 0)
    mix_ref[:, 0:w] = a_s[...] * jax.nn.gelu(proj_ref[:, w:2 * w])

    cos = cs_ref[...]
    sin = sn_ref[...]
    for hd in range(N_HEADS):
        cols = slice(hd * HEAD_DIM, (hd + 1) * HEAD_DIM)
        q_s[:, cols] = _rope(proj_ref[:, Q_COL + hd * HEAD_DIM:Q_COL + (hd + 1) * HEAD_DIM], cos, sin)
    for kh in range(N_KV_HEADS):
        cols = slice(kh * HEAD_DIM, (kh + 1) * HEAD_DIM)
        knew_ref[:, cols] = _rope(proj_ref[:, K_COL + kh * HEAD_DIM:K_COL + (kh + 1) * HEAD_DIM], cos, sin)

    nq = GROUP * seq
    nk = 2 * WINDOW
    r_i = lax.broadcasted_iota(jnp.int32, (nq, nk), 0) & (seq - 1)
    c_i = lax.broadcasted_iota(jnp.int32, (nq, nk), 1)
    mask = (c_i >= r_i) & (c_i <= r_i + WINDOW)
    pad = jnp.zeros((nk - WINDOW - seq, HEAD_DIM), F32)

    def attn_body(b, carry):
        rows = group(b)
        for kh in range(N_KV_HEADS):
            cols = slice(kh * HEAD_DIM, (kh + 1) * HEAD_DIM)
            q4 = jnp.concatenate(
                [q_s[rows, (kh * GROUP + g) * HEAD_DIM:(kh * GROUP + g + 1) * HEAD_DIM] for g in range(GROUP)],
                axis=0).astype(BF16)
            kk = jnp.concatenate([ck_ref[b, :, cols], knew_ref[rows, cols], pad], axis=0).astype(BF16)
            vv = jnp.concatenate(
                [cv_ref[b, :, cols], proj_ref[rows, V_COL + kh * HEAD_DIM:V_COL + (kh + 1) * HEAD_DIM], pad],
                axis=0).astype(BF16)
            s = lax.dot_general(q4, kk, (((1,), (1,)), ((), ())), preferred_element_type=F32) * SCALE
            o = _softmax_pv(s, mask, _sink_column(sink_ref, kh, seq), vv)
            for g in range(GROUP):
                hd = kh * GROUP + g
                mix_ref[rows, w + hd * HEAD_DIM:w + (hd + 1) * HEAD_DIM] = o[g * seq:(g + 1) * seq, :]
        return carry

    lax.fori_loop(0, batch, attn_body, 0)


def _mix_sample(proj, prev, h0, cache_k, cache_v, cos, sin, p, *, batch, seq):
    assert seq == SUBLANES, "each sample batch must be exactly one sublane group"
    t = batch * seq
    body = functools.partial(_mix_sample_body, batch=batch, seq=seq)
    vmem = pl.BlockSpec(memory_space=pltpu.VMEM)
    return pl.pallas_call(
        body,
        in_specs=[vmem] * 14 + [pl.BlockSpec(memory_space=pltpu.SMEM)],
        out_specs=[vmem, vmem, vmem],
        out_shape=[
            jax.ShapeDtypeStruct((t, D_MODEL), F32),
            jax.ShapeDtypeStruct((batch, LRU_WIDTH), F32),
            jax.ShapeDtypeStruct((t, KV_WIDTH), F32),
        ],
        scratch_shapes=[
            pltpu.VMEM((t, LRU_WIDTH), F32),
            pltpu.VMEM((t, LRU_WIDTH), F32),
            pltpu.VMEM((t, ATTN_WIDTH), F32),
        ],
        compiler_params=pltpu.CompilerParams(vmem_limit_bytes=V7X_VMEM_LIMIT_BYTES),
        name="mix_sample",
    )(proj, prev, h0, cache_k, cache_v, cos, sin, p["conv_w"], p["conv_b"], p["wa"], p["ba"], p["wx"], p["bx"],
      p["lam"], p["sinks"])


def _layer(w, l):
    return w.reshape(w.shape[1:]) if w.shape[0] == 1 else w[l]


def _rope_tables(pos):
    half = HEAD_DIM // 2
    inv = ROPE_THETA ** (-jnp.arange(half, dtype=F32) / half)
    ang = pos.astype(F32)[:, None] * inv[None, :]
    cos = jnp.cos(ang)
    sin = jnp.sin(ang)
    return jnp.concatenate([cos, cos], axis=-1), jnp.concatenate([-sin, sin], axis=-1)


def kernel(x_prompt, x_sample, c_prompt, c_sample, state_conv, state_lru, cache_k_win, cache_v_win, ada_w, ada_b, norm_ffn1, norm_mix, norm_ffn2, ffn1_w_gate, ffn1_w_up, ffn1_w_down, w_in, conv_w, conv_b, lru_w_a, lru_b_a, lru_w_x, lru_b_x, lru_lambda, attn_sinks, w_out, ffn2_w_gate, ffn2_w_up, ffn2_w_down, norm_final):
    bp, sp, _ = x_prompt.shape
    bs, ss, _ = x_sample.shape
    depth = ada_w.shape[0]
    tp, tsn = bp * sp, bs * ss
    assert bp <= MOD_PROMPT_ROWS and bp + bs <= MOD_ROWS
    tm_p = 1024
    assert sp % tm_p == 0

    xp = x_prompt.reshape(tp, D_MODEL)
    xs = x_sample.reshape(tsn, D_MODEL)
    c_all = jnp.concatenate(
        [c_prompt, jnp.zeros((MOD_PROMPT_ROWS - bp, D_MODEL), F32), c_sample,
         jnp.zeros((MOD_ROWS - MOD_PROMPT_ROWS - bs, D_MODEL), F32)], axis=0)
    cos_p, sin_p = _rope_tables(jnp.arange(sp))
    cos_s, sin_s = _rope_tables(PAST_LEN + jnp.arange(ss))
    cos_s = jnp.tile(cos_s, (bs, 1))
    sin_s = jnp.tile(sin_s, (bs, 1))

    outs_p, outs_s = [], []
    for l in range(depth):
        last = l == depth - 1
        L = functools.partial(_layer, l=l)
        cache_k, cache_v = L(cache_k_win), L(cache_v_win)
        mod = _adaln(c_all, L(ada_w), L(ada_b))
        p = dict(conv_w=L(conv_w), conv_b=L(conv_b).reshape(1, LRU_WIDTH), wa=L(lru_w_a),
                 ba=L(lru_b_a).reshape(1, LRU_WIDTH), wx=L(lru_w_x), bx=L(lru_b_x).reshape(1, LRU_WIDTH),
                 lam=L(lru_lambda).reshape(1, LRU_WIDTH), sinks=L(attn_sinks))
        ffn1 = (L(norm_ffn1), L(ffn1_w_gate), L(ffn1_w_up), L(ffn1_w_down), norm_final)
        ffn2 = (L(norm_ffn2), L(ffn2_w_gate), L(ffn2_w_up), L(ffn2_w_down), norm_final)
        grp_p = dict(row0=0, rows_per_batch=sp, emit=False)
        grp_s = dict(tm=tsn, nr=ss, row0=MOD_PROMPT_ROWS, rows_per_batch=ss, emit=True)
        ffn_p = dict(tm=tm_p, nr=FFN_ROWS, **grp_p)
        ffn_s = grp_s

        xs, wg_b, wu_b, wd_b = _ffn(xs, mod, (0, 1, 2), *ffn1, final_norm=False, name="ffn1_sample", **ffn_s)
        xp, = _ffn(xp, mod, (0, 1, 2), ffn1[0], wg_b, wu_b, wd_b, norm_final, final_norm=False,
                   name="ffn1_prompt", **ffn_p)

        proj_s, w_in_b = _inproj(xs, mod, (3, 4), L(norm_mix), L(w_in), tn=PROJ_TN, name="inproj_sample",
                                 **grp_s)
        proj_p, = _inproj(xp, mod, (3, 4), L(norm_mix), w_in_b, tm=tm_p, tn=IN_COLS // 2, nr=FFN_ROWS,
                          name="inproj_prompt", **grp_p)
        mix_p, lru_p, k_p = _mix_prompt(proj_p, cos_p, sin_p, p, batch=bp, seq=sp)
        prev_s = jnp.pad(L(state_conv), ((0, 0), (SUBLANES - (CONV_W - 1), 0), (0, 0))).reshape(tsn, LRU_WIDTH)
        mix_s, lru_s, knew_s = _mix_sample(
            proj_s, prev_s, L(state_lru), cache_k.reshape(bs, WINDOW, KV_WIDTH),
            cache_v.reshape(bs, WINDOW, KV_WIDTH), cos_s, sin_s, p, batch=bs, seq=ss)
        xs, w_out_b = _outproj(mix_s, xs, mod, 5, L(w_out), tn=PROJ_TN, name="outproj_sample", **grp_s)
        xp, = _outproj(mix_p, xp, mod, 5, w_out_b, tm=OUT_TM, tn=D_MODEL, nr=OUT_TM, name="outproj_prompt",
                       **grp_p)

        xs, wg_b, wu_b, wd_b = _ffn(xs, mod, (6, 7, 8), *ffn2, final_norm=last, name="ffn2_sample", **ffn_s)
        xp, = _ffn(xp, mod, (6, 7, 8), ffn2[0], wg_b, wu_b, wd_b, norm_final, final_norm=last,
                   name="ffn2_prompt", **ffn_p)

        pp = proj_p.reshape(bp, sp, IN_COLS)
        ps = proj_s.reshape(bs, ss, IN_COLS)
        outs_p.append((
            pp[:, sp - (CONV_W - 1):, 0:LRU_WIDTH],
            lru_p.reshape(bp, LRU_WIDTH),
            k_p.reshape(bp, WINDOW, N_KV_HEADS, HEAD_DIM),
            pp[:, sp - WINDOW:, V_COL:].reshape(bp, WINDOW, N_KV_HEADS, HEAD_DIM),
        ))
        k_all = jnp.concatenate([cache_k, knew_s.reshape(bs, ss, N_KV_HEADS, HEAD_DIM)], axis=1)
        v_all = jnp.concatenate([cache_v, ps[:, :, V_COL:].reshape(bs, ss, N_KV_HEADS, HEAD_DIM)], axis=1)
        outs_s.append((
            ps[:, ss - (CONV_W - 1):, 0:LRU_WIDTH],
            lru_s,
            k_all[:, -WINDOW:],
            v_all[:, -WINDOW:],
        ))

    y_prompt = xp.reshape(bp, sp, D_MODEL)
    y_sample = xs.reshape(bs, ss, D_MODEL)
    st_p = [jnp.stack([o[i] for o in outs_p]) for i in range(4)]
    st_s = [jnp.stack([o[i] for o in outs_s]) for i in range(4)]
    return (y_prompt, y_sample, st_p[0], st_p[1], st_p[2], st_p[3], st_s[0], st_s[1], st_s[2], st_s[3])
```

```python
import functools

import jax
import jax.numpy as jnp
from jax import lax
from jax.experimental import pallas as pl
from jax.experimental.pallas import tpu as pltpu

F32 = jnp.float32
BF16 = jnp.bfloat16

D_MODEL = 2048
D_FF = 5504
LRU_WIDTH = 1024
LRU_HEADS = 8
LRU_BLOCK = 128
CONV_W = 4
LRU_C = 8.0
HEAD_DIM = 128
N_HEADS = 8
N_KV_HEADS = 2
GROUP = N_HEADS // N_KV_HEADS
ATTN_WIDTH = N_HEADS * HEAD_DIM
KV_WIDTH = N_KV_HEADS * HEAD_DIM
WINDOW = 128
ROPE_THETA = 10000.0
N_MOD = 9
EPS = 1e-6
NEG = -1e30
PAST_LEN = 16384
IN_COLS = 2 * LRU_WIDTH + ATTN_WIDTH + 2 * KV_WIDTH
Q_COL = 2 * LRU_WIDTH
K_COL = Q_COL + ATTN_WIDTH
V_COL = K_COL + KV_WIDTH
SCALE = HEAD_DIM ** -0.5

SUBLANES = 8
LANES = 128
V7X_VMEM_LIMIT_BYTES = 56 * 1024 * 1024

MOD_ROWS = 48
MOD_PROMPT_ROWS = 8
ADALN_TN = 1024
FFN_TF = 512
FFN_ROWS = 16
FFN_EPILOGUE_ROWS = 128
PROJ_TN = 512
OUT_TM = 512
MIX_TS = 256


def _params(sem, vmem=V7X_VMEM_LIMIT_BYTES):
    return pltpu.CompilerParams(dimension_semantics=sem, vmem_limit_bytes=vmem)


def _sigmoid(x):
    return 1.0 / (1.0 + jnp.exp(-x))


def _normmod(x, g, sc, sh):
    ms = jnp.mean(x * x, axis=-1, keepdims=True)
    xn = x * lax.rsqrt(ms + EPS) * g
    return xn * (1.0 + sc) + sh


def _mod_row(ref, token, row0, batch_shift):
    return ref[pl.ds(row0 + lax.shift_right_logical(token, batch_shift), 1), :]


def _mod_spec(chunk):
    return pl.BlockSpec((MOD_ROWS, D_MODEL), lambda i, j: (0, chunk))


def _adaln_body(c_ref, w_ref, b_ref, o_ref):
    c = c_ref[...]
    s = (c * _sigmoid(c)).astype(BF16)
    o_ref[...] = jnp.dot(s, w_ref[...].astype(BF16), preferred_element_type=F32) + b_ref[...]


def _adaln(c_all, ada_w, ada_b):
    n = ada_w.shape[1]
    return pl.pallas_call(
        _adaln_body,
        grid=(n // ADALN_TN,),
        in_specs=[
            pl.BlockSpec((MOD_ROWS, D_MODEL), lambda j: (0, 0)),
            pl.BlockSpec((D_MODEL, ADALN_TN), lambda j: (0, j)),
            pl.BlockSpec((1, ADALN_TN), lambda j: (0, j)),
        ],
        out_specs=pl.BlockSpec((MOD_ROWS, ADALN_TN), lambda j: (0, j)),
        out_shape=jax.ShapeDtypeStruct((MOD_ROWS, n), F32),
        compiler_params=_params(("arbitrary",)),
        name="adaln",
    )(c_all, ada_w, ada_b.reshape(1, n))


def _ffn_body(x_ref, sh_ref, sc_ref, gt_ref, g_ref, wg_ref, wu_ref, wd_ref, gf_ref, *rest,
              nr, er, row0, batch_shift, final_norm, emit):
    if emit:
        o_ref, wgb_ref, wub_ref, wdb_ref, h_ref = rest
    else:
        o_ref, h_ref = rest
    i = pl.program_id(0)
    j = pl.program_id(1)
    nj = pl.num_programs(1)
    tm = x_ref.shape[0]
    tf = wd_ref.shape[0]
    nchunks = tm // nr

    def chunk(r):
        return pl.ds(pl.multiple_of(r * nr, nr), nr)

    def mod_row(ref, r):
        return _mod_row(ref, i * tm + r * nr, row0, batch_shift)

    @pl.when(j == 0)
    def _():
        g = g_ref[...]

        def body(r, carry):
            rows = chunk(r)
            h = _normmod(x_ref[rows, :], g, mod_row(sc_ref, r), mod_row(sh_ref, r))
            h_ref[rows, :] = h.astype(h_ref.dtype)
            o_ref[rows, :] = jnp.zeros((nr, D_MODEL), F32)
            return carry

        lax.fori_loop(0, nchunks, body, 0, unroll=4)

    if emit:
        valid = D_FF - j * tf
        cmask = lax.broadcasted_iota(jnp.int32, (1, tf), 1) < valid
        rmask = lax.broadcasted_iota(jnp.int32, (tf, 1), 0) < valid
        wg = jnp.where(cmask, wg_ref[...], 0.0).astype(BF16)
        wu = jnp.where(cmask, wu_ref[...], 0.0).astype(BF16)
        wd = jnp.where(rmask, wd_ref[...], 0.0).astype(BF16)
        wgb_ref[...] = wg
        wub_ref[...] = wu
        wdb_ref[...] = wd
    else:
        wg, wu, wd = wg_ref[...], wu_ref[...], wd_ref[...]

    h = h_ref[...].astype(BF16)
    g = jnp.dot(h, wg, preferred_element_type=F32)
    u = jnp.dot(h, wu, preferred_element_type=F32)
    a = (g * _sigmoid(g) * u).astype(BF16)
    o_ref[...] += jnp.dot(a, wd, preferred_element_type=F32)

    @pl.when(j == nj - 1)
    def _():
        gf = gf_ref[...]

        def body(r, carry):
            rows = pl.ds(pl.multiple_of(r * er, er), er)
            gt = _mod_row(gt_ref, i * tm + r * er, row0, batch_shift)
            y = x_ref[rows, :] + (0.5 * gt) * o_ref[rows, :]
            if final_norm:
                ms = jnp.mean(y * y, axis=-1, keepdims=True)
                y = y * lax.rsqrt(ms + EPS) * gf
            o_ref[rows, :] = y
            return carry

        lax.fori_loop(0, tm // er, body, 0)


def _ffn(x, mod, chunks, g, wg, wu, wd, gf, *, tm, nr, row0, rows_per_batch, final_norm, emit, name):
    t = x.shape[0]
    tf = FFN_TF
    nj = pl.cdiv(D_FF, tf)
    er = min(FFN_EPILOGUE_ROWS, rows_per_batch)
    assert rows_per_batch & (rows_per_batch - 1) == 0 and rows_per_batch % nr == 0 and tm % nr == 0
    assert rows_per_batch % er == 0 and tm % er == 0
    body = functools.partial(_ffn_body, nr=nr, er=er, row0=row0, batch_shift=rows_per_batch.bit_length() - 1,
                             final_norm=final_norm, emit=emit)

    mspec = _mod_spec
    row = pl.BlockSpec((1, D_MODEL), lambda i, j: (0, 0))
    if emit:
        wspecs = [pl.BlockSpec((D_MODEL, tf), lambda i, j: (0, j)),
                  pl.BlockSpec((D_MODEL, tf), lambda i, j: (0, j)),
                  pl.BlockSpec((tf, D_MODEL), lambda i, j: (j, 0))]
    else:
        wspecs = [pl.BlockSpec((None, D_MODEL, tf), lambda i, j: (j, 0, 0)),
                  pl.BlockSpec((None, D_MODEL, tf), lambda i, j: (j, 0, 0)),
                  pl.BlockSpec((tf, D_MODEL), lambda i, j: (j, 0))]
    out_specs = [pl.BlockSpec((tm, D_MODEL), lambda i, j: (i, 0))]
    out_shape = [jax.ShapeDtypeStruct((t, D_MODEL), F32)]
    if emit:
        assert t == tm, "bf16 weight tiles are written once, by a single token tile"
        out_specs += [pl.BlockSpec((None, D_MODEL, tf), lambda i, j: (j, 0, 0)),
                      pl.BlockSpec((None, D_MODEL, tf), lambda i, j: (j, 0, 0)),
                      pl.BlockSpec((tf, D_MODEL), lambda i, j: (j, 0))]
        out_shape += [jax.ShapeDtypeStruct((nj, D_MODEL, tf), BF16),
                      jax.ShapeDtypeStruct((nj, D_MODEL, tf), BF16),
                      jax.ShapeDtypeStruct((nj * tf, D_MODEL), BF16)]
    return pl.pallas_call(
        body,
        grid=(t // tm, nj),
        in_specs=[pl.BlockSpec((tm, D_MODEL), lambda i, j: (i, 0)),
                  mspec(chunks[0]), mspec(chunks[1]), mspec(chunks[2]), row] + wspecs + [row],
        out_specs=out_specs,
        out_shape=out_shape,
        scratch_shapes=[pltpu.VMEM((tm, D_MODEL), F32 if emit else BF16)],
        compiler_params=_params(("parallel", "arbitrary")),
        name=name,
    )(x, mod, mod, mod, g.reshape(1, D_MODEL), wg, wu, wd, gf.reshape(1, D_MODEL))


def _inproj_body(x_ref, sh_ref, sc_ref, g_ref, w_ref, *rest, nr, row0, batch_shift, emit):
    if emit:
        o_ref, wb_ref, h_ref = rest
    else:
        o_ref, h_ref = rest
    i = pl.program_id(0)
    j = pl.program_id(1)
    tm = x_ref.shape[0]

    @pl.when(j == 0)
    def _():
        g = g_ref[...]

        def body(r, carry):
            rows = pl.ds(pl.multiple_of(r * nr, nr), nr)
            tok = i * tm + r * nr
            h = _normmod(x_ref[rows, :], g, _mod_row(sc_ref, tok, row0, batch_shift),
                         _mod_row(sh_ref, tok, row0, batch_shift))
            h_ref[rows, :] = h.astype(h_ref.dtype)
            return carry

        lax.fori_loop(0, tm // nr, body, 0, unroll=4)

    w = w_ref[...].astype(BF16)
    if emit:
        wb_ref[...] = w
    o_ref[...] = jnp.dot(h_ref[...].astype(BF16), w, preferred_element_type=F32)


def _inproj(x, mod, chunks, g, w_in, *, tm, tn, nr, row0, rows_per_batch, emit, name):
    t = x.shape[0]
    assert rows_per_batch & (rows_per_batch - 1) == 0 and rows_per_batch % nr == 0 and tm % nr == 0
    body = functools.partial(_inproj_body, nr=nr, row0=row0, batch_shift=rows_per_batch.bit_length() - 1,
                             emit=emit)
    wspec = pl.BlockSpec((D_MODEL, tn), lambda i, j: (0, j))
    out_specs = [pl.BlockSpec((tm, tn), lambda i, j: (i, j))]
    out_shape = [jax.ShapeDtypeStruct((t, IN_COLS), F32)]
    if emit:
        assert t == tm, "the bf16 weight copy is written once, by a single token tile"
        out_specs.append(wspec)
        out_shape.append(jax.ShapeDtypeStruct((D_MODEL, IN_COLS), BF16))
    return pl.pallas_call(
        body,
        grid=(t // tm, IN_COLS // tn),
        in_specs=[
            pl.BlockSpec((tm, D_MODEL), lambda i, j: (i, 0)),
            _mod_spec(chunks[0]), _mod_spec(chunks[1]),
            pl.BlockSpec((1, D_MODEL), lambda i, j: (0, 0)),
            wspec,
        ],
        out_specs=out_specs,
        out_shape=out_shape,
        scratch_shapes=[pltpu.VMEM((tm, D_MODEL), F32 if emit else BF16)],
        compiler_params=_params(("parallel", "arbitrary")),
        name=name,
    )(x, mod, mod, g.reshape(1, D_MODEL), w_in)


def _outproj_body(m_ref, x_ref, gt_ref, w_ref, *rest, nr, row0, batch_shift, emit):
    if emit:
        o_ref, wb_ref = rest
    else:
        o_ref, = rest
    i = pl.program_id(0)
    tm = x_ref.shape[0]
    w = w_ref[...].astype(BF16)
    if emit:
        wb_ref[...] = w
    d = jnp.dot(m_ref[...].astype(BF16), w, preferred_element_type=F32)
    if nr == tm:
        o_ref[...] = x_ref[...] + _mod_row(gt_ref, i * tm, row0, batch_shift) * d
    else:
        o_ref[...] = d

        def body(r, carry):
            rows = pl.ds(pl.multiple_of(r * nr, nr), nr)
            gt = _mod_row(gt_ref, i * tm + r * nr, row0, batch_shift)
            o_ref[rows, :] = x_ref[rows, :] + gt * o_ref[rows, :]
            return carry

        lax.fori_loop(0, tm // nr, body, 0)


def _outproj(mix, x, mod, gt_c, w_out, *, tm, tn, nr, row0, rows_per_batch, emit, name):
    t = x.shape[0]
    nn = D_MODEL // tn
    assert rows_per_batch & (rows_per_batch - 1) == 0 and rows_per_batch % nr == 0 and tm % nr == 0
    body = functools.partial(_outproj_body, nr=nr, row0=row0, batch_shift=rows_per_batch.bit_length() - 1,
                             emit=emit)
    wspec = pl.BlockSpec((D_MODEL, tn), lambda i, j: (0, j))
    out_specs = [pl.BlockSpec((tm, tn), lambda i, j: (i, j))]
    out_shape = [jax.ShapeDtypeStruct((t, D_MODEL), F32)]
    if emit:
        assert t == tm, "the bf16 weight copy is written once, by a single token tile"
        out_specs.append(wspec)
        out_shape.append(jax.ShapeDtypeStruct((D_MODEL, D_MODEL), BF16))
    return pl.pallas_call(
        body,
        grid=(t // tm, nn),
        in_specs=[
            pl.BlockSpec((tm, D_MODEL), lambda i, j: (i, 0)),
            pl.BlockSpec((tm, tn), lambda i, j: (i, j)),
            pl.BlockSpec((MOD_ROWS, tn), lambda i, j: (0, gt_c * nn + j)),
            wspec,
        ],
        out_specs=out_specs,
        out_shape=out_shape,
        compiler_params=_params(("parallel", "arbitrary")),
        name=name,
    )(mix, x, mod, w_out)


def _rope(x, cos, sin_signed):
    return x * cos + pltpu.roll(x, HEAD_DIM // 2, 1) * sin_signed


def _lru_gates(xc_ref, a_ref, wa_ref, ba_ref, wx_ref, bx_ref, lam_ref):
    nlam = -lam_ref[...]
    softplus = jnp.maximum(nlam, 0.0) + jnp.log1p(jnp.exp(-jnp.abs(nlam)))
    rate = -LRU_C * softplus
    for hh in range(LRU_HEADS):
        cols = slice(hh * LRU_BLOCK, (hh + 1) * LRU_BLOCK)
        xc = xc_ref[:, cols]
        xcb = xc.astype(BF16)
        ra = jnp.dot(xcb, wa_ref[hh].astype(BF16), preferred_element_type=F32) + ba_ref[:, cols]
        rx = jnp.dot(xcb, wx_ref[hh].astype(BF16), preferred_element_type=F32) + bx_ref[:, cols]
        r = _sigmoid(ra)
        gi = _sigmoid(rx)
        a = jnp.exp(r * rate[:, cols])
        a_ref[:, cols] = a
        xc_ref[:, cols] = jnp.sqrt(1.0 - a * a) * (gi * xc)


def _scan_group(a, u, carry, row):
    for s in (1, 2, 4):
        a_sh = pltpu.roll(a, s, 0)
        u_sh = pltpu.roll(u, s, 0)
        m = row >= s
        u = jnp.where(m, a * u_sh + u, u)
        a = jnp.where(m, a * a_sh, a)
    return a * carry + u


def _softmax_pv(s, mask, sink, v):
    s = jnp.where(mask, s, NEG)
    m = jnp.maximum(jnp.max(s, axis=-1, keepdims=True), sink)
    p = jnp.exp(s - m)
    den = jnp.sum(p, axis=-1, keepdims=True) + jnp.exp(sink - m)
    return jnp.dot(p.astype(BF16), v, preferred_element_type=F32) / den


def _sink_column(sink_ref, kh, rows_per_head):
    n = GROUP * rows_per_head
    ri = lax.broadcasted_iota(jnp.int32, (n, 1), 0)
    col = jnp.full((n, 1), sink_ref[kh * GROUP + GROUP - 1], F32)
    for g in range(GROUP - 2, -1, -1):
        col = jnp.where(ri < (g + 1) * rows_per_head, sink_ref[kh * GROUP + g], col)
    return col


def _mix_tile(proj, mix, cs_ref, sn_ref, cw_ref, cb_ref, wa_ref, ba_ref, wx_ref, bx_ref, lam_ref, sink_ref,
              xpad, hc, kpad, vpad, a_s, u_s, klast, *, ts, pos0):
    w = LRU_WIDTH
    xpad[SUBLANES:SUBLANES + ts, :] = proj[:, 0:w]
    xc = cb_ref[...]
    for jj in range(CONV_W):
        off = SUBLANES - (CONV_W - 1) + jj
        xc = xc + xpad[off:off + ts, :] * cw_ref[jj:jj + 1, :]
    u_s[...] = xc
    xpad[0:SUBLANES, :] = xpad[ts:ts + SUBLANES, :]

    _lru_gates(u_s, a_s, wa_ref, ba_ref, wx_ref, bx_ref, lam_ref)

    row = lax.broadcasted_iota(jnp.int32, (SUBLANES, w), 0)
    carry = hc[...]
    for g in range(ts // SUBLANES):
        rows = slice(g * SUBLANES, (g + 1) * SUBLANES)
        h = _scan_group(a_s[rows, :], u_s[rows, :], carry, row)
        a_s[rows, :] = h
        carry = jnp.broadcast_to(h[SUBLANES - 1:SUBLANES, :], (SUBLANES, w))
    hc[...] = carry
    mix[:, 0:w] = (a_s[...] * jax.nn.gelu(proj[:, w:2 * w])).astype(BF16)

    cos = cs_ref[...]
    sin = sn_ref[...]
    for kh in range(N_KV_HEADS):
        cols = slice(kh * HEAD_DIM, (kh + 1) * HEAD_DIM)
        kr = _rope(proj[:, K_COL + kh * HEAD_DIM:K_COL + (kh + 1) * HEAD_DIM], cos, sin)
        kpad[WINDOW:WINDOW + ts, cols] = kr.astype(BF16)
        klast[:, cols] = kr[ts - WINDOW:, :]
    vpad[WINDOW:WINDOW + ts, :] = proj[:, V_COL:V_COL + KV_WIDTH].astype(BF16)

    nq = GROUP * WINDOW
    r_i = lax.broadcasted_iota(jnp.int32, (nq, 2 * WINDOW), 0) & (WINDOW - 1)
    c_i = lax.broadcasted_iota(jnp.int32, (nq, 2 * WINDOW), 1)
    band = (c_i >= r_i) & (c_i <= r_i + WINDOW)
    for n in range(ts // WINDOW):
        qrows = slice(n * WINDOW, (n + 1) * WINDOW)
        kpos0 = pos0 + (n - 1) * WINDOW
        mask = band & (c_i + kpos0 >= 0)
        for kh in range(N_KV_HEADS):
            cols = slice(kh * HEAD_DIM, (kh + 1) * HEAD_DIM)
            qs = []
            for g in range(GROUP):
                hd = kh * GROUP + g
                qh = proj[qrows, Q_COL + hd * HEAD_DIM:Q_COL + (hd + 1) * HEAD_DIM]
                qs.append(_rope(qh, cos[qrows, :], sin[qrows, :]).astype(BF16))
            q4 = jnp.concatenate(qs, axis=0)
            kk = kpad[n * WINDOW:(n + 2) * WINDOW, cols]
            vv = vpad[n * WINDOW:(n + 2) * WINDOW, cols]
            s = lax.dot_general(q4, kk, (((1,), (1,)), ((), ())), preferred_element_type=F32) * SCALE
            o = _softmax_pv(s, mask, _sink_column(sink_ref, kh, WINDOW), vv)
            for g in range(GROUP):
                hd = kh * GROUP + g
                mix[qrows, w + hd * HEAD_DIM:w + (hd + 1) * HEAD_DIM] = (
                    o[g * WINDOW:(g + 1) * WINDOW, :].astype(BF16))

    kpad[0:WINDOW, :] = kpad[ts:ts + WINDOW, :]
    vpad[0:WINDOW, :] = vpad[ts:ts + WINDOW, :]


def _mixer_body(xa_ref, xc_ref, sh_ref, sc_ref, gt_ref, gm_ref, win_ref, wout_ref, cs_ref, sn_ref,
                cw_ref, cb_ref, wa_ref, ba_ref, wx_ref, bx_ref, lam_ref, sink_ref,
                o_ref, lru_ref, k_ref, v_ref, conv_ref,
                proj0, proj1, mix0, mix1, h_s, xpad, hc, kpad, vpad, a_s, u_s, klast,
                *, ts, nt, ntiles, batch_shift):
    s = pl.program_id(0)
    tile_a = jnp.minimum(s, ntiles - 1)
    tile_b = jnp.clip(s - 1, 0, ntiles - 1)
    tile_c = jnp.clip(s - 2, 0, ntiles - 1)
    t_in = tile_b & (nt - 1)

    @pl.when(s == 0)
    def _():
        proj1[...] = jnp.zeros(proj1.shape, F32)
        mix0[...] = jnp.zeros(mix0.shape, BF16)

    @pl.when(t_in == 0)
    def _():
        xpad[0:SUBLANES, :] = jnp.zeros((SUBLANES, LRU_WIDTH), F32)
        hc[...] = jnp.zeros((SUBLANES, LRU_WIDTH), F32)
        kpad[0:WINDOW, :] = jnp.zeros((WINDOW, KV_WIDTH), BF16)
        vpad[0:WINDOW, :] = jnp.zeros((WINDOW, KV_WIDTH), BF16)

    def step(par):
        proj_w, proj_r = (proj0, proj1) if par == 0 else (proj1, proj0)
        mix_w, mix_r = (mix1, mix0) if par == 0 else (mix0, mix1)

        gt = _mod_row(gt_ref, tile_c * ts, 0, batch_shift)
        o_ref[...] = xc_ref[...] + gt * jnp.dot(mix_r[...], wout_ref[...], preferred_element_type=F32)

        g = gm_ref[...]
        sc = _mod_row(sc_ref, tile_a * ts, 0, batch_shift)
        sh = _mod_row(sh_ref, tile_a * ts, 0, batch_shift)
        for r in range(ts // FFN_ROWS):
            rows = slice(r * FFN_ROWS, (r + 1) * FFN_ROWS)
            h_s[rows, :] = _normmod(xa_ref[rows, :], g, sc, sh).astype(BF16)
        proj_w[...] = jnp.dot(h_s[...], win_ref[...], preferred_element_type=F32)

        _mix_tile(proj_r, mix_w, cs_ref, sn_ref, cw_ref, cb_ref, wa_ref, ba_ref, wx_ref, bx_ref, lam_ref,
                  sink_ref, xpad, hc, kpad, vpad, a_s, u_s, klast, ts=ts, pos0=t_in * ts)

        @pl.when((s >= 1) & (s <= ntiles))
        def _():
            lru_ref[0] = hc[0:1, :]
            k_ref[0] = klast[...]
            v_ref[0] = proj_r[ts - WINDOW:, V_COL:V_COL + KV_WIDTH]
            conv_ref[0] = xpad[0:SUBLANES, :]

    @pl.when((s & 1) == 0)
    def _():
        step(0)

    @pl.when((s & 1) == 1)
    def _():
        step(1)


def _mixer_prompt(x, mod, g_mix, w_in_b, w_out_b, cos, sin, p, *, batch, seq):
    ts = MIX_TS
    nt = seq // ts
    ntiles = batch * nt
    assert nt & (nt - 1) == 0 and seq & (seq - 1) == 0
    w = LRU_WIDTH
    body = functools.partial(_mixer_body, ts=ts, nt=nt, ntiles=ntiles, batch_shift=seq.bit_length() - 1)

    def tile_a(s):
        return jnp.minimum(s, ntiles - 1)

    def tile_b(s):
        return jnp.clip(s - 1, 0, ntiles - 1)

    def tile_c(s):
        return jnp.clip(s - 2, 0, ntiles - 1)

    def whole(shape, **kw):
        return pl.BlockSpec(shape, lambda s: (0,) * len(shape), **kw)

    def mspec(c):
        return pl.BlockSpec((MOD_ROWS, D_MODEL), lambda s: (0, c))

    def state(shape):
        return pl.BlockSpec((1,) + shape, lambda s: (tile_b(s) // nt, 0, 0))

    once = dict(pipeline_mode=pl.Buffered(1))
    return pl.pallas_call(
        body,
        grid=(ntiles + 2,),
        in_specs=[
            pl.BlockSpec((ts, D_MODEL), lambda s: (tile_a(s), 0)),
            pl.BlockSpec((ts, D_MODEL), lambda s: (tile_c(s), 0)),
            mspec(3), mspec(4), mspec(5),
            whole((1, D_MODEL)),
            whole((D_MODEL, IN_COLS), **once),
            whole((D_MODEL, D_MODEL), **once),
            pl.BlockSpec((ts, HEAD_DIM), lambda s: (tile_b(s) % nt, 0)),
            pl.BlockSpec((ts, HEAD_DIM), lambda s: (tile_b(s) % nt, 0)),
            whole((CONV_W, w)), whole((1, w)),
            whole((LRU_HEADS, LRU_BLOCK, LRU_BLOCK)), whole((1, w)),
            whole((LRU_HEADS, LRU_BLOCK, LRU_BLOCK)), whole((1, w)),
            whole((1, w)),
            pl.BlockSpec(memory_space=pltpu.SMEM),
        ],
        out_specs=[
            pl.BlockSpec((ts, D_MODEL), lambda s: (tile_c(s), 0)),
            state((1, w)), state((WINDOW, KV_WIDTH)), state((WINDOW, KV_WIDTH)), state((SUBLANES, w)),
        ],
        out_shape=[
            jax.ShapeDtypeStruct((batch * seq, D_MODEL), F32),
            jax.ShapeDtypeStruct((batch, 1, w), F32),
            jax.ShapeDtypeStruct((batch, WINDOW, KV_WIDTH), F32),
            jax.ShapeDtypeStruct((batch, WINDOW, KV_WIDTH), F32),
            jax.ShapeDtypeStruct((batch, SUBLANES, w), F32),
        ],
        scratch_shapes=[
            pltpu.VMEM((ts, IN_COLS), F32), pltpu.VMEM((ts, IN_COLS), F32),
            pltpu.VMEM((ts, D_MODEL), BF16), pltpu.VMEM((ts, D_MODEL), BF16),
            pltpu.VMEM((ts, D_MODEL), BF16),
            pltpu.VMEM((SUBLANES + ts, w), F32),
            pltpu.VMEM((SUBLANES, w), F32),
            pltpu.VMEM((WINDOW + ts, KV_WIDTH), BF16),
            pltpu.VMEM((WINDOW + ts, KV_WIDTH), BF16),
            pltpu.VMEM((ts, w), F32),
            pltpu.VMEM((ts, w), F32),
            pltpu.VMEM((WINDOW, KV_WIDTH), F32),
        ],
        compiler_params=_params(("arbitrary",)),
        name="mixer_prompt",
    )(x, x, mod, mod, mod, g_mix.reshape(1, D_MODEL), w_in_b, w_out_b, cos, sin,
      p["conv_w"], p["conv_b"], p["wa"], p["ba"], p["wx"], p["bx"], p["lam"], p["sinks"])


def _mix_sample_body(proj_ref, prev_ref, h0_ref, ck_ref, cv_ref, cs_ref, sn_ref, cw_ref, cb_ref,
                     wa_ref, ba_ref, wx_ref, bx_ref, lam_ref, sink_ref,
                     mix_ref, lru_ref, knew_ref,
                     a_s, u_s, q_s, *, batch, seq):
    w = LRU_WIDTH
    row = lax.broadcasted_iota(jnp.int32, (SUBLANES, w), 0)

    def group(b):
        return pl.ds(pl.multiple_of(b * SUBLANES, SUBLANES), SUBLANES)

    def conv_body(b, carry):
        rows = group(b)
        cur = proj_ref[rows, 0:w]
        prev = prev_ref[rows, :]
        xc = cb_ref[...]
        for jj in range(CONV_W):
            d = CONV_W - 1 - jj
            if d == 0:
                term = cur
            else:
                term = jnp.where(row >= d, pltpu.roll(cur, d, 0), pltpu.roll(prev, d, 0))
            xc = xc + term * cw_ref[jj:jj + 1, :]
        u_s[rows, :] = xc
        return carry

    lax.fori_loop(0, batch, conv_body, 0)

    _lru_gates(u_s, a_s, wa_ref, ba_ref, wx_ref, bx_ref, lam_ref)

    def scan_body(b, carry):
        rows = group(b)
        h0 = jnp.broadcast_to(h0_ref[pl.ds(b, 1), :], (SUBLANES, w))
        h = _scan_group(a_s[rows, :], u_s[rows, :], h0, row)
        a_s[rows, :] = h
        lru_ref[pl.ds(b, 1), :] = h[SUBLANES - 1:SUBLANES, :]
        return carry

    lax.fori_loop(0, batch, scan_body, 0)
    mix_ref[:, 0:w] = a_s[...] * jax.nn.gelu(proj_ref[:, w:2 * w])

    cos = cs_ref[...]
    sin = sn_ref[...]
    for hd in range(N_HEADS):
        cols = slice(hd * HEAD_DIM, (hd + 1) * HEAD_DIM)
        q_s[:, cols] = _rope(proj_ref[:, Q_COL + hd * HEAD_DIM:Q_COL + (hd + 1) * HEAD_DIM], cos, sin)
    for kh in range(N_KV_HEADS):
        cols = slice(kh * HEAD_DIM, (kh + 1) * HEAD_DIM)
        knew_ref[:, cols] = _rope(proj_ref[:, K_COL + kh * HEAD_DIM:K_COL + (kh + 1) * HEAD_DIM], cos, sin)

    nq = GROUP * seq
    nk = 2 * WINDOW
    r_i = lax.broadcasted_iota(jnp.int32, (nq, nk), 0) & (seq - 1)
    c_i = lax.broadcasted_iota(jnp.int32, (nq, nk), 1)
    mask = (c_i >= r_i) & (c_i <= r_i + WINDOW)
    pad = jnp.zeros((nk - WINDOW - seq, HEAD_DIM), F32)

    def attn_body(b, carry):
        rows = group(b)
        for kh in range(N_KV_HEADS):
            cols = slice(kh * HEAD_DIM, (kh + 1) * HEAD_DIM)
            q4 = jnp.concatenate(
                [q_s[rows, (kh * GROUP + g) * HEAD_DIM:(kh * GROUP + g + 1) * HEAD_DIM] for g in range(GROUP)],
                axis=0).astype(BF16)
            kk = jnp.concatenate([ck_ref[b, :, cols], knew_ref[rows, cols], pad], axis=0).astype(BF16)
            vv = jnp.concatenate(
                [cv_ref[b, :, cols], proj_ref[rows, V_COL + kh * HEAD_DIM:V_COL + (kh + 1) * HEAD_DIM], pad],
                axis=0).astype(BF16)
            s = lax.dot_general(q4, kk, (((1,), (1,)), ((), ())), preferred_element_type=F32) * SCALE
            o = _softmax_pv(s, mask, _sink_column(sink_ref, kh, seq), vv)
            for g in range(GROUP):
                hd = kh * GROUP + g
                mix_ref[rows, w + hd * HEAD_DIM:w + (hd + 1) * HEAD_DIM] = o[g * seq:(g + 1) * seq, :]
        return carry

    lax.fori_loop(0, batch, attn_body, 0)


def _mix_sample(proj, prev, h0, cache_k, cache_v, cos, sin, p, *, batch, seq):
    assert seq == SUBLANES, "each sample batch must be exactly one sublane group"
    t = batch * seq
    body = functools.partial(_mix_sample_body, batch=batch, seq=seq)
    vmem = pl.BlockSpec(memory_space=pltpu.VMEM)
    return pl.pallas_call(
        body,
        in_specs=[vmem] * 14 + [pl.BlockSpec(memory_space=pltpu.SMEM)],
        out_specs=[vmem, vmem, vmem],
        out_shape=[
            jax.ShapeDtypeStruct((t, D_MODEL), F32),
            jax.ShapeDtypeStruct((batch, LRU_WIDTH), F32),
            jax.ShapeDtypeStruct((t, KV_WIDTH), F32),
        ],
        scratch_shapes=[
            pltpu.VMEM((t, LRU_WIDTH), F32),
            pltpu.VMEM((t, LRU_WIDTH), F32),
            pltpu.VMEM((t, ATTN_WIDTH), F32),
        ],
        compiler_params=pltpu.CompilerParams(vmem_limit_bytes=V7X_VMEM_LIMIT_BYTES),
        name="mix_sample",
    )(proj, prev, h0, cache_k, cache_v, cos, sin, p["conv_w"], p["conv_b"], p["wa"], p["ba"], p["wx"], p["bx"],
      p["lam"], p["sinks"])


def _layer(w, l):
    return w.reshape(w.shape[1:]) if w.shape[0] == 1 else w[l]


def _rope_tables(pos):
    half = HEAD_DIM // 2
    inv = ROPE_THETA ** (-jnp.arange(half, dtype=F32) / half)
    ang = pos.astype(F32)[:, None] * inv[None, :]
    cos = jnp.cos(ang)
    sin = jnp.sin(ang)
    return jnp.concatenate([cos, cos], axis=-1), jnp.concatenate([-sin, sin], axis=-1)


def kernel(x_prompt, x_sample, c_prompt, c_sample, state_conv, state_lru, cache_k_win, cache_v_win, ada_w, ada_b, norm_ffn1, norm_mix, norm_ffn2, ffn1_w_gate, ffn1_w_up, ffn1_w_down, w_in, conv_w, conv_b, lru_w_a, lru_b_a, lru_w_x, lru_b_x, lru_lambda, attn_sinks, w_out, ffn2_w_gate, ffn2_w_up, ffn2_w_down, norm_final):
    bp, sp, _ = x_prompt.shape
    bs, ss, _ = x_sample.shape
    depth = ada_w.shape[0]
    tp, tsn = bp * sp, bs * ss
    assert bp <= MOD_PROMPT_ROWS and bp + bs <= MOD_ROWS
    tm_p = 1024
    assert sp % tm_p == 0

    xp = x_prompt.reshape(tp, D_MODEL)
    xs = x_sample.reshape(tsn, D_MODEL)
    c_all = jnp.concatenate(
        [c_prompt, jnp.zeros((MOD_PROMPT_ROWS - bp, D_MODEL), F32), c_sample,
         jnp.zeros((MOD_ROWS - MOD_PROMPT_ROWS - bs, D_MODEL), F32)], axis=0)
    cos_p, sin_p = _rope_tables(jnp.arange(sp))
    cos_s, sin_s = _rope_tables(PAST_LEN + jnp.arange(ss))
    cos_s = jnp.tile(cos_s, (bs, 1))
    sin_s = jnp.tile(sin_s, (bs, 1))

    outs_p, outs_s = [], []
    for l in range(depth):
        last = l == depth - 1
        L = functools.partial(_layer, l=l)
        cache_k, cache_v = L(cache_k_win), L(cache_v_win)
        mod = _adaln(c_all, L(ada_w), L(ada_b))
        p = dict(conv_w=L(conv_w), conv_b=L(conv_b).reshape(1, LRU_WIDTH), wa=L(lru_w_a),
                 ba=L(lru_b_a).reshape(1, LRU_WIDTH), wx=L(lru_w_x), bx=L(lru_b_x).reshape(1, LRU_WIDTH),
                 lam=L(lru_lambda).reshape(1, LRU_WIDTH), sinks=L(attn_sinks))
        ffn1 = (L(norm_ffn1), L(ffn1_w_gate), L(ffn1_w_up), L(ffn1_w_down), norm_final)
        ffn2 = (L(norm_ffn2), L(ffn2_w_gate), L(ffn2_w_up), L(ffn2_w_down), norm_final)
        grp_p = dict(row0=0, rows_per_batch=sp, emit=False)
        grp_s = dict(tm=tsn, nr=ss, row0=MOD_PROMPT_ROWS, rows_per_batch=ss, emit=True)
        ffn_p = dict(tm=tm_p, nr=FFN_ROWS, **grp_p)
        ffn_s = grp_s

        xs, wg_b, wu_b, wd_b = _ffn(xs, mod, (0, 1, 2), *ffn1, final_norm=False, name="ffn1_sample", **ffn_s)
        xp, = _ffn(xp, mod, (0, 1, 2), ffn1[0], wg_b, wu_b, wd_b, norm_final, final_norm=False,
                   name="ffn1_prompt", **ffn_p)

        proj_s, w_in_b = _inproj(xs, mod, (3, 4), L(norm_mix), L(w_in), tn=PROJ_TN, name="inproj_sample",
                                 **grp_s)
        prev_s = jnp.pad(L(state_conv), ((0, 0), (SUBLANES - (CONV_W - 1), 0), (0, 0))).reshape(tsn, LRU_WIDTH)
        mix_s, lru_s, knew_s = _mix_sample(
            proj_s, prev_s, L(state_lru), cache_k.reshape(bs, WINDOW, KV_WIDTH),
            cache_v.reshape(bs, WINDOW, KV_WIDTH), cos_s, sin_s, p, batch=bs, seq=ss)
        xs, w_out_b = _outproj(mix_s, xs, mod, 5, L(w_out), tn=PROJ_TN, name="outproj_sample", **grp_s)
        xp, lru_p, k_p, v_p, conv_p = _mixer_prompt(xp, mod, L(norm_mix), w_in_b, w_out_b, cos_p, sin_p, p,
                                                    batch=bp, seq=sp)

        xs, wg_b, wu_b, wd_b = _ffn(xs, mod, (6, 7, 8), *ffn2, final_norm=last, name="ffn2_sample", **ffn_s)
        xp, = _ffn(xp, mod, (6, 7, 8), ffn2[0], wg_b, wu_b, wd_b, norm_final, final_norm=last,
                   name="ffn2_prompt", **ffn_p)

        ps = proj_s.reshape(bs, ss, IN_COLS)
        outs_p.append((
            conv_p[:, SUBLANES - (CONV_W - 1):, :],
            lru_p.reshape(bp, LRU_WIDTH),
            k_p.reshape(bp, WINDOW, N_KV_HEADS, HEAD_DIM),
            v_p.reshape(bp, WINDOW, N_KV_HEADS, HEAD_DIM),
        ))
        k_all = jnp.concatenate([cache_k, knew_s.reshape(bs, ss, N_KV_HEADS, HEAD_DIM)], axis=1)
        v_all = jnp.concatenate([cache_v, ps[:, :, V_COL:].reshape(bs, ss, N_KV_HEADS, HEAD_DIM)], axis=1)
        outs_s.append((
            ps[:, ss - (CONV_W - 1):, 0:LRU_WIDTH],
            lru_s,
            k_all[:, -WINDOW:],
            v_all[:, -WINDOW:],
        ))

    y_prompt = xp.reshape(bp, sp, D_MODEL)
    y_sample = xs.reshape(bs, ss, D_MODEL)
    st_p = [jnp.stack([o[i] for o in outs_p]) for i in range(4)]
    st_s = [jnp.stack([o[i] for o in outs_s]) for i in range(4)]
    return (y_prompt, y_sample, st_p[0], st_p[1], st_p[2], st_p[3], st_s[0], st_s[1], st_s[2], st_s[3])
```

```python
import functools

import jax
import jax.numpy as jnp
from jax import lax
from jax.experimental import pallas as pl
from jax.experimental.pallas import tpu as pltpu

F32 = jnp.float32
BF16 = jnp.bfloat16

D_MODEL = 2048
D_FF = 5504
LRU_WIDTH = 1024
LRU_HEADS = 8
LRU_BLOCK = 128
CONV_W = 4
LRU_C = 8.0
HEAD_DIM = 128
N_HEADS = 8
N_KV_HEADS = 2
GROUP = N_HEADS // N_KV_HEADS
ATTN_WIDTH = N_HEADS * HEAD_DIM
KV_WIDTH = N_KV_HEADS * HEAD_DIM
WINDOW = 128
ROPE_THETA = 10000.0
N_MOD = 9
EPS = 1e-6
NEG = -1e30
PAST_LEN = 16384
IN_COLS = 2 * LRU_WIDTH + ATTN_WIDTH + 2 * KV_WIDTH
Q_COL = 2 * LRU_WIDTH
K_COL = Q_COL + ATTN_WIDTH
V_COL = K_COL + KV_WIDTH
SCALE = HEAD_DIM ** -0.5

SUBLANES = 8
LANES = 128
V7X_VMEM_LIMIT_BYTES = 56 * 1024 * 1024

MOD_ROWS = 48
MOD_PROMPT_ROWS = 8
ADALN_TN = 1024
FFN_TF = 512
FFN_ROWS = 16
FFN_EPILOGUE_ROWS = 128
PROJ_TN = 512
OUT_TM = 512
MIX_TS = 256
MIXER_TN = 256


def _params(sem, vmem=V7X_VMEM_LIMIT_BYTES):
    return pltpu.CompilerParams(dimension_semantics=sem, vmem_limit_bytes=vmem)


def _sigmoid(x):
    return 1.0 / (1.0 + jnp.exp(-x))


def _normmod(x, g, sc, sh):
    ms = jnp.mean(x * x, axis=-1, keepdims=True)
    xn = x * lax.rsqrt(ms + EPS) * g
    return xn * (1.0 + sc) + sh


def _mod_row(ref, token, row0, batch_shift):
    return ref[pl.ds(row0 + lax.shift_right_logical(token, batch_shift), 1), :]


def _mod_spec(chunk):
    return pl.BlockSpec((MOD_ROWS, D_MODEL), lambda i, j: (0, chunk))


def _adaln_body(c_ref, w_ref, b_ref, o_ref):
    c = c_ref[...]
    s = (c * _sigmoid(c)).astype(BF16)
    o_ref[...] = jnp.dot(s, w_ref[...].astype(BF16), preferred_element_type=F32) + b_ref[...]


def _adaln(c_all, ada_w, ada_b):
    n = ada_w.shape[1]
    return pl.pallas_call(
        _adaln_body,
        grid=(n // ADALN_TN,),
        in_specs=[
            pl.BlockSpec((MOD_ROWS, D_MODEL), lambda j: (0, 0)),
            pl.BlockSpec((D_MODEL, ADALN_TN), lambda j: (0, j)),
            pl.BlockSpec((1, ADALN_TN), lambda j: (0, j)),
        ],
        out_specs=pl.BlockSpec((MOD_ROWS, ADALN_TN), lambda j: (0, j)),
        out_shape=jax.ShapeDtypeStruct((MOD_ROWS, n), F32),
        compiler_params=_params(("arbitrary",)),
        name="adaln",
    )(c_all, ada_w, ada_b.reshape(1, n))


def _ffn_body(x_ref, sh_ref, sc_ref, gt_ref, g_ref, wg_ref, wu_ref, wd_ref, gf_ref, *rest,
              nr, er, row0, batch_shift, final_norm, emit):
    if emit:
        o_ref, wgb_ref, wub_ref, wdb_ref, h_ref = rest
    else:
        o_ref, h_ref = rest
    i = pl.program_id(0)
    j = pl.program_id(1)
    nj = pl.num_programs(1)
    tm = x_ref.shape[0]
    tf = wd_ref.shape[0]
    nchunks = tm // nr

    def chunk(r):
        return pl.ds(pl.multiple_of(r * nr, nr), nr)

    def mod_row(ref, r):
        return _mod_row(ref, i * tm + r * nr, row0, batch_shift)

    @pl.when(j == 0)
    def _():
        g = g_ref[...]

        def body(r, carry):
            rows = chunk(r)
            h = _normmod(x_ref[rows, :], g, mod_row(sc_ref, r), mod_row(sh_ref, r))
            h_ref[rows, :] = h.astype(h_ref.dtype)
            o_ref[rows, :] = jnp.zeros((nr, D_MODEL), F32)
            return carry

        lax.fori_loop(0, nchunks, body, 0, unroll=4)

    if emit:
        valid = D_FF - j * tf
        cmask = lax.broadcasted_iota(jnp.int32, (1, tf), 1) < valid
        rmask = lax.broadcasted_iota(jnp.int32, (tf, 1), 0) < valid
        wg = jnp.where(cmask, wg_ref[...], 0.0).astype(BF16)
        wu = jnp.where(cmask, wu_ref[...], 0.0).astype(BF16)
        wd = jnp.where(rmask, wd_ref[...], 0.0).astype(BF16)
        wgb_ref[...] = wg
        wub_ref[...] = wu
        wdb_ref[...] = wd
    else:
        wg, wu, wd = wg_ref[...], wu_ref[...], wd_ref[...]

    h = h_ref[...].astype(BF16)
    g = jnp.dot(h, wg, preferred_element_type=F32)
    u = jnp.dot(h, wu, preferred_element_type=F32)
    a = (g * _sigmoid(g) * u).astype(BF16)
    o_ref[...] += jnp.dot(a, wd, preferred_element_type=F32)

    @pl.when(j == nj - 1)
    def _():
        gf = gf_ref[...]

        def body(r, carry):
            rows = pl.ds(pl.multiple_of(r * er, er), er)
            gt = _mod_row(gt_ref, i * tm + r * er, row0, batch_shift)
            y = x_ref[rows, :] + (0.5 * gt) * o_ref[rows, :]
            if final_norm:
                ms = jnp.mean(y * y, axis=-1, keepdims=True)
                y = y * lax.rsqrt(ms + EPS) * gf
            o_ref[rows, :] = y
            return carry

        lax.fori_loop(0, tm // er, body, 0)


def _ffn(x, mod, chunks, g, wg, wu, wd, gf, *, tm, nr, row0, rows_per_batch, final_norm, emit, name):
    t = x.shape[0]
    tf = FFN_TF
    nj = pl.cdiv(D_FF, tf)
    er = min(FFN_EPILOGUE_ROWS, rows_per_batch)
    assert rows_per_batch & (rows_per_batch - 1) == 0 and rows_per_batch % nr == 0 and tm % nr == 0
    assert rows_per_batch % er == 0 and tm % er == 0
    body = functools.partial(_ffn_body, nr=nr, er=er, row0=row0, batch_shift=rows_per_batch.bit_length() - 1,
                             final_norm=final_norm, emit=emit)

    mspec = _mod_spec
    row = pl.BlockSpec((1, D_MODEL), lambda i, j: (0, 0))
    if emit:
        wspecs = [pl.BlockSpec((D_MODEL, tf), lambda i, j: (0, j)),
                  pl.BlockSpec((D_MODEL, tf), lambda i, j: (0, j)),
                  pl.BlockSpec((tf, D_MODEL), lambda i, j: (j, 0))]
    else:
        wspecs = [pl.BlockSpec((None, D_MODEL, tf), lambda i, j: (j, 0, 0)),
                  pl.BlockSpec((None, D_MODEL, tf), lambda i, j: (j, 0, 0)),
                  pl.BlockSpec((tf, D_MODEL), lambda i, j: (j, 0))]
    out_specs = [pl.BlockSpec((tm, D_MODEL), lambda i, j: (i, 0))]
    out_shape = [jax.ShapeDtypeStruct((t, D_MODEL), F32)]
    if emit:
        assert t == tm, "bf16 weight tiles are written once, by a single token tile"
        out_specs += [pl.BlockSpec((None, D_MODEL, tf), lambda i, j: (j, 0, 0)),
                      pl.BlockSpec((None, D_MODEL, tf), lambda i, j: (j, 0, 0)),
                      pl.BlockSpec((tf, D_MODEL), lambda i, j: (j, 0))]
        out_shape += [jax.ShapeDtypeStruct((nj, D_MODEL, tf), BF16),
                      jax.ShapeDtypeStruct((nj, D_MODEL, tf), BF16),
                      jax.ShapeDtypeStruct((nj * tf, D_MODEL), BF16)]
    return pl.pallas_call(
        body,
        grid=(t // tm, nj),
        in_specs=[pl.BlockSpec((tm, D_MODEL), lambda i, j: (i, 0)),
                  mspec(chunks[0]), mspec(chunks[1]), mspec(chunks[2]), row] + wspecs + [row],
        out_specs=out_specs,
        out_shape=out_shape,
        scratch_shapes=[pltpu.VMEM((tm, D_MODEL), F32 if emit else BF16)],
        compiler_params=_params(("parallel", "arbitrary")),
        name=name,
    )(x, mod, mod, mod, g.reshape(1, D_MODEL), wg, wu, wd, gf.reshape(1, D_MODEL))


def _inproj_body(x_ref, sh_ref, sc_ref, g_ref, w_ref, *rest, nr, row0, batch_shift, emit):
    if emit:
        o_ref, wb_ref, h_ref = rest
    else:
        o_ref, h_ref = rest
    i = pl.program_id(0)
    j = pl.program_id(1)
    tm = x_ref.shape[0]

    @pl.when(j == 0)
    def _():
        g = g_ref[...]

        def body(r, carry):
            rows = pl.ds(pl.multiple_of(r * nr, nr), nr)
            tok = i * tm + r * nr
            h = _normmod(x_ref[rows, :], g, _mod_row(sc_ref, tok, row0, batch_shift),
                         _mod_row(sh_ref, tok, row0, batch_shift))
            h_ref[rows, :] = h.astype(h_ref.dtype)
            return carry

        lax.fori_loop(0, tm // nr, body, 0, unroll=4)

    w = w_ref[...].astype(BF16)
    if emit:
        wb_ref[...] = w
    o_ref[...] = jnp.dot(h_ref[...].astype(BF16), w, preferred_element_type=F32)


def _inproj(x, mod, chunks, g, w_in, *, tm, tn, nr, row0, rows_per_batch, emit, name):
    t = x.shape[0]
    assert rows_per_batch & (rows_per_batch - 1) == 0 and rows_per_batch % nr == 0 and tm % nr == 0
    body = functools.partial(_inproj_body, nr=nr, row0=row0, batch_shift=rows_per_batch.bit_length() - 1,
                             emit=emit)
    wspec = pl.BlockSpec((D_MODEL, tn), lambda i, j: (0, j))
    out_specs = [pl.BlockSpec((tm, tn), lambda i, j: (i, j))]
    out_shape = [jax.ShapeDtypeStruct((t, IN_COLS), F32)]
    if emit:
        assert t == tm, "the bf16 weight copy is written once, by a single token tile"
        out_specs.append(wspec)
        out_shape.append(jax.ShapeDtypeStruct((D_MODEL, IN_COLS), BF16))
    return pl.pallas_call(
        body,
        grid=(t // tm, IN_COLS // tn),
        in_specs=[
            pl.BlockSpec((tm, D_MODEL), lambda i, j: (i, 0)),
            _mod_spec(chunks[0]), _mod_spec(chunks[1]),
            pl.BlockSpec((1, D_MODEL), lambda i, j: (0, 0)),
            wspec,
        ],
        out_specs=out_specs,
        out_shape=out_shape,
        scratch_shapes=[pltpu.VMEM((tm, D_MODEL), F32 if emit else BF16)],
        compiler_params=_params(("parallel", "arbitrary")),
        name=name,
    )(x, mod, mod, g.reshape(1, D_MODEL), w_in)


def _outproj_body(m_ref, x_ref, gt_ref, w_ref, *rest, nr, row0, batch_shift, emit):
    if emit:
        o_ref, wb_ref = rest
    else:
        o_ref, = rest
    i = pl.program_id(0)
    tm = x_ref.shape[0]
    w = w_ref[...].astype(BF16)
    if emit:
        wb_ref[...] = w
    d = jnp.dot(m_ref[...].astype(BF16), w, preferred_element_type=F32)
    if nr == tm:
        o_ref[...] = x_ref[...] + _mod_row(gt_ref, i * tm, row0, batch_shift) * d
    else:
        o_ref[...] = d

        def body(r, carry):
            rows = pl.ds(pl.multiple_of(r * nr, nr), nr)
            gt = _mod_row(gt_ref, i * tm + r * nr, row0, batch_shift)
            o_ref[rows, :] = x_ref[rows, :] + gt * o_ref[rows, :]
            return carry

        lax.fori_loop(0, tm // nr, body, 0)


def _outproj(mix, x, mod, gt_c, w_out, *, tm, tn, nr, row0, rows_per_batch, emit, name):
    t = x.shape[0]
    nn = D_MODEL // tn
    assert rows_per_batch & (rows_per_batch - 1) == 0 and rows_per_batch % nr == 0 and tm % nr == 0
    body = functools.partial(_outproj_body, nr=nr, row0=row0, batch_shift=rows_per_batch.bit_length() - 1,
                             emit=emit)
    wspec = pl.BlockSpec((D_MODEL, tn), lambda i, j: (0, j))
    out_specs = [pl.BlockSpec((tm, tn), lambda i, j: (i, j))]
    out_shape = [jax.ShapeDtypeStruct((t, D_MODEL), F32)]
    if emit:
        assert t == tm, "the bf16 weight copy is written once, by a single token tile"
        out_specs.append(wspec)
        out_shape.append(jax.ShapeDtypeStruct((D_MODEL, D_MODEL), BF16))
    return pl.pallas_call(
        body,
        grid=(t // tm, nn),
        in_specs=[
            pl.BlockSpec((tm, D_MODEL), lambda i, j: (i, 0)),
            pl.BlockSpec((tm, tn), lambda i, j: (i, j)),
            pl.BlockSpec((MOD_ROWS, tn), lambda i, j: (0, gt_c * nn + j)),
            wspec,
        ],
        out_specs=out_specs,
        out_shape=out_shape,
        compiler_params=_params(("parallel", "arbitrary")),
        name=name,
    )(mix, x, mod, w_out)


def _rope(x, cos, sin_signed):
    return x * cos + pltpu.roll(x, HEAD_DIM // 2, 1) * sin_signed


def _lru_gates(xc_ref, a_ref, wa_ref, ba_ref, wx_ref, bx_ref, lam_ref, midway=None):
    nlam = -lam_ref[...]
    softplus = jnp.maximum(nlam, 0.0) + jnp.log1p(jnp.exp(-jnp.abs(nlam)))
    rate = -LRU_C * softplus
    for hh in range(LRU_HEADS):
        if midway is not None:
            midway()
        cols = slice(hh * LRU_BLOCK, (hh + 1) * LRU_BLOCK)
        xc = xc_ref[:, cols]
        xcb = xc.astype(BF16)
        ra = jnp.dot(xcb, wa_ref[hh].astype(BF16), preferred_element_type=F32) + ba_ref[:, cols]
        rx = jnp.dot(xcb, wx_ref[hh].astype(BF16), preferred_element_type=F32) + bx_ref[:, cols]
        r = _sigmoid(ra)
        gi = _sigmoid(rx)
        a = jnp.exp(r * rate[:, cols])
        a_ref[:, cols] = a
        xc_ref[:, cols] = jnp.sqrt(1.0 - a * a) * (gi * xc)


def _scan_group(a, u, carry, row):
    for s in (1, 2, 4):
        a_sh = pltpu.roll(a, s, 0)
        u_sh = pltpu.roll(u, s, 0)
        m = row >= s
        u = jnp.where(m, a * u_sh + u, u)
        a = jnp.where(m, a * a_sh, a)
    return a * carry + u


def _softmax_pv(s, mask, sink, v):
    s = jnp.where(mask, s, NEG)
    m = jnp.maximum(jnp.max(s, axis=-1, keepdims=True), sink)
    p = jnp.exp(s - m)
    den = jnp.sum(p, axis=-1, keepdims=True) + jnp.exp(sink - m)
    return jnp.dot(p.astype(BF16), v, preferred_element_type=F32) / den


def _sink_column(sink_ref, kh, rows_per_head):
    n = GROUP * rows_per_head
    ri = lax.broadcasted_iota(jnp.int32, (n, 1), 0)
    col = jnp.full((n, 1), sink_ref[kh * GROUP + GROUP - 1], F32)
    for g in range(GROUP - 2, -1, -1):
        col = jnp.where(ri < (g + 1) * rows_per_head, sink_ref[kh * GROUP + g], col)
    return col


def _mix_tile(proj, mix, cs_ref, sn_ref, cw_ref, cb_ref, wa_ref, ba_ref, wx_ref, bx_ref, lam_ref, sink_ref,
              xpad, hc, kpad, vpad, a_s, u_s, klast, *, ts, pos0, mxu_fill):
    fill = iter(mxu_fill)

    def take(n):
        for _ in range(n):
            f = next(fill, None)
            if f is not None:
                f()

    w = LRU_WIDTH
    xpad[SUBLANES:SUBLANES + ts, :] = proj[:, 0:w]
    half = ts // 2
    for r0 in (0, half):
        take(1)
        xc = cb_ref[...]
        for jj in range(CONV_W):
            off = r0 + SUBLANES - (CONV_W - 1) + jj
            xc = xc + xpad[off:off + half, :] * cw_ref[jj:jj + 1, :]
        u_s[r0:r0 + half, :] = xc
    xpad[0:SUBLANES, :] = xpad[ts:ts + SUBLANES, :]

    _lru_gates(u_s, a_s, wa_ref, ba_ref, wx_ref, bx_ref, lam_ref, midway=lambda: take(1))

    row = lax.broadcasted_iota(jnp.int32, (SUBLANES, w), 0)
    carry = hc[...]
    ngroups = ts // SUBLANES
    for g in range(ngroups):
        if g % (ngroups // 8) == 0:
            take(1)
        rows = slice(g * SUBLANES, (g + 1) * SUBLANES)
        h = _scan_group(a_s[rows, :], u_s[rows, :], carry, row)
        a_s[rows, :] = h
        carry = jnp.broadcast_to(h[SUBLANES - 1:SUBLANES, :], (SUBLANES, w))
    hc[...] = carry
    for r0 in (0, half):
        take(1)
        rows = slice(r0, r0 + half)
        mix[rows, 0:w] = (a_s[rows, :] * jax.nn.gelu(proj[rows, w:2 * w])).astype(BF16)

    take(1)
    cos = cs_ref[...]
    sin = sn_ref[...]
    for kh in range(N_KV_HEADS):
        cols = slice(kh * HEAD_DIM, (kh + 1) * HEAD_DIM)
        kr = _rope(proj[:, K_COL + kh * HEAD_DIM:K_COL + (kh + 1) * HEAD_DIM], cos, sin)
        kpad[WINDOW:WINDOW + ts, cols] = kr.astype(BF16)
        klast[:, cols] = kr[ts - WINDOW:, :]
    vpad[WINDOW:WINDOW + ts, :] = proj[:, V_COL:V_COL + KV_WIDTH].astype(BF16)

    nq = GROUP * WINDOW
    r_i = lax.broadcasted_iota(jnp.int32, (nq, 2 * WINDOW), 0) & (WINDOW - 1)
    c_i = lax.broadcasted_iota(jnp.int32, (nq, 2 * WINDOW), 1)
    band = (c_i >= r_i) & (c_i <= r_i + WINDOW)
    for n in range(ts // WINDOW):
        qrows = slice(n * WINDOW, (n + 1) * WINDOW)
        kpos0 = pos0 + (n - 1) * WINDOW
        mask = band & (c_i + kpos0 >= 0)
        for kh in range(N_KV_HEADS):
            cols = slice(kh * HEAD_DIM, (kh + 1) * HEAD_DIM)
            qs = []
            for g in range(GROUP):
                hd = kh * GROUP + g
                qh = proj[qrows, Q_COL + hd * HEAD_DIM:Q_COL + (hd + 1) * HEAD_DIM]
                qs.append(_rope(qh, cos[qrows, :], sin[qrows, :]).astype(BF16))
            q4 = jnp.concatenate(qs, axis=0)
            kk = kpad[n * WINDOW:(n + 2) * WINDOW, cols]
            vv = vpad[n * WINDOW:(n + 2) * WINDOW, cols]
            s = lax.dot_general(q4, kk, (((1,), (1,)), ((), ())), preferred_element_type=F32) * SCALE
            take(1)
            o = _softmax_pv(s, mask, _sink_column(sink_ref, kh, WINDOW), vv)
            for g in range(GROUP):
                hd = kh * GROUP + g
                mix[qrows, w + hd * HEAD_DIM:w + (hd + 1) * HEAD_DIM] = (
                    o[g * WINDOW:(g + 1) * WINDOW, :].astype(BF16))

    take(len(mxu_fill))
    kpad[0:WINDOW, :] = kpad[ts:ts + WINDOW, :]
    vpad[0:WINDOW, :] = vpad[ts:ts + WINDOW, :]


def _mixer_body(xa_ref, xc_ref, sh_ref, sc_ref, gt_ref, gm_ref, win_ref, wout_ref, cs_ref, sn_ref,
                cw_ref, cb_ref, wa_ref, ba_ref, wx_ref, bx_ref, lam_ref, sink_ref,
                o_ref, lru_ref, k_ref, v_ref, conv_ref,
                proj0, proj1, mix0, mix1, h_s, xpad, hc, kpad, vpad, a_s, u_s, klast,
                *, ts, nt, ntiles, batch_shift):
    s = pl.program_id(0)
    tile_a = jnp.minimum(s, ntiles - 1)
    tile_b = jnp.clip(s - 1, 0, ntiles - 1)
    tile_c = jnp.clip(s - 2, 0, ntiles - 1)
    t_in = tile_b & (nt - 1)

    @pl.when(s == 0)
    def _():
        proj1[...] = jnp.zeros(proj1.shape, F32)
        mix0[...] = jnp.zeros(mix0.shape, BF16)

    @pl.when(t_in == 0)
    def _():
        xpad[0:SUBLANES, :] = jnp.zeros((SUBLANES, LRU_WIDTH), F32)
        hc[...] = jnp.zeros((SUBLANES, LRU_WIDTH), F32)
        kpad[0:WINDOW, :] = jnp.zeros((WINDOW, KV_WIDTH), BF16)
        vpad[0:WINDOW, :] = jnp.zeros((WINDOW, KV_WIDTH), BF16)

    def step(par):
        proj_w, proj_r = (proj0, proj1) if par == 0 else (proj1, proj0)
        mix_w, mix_r = (mix1, mix0) if par == 0 else (mix0, mix1)

        gt = _mod_row(gt_ref, tile_c * ts, 0, batch_shift)

        def out_chunk(k):
            cols = slice(k * MIXER_TN, (k + 1) * MIXER_TN)

            def run():
                d = jnp.dot(mix_r[...], wout_ref[:, cols], preferred_element_type=F32)
                o_ref[:, cols] = xc_ref[:, cols] + gt[:, cols] * d
            return run

        def in_chunk(k):
            cols = slice(k * MIXER_TN, (k + 1) * MIXER_TN)

            def run():
                if k == 0:
                    g = gm_ref[...]
                    sc = _mod_row(sc_ref, tile_a * ts, 0, batch_shift)
                    sh = _mod_row(sh_ref, tile_a * ts, 0, batch_shift)
                    for r in range(ts // FFN_ROWS):
                        rows = slice(r * FFN_ROWS, (r + 1) * FFN_ROWS)
                        h_s[rows, :] = _normmod(xa_ref[rows, :], g, sc, sh).astype(BF16)
                proj_w[:, cols] = jnp.dot(h_s[...], win_ref[:, cols], preferred_element_type=F32)
            return run

        fill = ([in_chunk(k) for k in range(IN_COLS // MIXER_TN)]
                + [out_chunk(k) for k in range(D_MODEL // MIXER_TN)])
        _mix_tile(proj_r, mix_w, cs_ref, sn_ref, cw_ref, cb_ref, wa_ref, ba_ref, wx_ref, bx_ref, lam_ref,
                  sink_ref, xpad, hc, kpad, vpad, a_s, u_s, klast, ts=ts, pos0=t_in * ts, mxu_fill=fill)

        @pl.when((s >= 1) & (s <= ntiles))
        def _():
            lru_ref[0] = hc[0:1, :]
            k_ref[0] = klast[...]
            v_ref[0] = proj_r[ts - WINDOW:, V_COL:V_COL + KV_WIDTH]
            conv_ref[0] = xpad[0:SUBLANES, :]

    @pl.when((s & 1) == 0)
    def _():
        step(0)

    @pl.when((s & 1) == 1)
    def _():
        step(1)


def _mixer_prompt(x, mod, g_mix, w_in_b, w_out_b, cos, sin, p, *, batch, seq):
    ts = MIX_TS
    nt = seq // ts
    ntiles = batch * nt
    assert nt & (nt - 1) == 0 and seq & (seq - 1) == 0
    w = LRU_WIDTH
    body = functools.partial(_mixer_body, ts=ts, nt=nt, ntiles=ntiles, batch_shift=seq.bit_length() - 1)

    def tile_a(s):
        return jnp.minimum(s, ntiles - 1)

    def tile_b(s):
        return jnp.clip(s - 1, 0, ntiles - 1)

    def tile_c(s):
        return jnp.clip(s - 2, 0, ntiles - 1)

    def whole(shape, **kw):
        return pl.BlockSpec(shape, lambda s: (0,) * len(shape), **kw)

    def mspec(c):
        return pl.BlockSpec((MOD_ROWS, D_MODEL), lambda s: (0, c))

    def state(shape):
        return pl.BlockSpec((1,) + shape, lambda s: (tile_b(s) // nt, 0, 0))

    once = dict(pipeline_mode=pl.Buffered(1))
    return pl.pallas_call(
        body,
        grid=(ntiles + 2,),
        in_specs=[
            pl.BlockSpec((ts, D_MODEL), lambda s: (tile_a(s), 0)),
            pl.BlockSpec((ts, D_MODEL), lambda s: (tile_c(s), 0)),
            mspec(3), mspec(4), mspec(5),
            whole((1, D_MODEL)),
            whole((D_MODEL, IN_COLS), **once),
            whole((D_MODEL, D_MODEL), **once),
            pl.BlockSpec((ts, HEAD_DIM), lambda s: (tile_b(s) % nt, 0)),
            pl.BlockSpec((ts, HEAD_DIM), lambda s: (tile_b(s) % nt, 0)),
            whole((CONV_W, w)), whole((1, w)),
            whole((LRU_HEADS, LRU_BLOCK, LRU_BLOCK)), whole((1, w)),
            whole((LRU_HEADS, LRU_BLOCK, LRU_BLOCK)), whole((1, w)),
            whole((1, w)),
            pl.BlockSpec(memory_space=pltpu.SMEM),
        ],
        out_specs=[
            pl.BlockSpec((ts, D_MODEL), lambda s: (tile_c(s), 0)),
            state((1, w)), state((WINDOW, KV_WIDTH)), state((WINDOW, KV_WIDTH)), state((SUBLANES, w)),
        ],
        out_shape=[
            jax.ShapeDtypeStruct((batch * seq, D_MODEL), F32),
            jax.ShapeDtypeStruct((batch, 1, w), F32),
            jax.ShapeDtypeStruct((batch, WINDOW, KV_WIDTH), F32),
            jax.ShapeDtypeStruct((batch, WINDOW, KV_WIDTH), F32),
            jax.ShapeDtypeStruct((batch, SUBLANES, w), F32),
        ],
        scratch_shapes=[
            pltpu.VMEM((ts, IN_COLS), F32), pltpu.VMEM((ts, IN_COLS), F32),
            pltpu.VMEM((ts, D_MODEL), BF16), pltpu.VMEM((ts, D_MODEL), BF16),
            pltpu.VMEM((ts, D_MODEL), BF16),
            pltpu.VMEM((SUBLANES + ts, w), F32),
            pltpu.VMEM((SUBLANES, w), F32),
            pltpu.VMEM((WINDOW + ts, KV_WIDTH), BF16),
            pltpu.VMEM((WINDOW + ts, KV_WIDTH), BF16),
            pltpu.VMEM((ts, w), F32),
            pltpu.VMEM((ts, w), F32),
            pltpu.VMEM((WINDOW, KV_WIDTH), F32),
        ],
        compiler_params=_params(("arbitrary",)),
        name="mixer_prompt",
    )(x, x, mod, mod, mod, g_mix.reshape(1, D_MODEL), w_in_b, w_out_b, cos, sin,
      p["conv_w"], p["conv_b"], p["wa"], p["ba"], p["wx"], p["bx"], p["lam"], p["sinks"])


def _mix_sample_body(proj_ref, prev_ref, h0_ref, ck_ref, cv_ref, cs_ref, sn_ref, cw_ref, cb_ref,
                     wa_ref, ba_ref, wx_ref, bx_ref, lam_ref, sink_ref,
                     mix_ref, lru_ref, knew_ref,
                     a_s, u_s, q_s, *, batch, seq):
    w = LRU_WIDTH
    row = lax.broadcasted_iota(jnp.int32, (SUBLANES, w), 0)

    def group(b):
        return pl.ds(pl.multiple_of(b * SUBLANES, SUBLANES), SUBLANES)

    def conv_body(b, carry):
        rows = group(b)
        cur = proj_ref[rows, 0:w]
        prev = prev_ref[rows, :]
        xc = cb_ref[...]
        for jj in range(CONV_W):
            d = CONV_W - 1 - jj
            if d == 0:
                term = cur
            else:
                term = jnp.where(row >= d, pltpu.roll(cur, d, 0), pltpu.roll(prev, d, 0))
            xc = xc + term * cw_ref[jj:jj + 1, :]
        u_s[rows, :] = xc
        return carry

    lax.fori_loop(0, batch, conv_body, 0)

    _lru_gates(u_s, a_s, wa_ref, ba_ref, wx_ref, bx_ref, lam_ref)

    def scan_body(b, carry):
        rows = group(b)
        h0 = jnp.broadcast_to(h0_ref[pl.ds(b, 1), :], (SUBLANES, w))
        h = _scan_group(a_s[rows, :], u_s[rows, :], h0, row)
        a_s[rows, :] = h
        lru_ref[pl.ds(b, 1), :] = h[SUBLANES - 1:SUBLANES, :]
        return carry

    lax.fori_loop(0, batch, scan_body, 0)
    mix_ref[:, 0:w] = a_s[...] * jax.nn.gelu(proj_ref[:, w:2 * w])

    cos = cs_ref[...]
    sin = sn_ref[...]
    for hd in range(N_HEADS):
        cols = slice(hd * HEAD_DIM, (hd + 1) * HEAD_DIM)
        q_s[:, cols] = _rope(proj_ref[:, Q_COL + hd * HEAD_DIM:Q_COL + (hd + 1) * HEAD_DIM], cos, sin)
    for kh in range(N_KV_HEADS):
        cols = slice(kh * HEAD_DIM, (kh + 1) * HEAD_DIM)
        knew_ref[:, cols] = _rope(proj_ref[:, K_COL + kh * HEAD_DIM:K_COL + (kh + 1) * HEAD_DIM], cos, sin)

    nq = GROUP * seq
    nk = 2 * WINDOW
    r_i = lax.broadcasted_iota(jnp.int32, (nq, nk), 0) & (seq - 1)
    c_i = lax.broadcasted_iota(jnp.int32, (nq, nk), 1)
    mask = (c_i >= r_i) & (c_i <= r_i + WINDOW)
    pad = jnp.zeros((nk - WINDOW - seq, HEAD_DIM), F32)

    def attn_body(b, carry):
        rows = group(b)
        for kh in range(N_KV_HEADS):
            cols = slice(kh * HEAD_DIM, (kh + 1) * HEAD_DIM)
            q4 = jnp.concatenate(
                [q_s[rows, (kh * GROUP + g) * HEAD_DIM:(kh * GROUP + g + 1) * HEAD_DIM] for g in range(GROUP)],
                axis=0).astype(BF16)
            kk = jnp.concatenate([ck_ref[b, :, cols], knew_ref[rows, cols], pad], axis=0).astype(BF16)
            vv = jnp.concatenate(
                [cv_ref[b, :, cols], proj_ref[rows, V_COL + kh * HEAD_DIM:V_COL + (kh + 1) * HEAD_DIM], pad],
                axis=0).astype(BF16)
            s = lax.dot_general(q4, kk, (((1,), (1,)), ((), ())), preferred_element_type=F32) * SCALE
            o = _softmax_pv(s, mask, _sink_column(sink_ref, kh, seq), vv)
            for g in range(GROUP):
                hd = kh * GROUP + g
                mix_ref[rows, w + hd * HEAD_DIM:w + (hd + 1) * HEAD_DIM] = o[g * seq:(g + 1) * seq, :]
        return carry

    lax.fori_loop(0, batch, attn_body, 0)


def _mix_sample(proj, prev, h0, cache_k, cache_v, cos, sin, p, *, batch, seq):
    assert seq == SUBLANES, "each sample batch must be exactly one sublane group"
    t = batch * seq
    body = functools.partial(_mix_sample_body, batch=batch, seq=seq)
    vmem = pl.BlockSpec(memory_space=pltpu.VMEM)
    return pl.pallas_call(
        body,
        in_specs=[vmem] * 14 + [pl.BlockSpec(memory_space=pltpu.SMEM)],
        out_specs=[vmem, vmem, vmem],
        out_shape=[
            jax.ShapeDtypeStruct((t, D_MODEL), F32),
            jax.ShapeDtypeStruct((batch, LRU_WIDTH), F32),
            jax.ShapeDtypeStruct((t, KV_WIDTH), F32),
        ],
        scratch_shapes=[
            pltpu.VMEM((t, LRU_WIDTH), F32),
            pltpu.VMEM((t, LRU_WIDTH), F32),
            pltpu.VMEM((t, ATTN_WIDTH), F32),
        ],
        compiler_params=pltpu.CompilerParams(vmem_limit_bytes=V7X_VMEM_LIMIT_BYTES),
        name="mix_sample",
    )(proj, prev, h0, cache_k, cache_v, cos, sin, p["conv_w"], p["conv_b"], p["wa"], p["ba"], p["wx"], p["bx"],
      p["lam"], p["sinks"])


def _layer(w, l):
    return w.reshape(w.shape[1:]) if w.shape[0] == 1 else w[l]


def _rope_tables(pos):
    half = HEAD_DIM // 2
    inv = ROPE_THETA ** (-jnp.arange(half, dtype=F32) / half)
    ang = pos.astype(F32)[:, None] * inv[None, :]
    cos = jnp.cos(ang)
    sin = jnp.sin(ang)
    return jnp.concatenate([cos, cos], axis=-1), jnp.concatenate([-sin, sin], axis=-1)


def kernel(x_prompt, x_sample, c_prompt, c_sample, state_conv, state_lru, cache_k_win, cache_v_win, ada_w, ada_b, norm_ffn1, norm_mix, norm_ffn2, ffn1_w_gate, ffn1_w_up, ffn1_w_down, w_in, conv_w, conv_b, lru_w_a, lru_b_a, lru_w_x, lru_b_x, lru_lambda, attn_sinks, w_out, ffn2_w_gate, ffn2_w_up, ffn2_w_down, norm_final):
    bp, sp, _ = x_prompt.shape
    bs, ss, _ = x_sample.shape
    depth = ada_w.shape[0]
    tp, tsn = bp * sp, bs * ss
    assert bp <= MOD_PROMPT_ROWS and bp + bs <= MOD_ROWS
    tm_p = 1024
    assert sp % tm_p == 0

    xp = x_prompt.reshape(tp, D_MODEL)
    xs = x_sample.reshape(tsn, D_MODEL)
    c_all = jnp.concatenate(
        [c_prompt, jnp.zeros((MOD_PROMPT_ROWS - bp, D_MODEL), F32), c_sample,
         jnp.zeros((MOD_ROWS - MOD_PROMPT_ROWS - bs, D_MODEL), F32)], axis=0)
    cos_p, sin_p = _rope_tables(jnp.arange(sp))
    cos_s, sin_s = _rope_tables(PAST_LEN + jnp.arange(ss))
    cos_s = jnp.tile(cos_s, (bs, 1))
    sin_s = jnp.tile(sin_s, (bs, 1))

    outs_p, outs_s = [], []
    for l in range(depth):
        last = l == depth - 1
        L = functools.partial(_layer, l=l)
        cache_k, cache_v = L(cache_k_win), L(cache_v_win)
        mod = _adaln(c_all, L(ada_w), L(ada_b))
        p = dict(conv_w=L(conv_w), conv_b=L(conv_b).reshape(1, LRU_WIDTH), wa=L(lru_w_a),
                 ba=L(lru_b_a).reshape(1, LRU_WIDTH), wx=L(lru_w_x), bx=L(lru_b_x).reshape(1, LRU_WIDTH),
                 lam=L(lru_lambda).reshape(1, LRU_WIDTH), sinks=L(attn_sinks))
        ffn1 = (L(norm_ffn1), L(ffn1_w_gate), L(ffn1_w_up), L(ffn1_w_down), norm_final)
        ffn2 = (L(norm_ffn2), L(ffn2_w_gate), L(ffn2_w_up), L(ffn2_w_down), norm_final)
        grp_p = dict(row0=0, rows_per_batch=sp, emit=False)
        grp_s = dict(tm=tsn, nr=ss, row0=MOD_PROMPT_ROWS, rows_per_batch=ss, emit=True)
        ffn_p = dict(tm=tm_p, nr=FFN_ROWS, **grp_p)
        ffn_s = grp_s

        xs, wg_b, wu_b, wd_b = _ffn(xs, mod, (0, 1, 2), *ffn1, final_norm=False, name="ffn1_sample", **ffn_s)
        xp, = _ffn(xp, mod, (0, 1, 2), ffn1[0], wg_b, wu_b, wd_b, norm_final, final_norm=False,
                   name="ffn1_prompt", **ffn_p)

        proj_s, w_in_b = _inproj(xs, mod, (3, 4), L(norm_mix), L(w_in), tn=PROJ_TN, name="inproj_sample",
                                 **grp_s)
        prev_s = jnp.pad(L(state_conv), ((0, 0), (SUBLANES - (CONV_W - 1), 0), (0, 0))).reshape(tsn, LRU_WIDTH)
        mix_s, lru_s, knew_s = _mix_sample(
            proj_s, prev_s, L(state_lru), cache_k.reshape(bs, WINDOW, KV_WIDTH),
            cache_v.reshape(bs, WINDOW, KV_WIDTH), cos_s, sin_s, p, batch=bs, seq=ss)
        xs, w_out_b = _outproj(mix_s, xs, mod, 5, L(w_out), tn=PROJ_TN, name="outproj_sample", **grp_s)
        xp, lru_p, k_p, v_p, conv_p = _mixer_prompt(xp, mod, L(norm_mix), w_in_b, w_out_b, cos_p, sin_p, p,
                                                    batch=bp, seq=sp)

        xs, wg_b, wu_b, wd_b = _ffn(xs, mod, (6, 7, 8), *ffn2, final_norm=last, name="ffn2_sample", **ffn_s)
        xp, = _ffn(xp, mod, (6, 7, 8), ffn2[0], wg_b, wu_b, wd_b, norm_final, final_norm=last,
                   name="ffn2_prompt", **ffn_p)

        ps = proj_s.reshape(bs, ss, IN_COLS)
        outs_p.append((
            conv_p[:, SUBLANES - (CONV_W - 1):, :],
            lru_p.reshape(bp, LRU_WIDTH),
            k_p.reshape(bp, WINDOW, N_KV_HEADS, HEAD_DIM),
            v_p.reshape(bp, WINDOW, N_KV_HEADS, HEAD_DIM),
        ))
        k_all = jnp.concatenate([cache_k, knew_s.reshape(bs, ss, N_KV_HEADS, HEAD_DIM)], axis=1)
        v_all = jnp.concatenate([cache_v, ps[:, :, V_COL:].reshape(bs, ss, N_KV_HEADS, HEAD_DIM)], axis=1)
        outs_s.append((
            ps[:, ss - (CONV_W - 1):, 0:LRU_WIDTH],
            lru_s,
            k_all[:, -WINDOW:],
            v_all[:, -WINDOW:],
        ))

    y_prompt = xp.reshape(bp, sp, D_MODEL)
    y_sample = xs.reshape(bs, ss, D_MODEL)
    st_p = [jnp.stack([o[i] for o in outs_p]) for i in range(4)]
    st_s = [jnp.stack([o[i] for o in outs_s]) for i in range(4)]
    return (y_prompt, y_sample, st_p[0], st_p[1], st_p[2], st_p[3], st_s[0], st_s[1], st_s[2], st_s[3])
```

```python
import functools

import jax
import jax.numpy as jnp
from jax import lax
from jax.experimental import pallas as pl
from jax.experimental.pallas import tpu as pltpu

F32 = jnp.float32
BF16 = jnp.bfloat16

D_MODEL = 2048
D_FF = 5504
LRU_WIDTH = 1024
LRU_HEADS = 8
LRU_BLOCK = 128
CONV_W = 4
LRU_C = 8.0
HEAD_DIM = 128
N_HEADS = 8
N_KV_HEADS = 2
GROUP = N_HEADS // N_KV_HEADS
ATTN_WIDTH = N_HEADS * HEAD_DIM
KV_WIDTH = N_KV_HEADS * HEAD_DIM
WINDOW = 128
ROPE_THETA = 10000.0
N_MOD = 9
EPS = 1e-6
NEG = -1e30
PAST_LEN = 16384
IN_COLS = 2 * LRU_WIDTH + ATTN_WIDTH + 2 * KV_WIDTH
Q_COL = 2 * LRU_WIDTH
K_COL = Q_COL + ATTN_WIDTH
V_COL = K_COL + KV_WIDTH
SCALE = HEAD_DIM ** -0.5

SUBLANES = 8
LANES = 128
V7X_VMEM_LIMIT_BYTES = 56 * 1024 * 1024

MOD_ROWS = 48
MOD_PROMPT_ROWS = 8
ADALN_TN = 1024
BF16_ROWS = 16
FFN_TF = 512
FFN_EMIT_TF = 256
FFN_ROWS = 16
FFN_EPILOGUE_ROWS = 128
FFN_ROW_BLOCK = 256
FFN_FINISH_ROWS = 32
PROJ_TN = 512
OUT_TM = 512
MIX_TS = 256
MIXER_TN = 256


def _params(sem, vmem=V7X_VMEM_LIMIT_BYTES):
    return pltpu.CompilerParams(dimension_semantics=sem, vmem_limit_bytes=vmem)


def _sigmoid(x):
    return 1.0 / (1.0 + jnp.exp(-x))


def _normmod(x, g, sc, sh):
    ms = jnp.mean(x * x, axis=-1, keepdims=True)
    xn = x * lax.rsqrt(ms + EPS) * g
    return xn * (1.0 + sc) + sh


def _mod_row(ref, token, row0, batch_shift):
    return ref[pl.ds(row0 + lax.shift_right_logical(token, batch_shift), 1), :]


def _mod_spec(chunk):
    return pl.BlockSpec((MOD_ROWS, D_MODEL), lambda i, j: (0, chunk))


def _adaln_body(c_ref, w_ref, b_ref, o_ref):
    c = c_ref[...]
    s = (c * _sigmoid(c)).astype(BF16)
    o_ref[...] = jnp.dot(s, w_ref[...].astype(BF16), preferred_element_type=F32) + b_ref[...]


def _adaln(c_all, ada_w, ada_b):
    n = ada_w.shape[1]
    return pl.pallas_call(
        _adaln_body,
        grid=(n // ADALN_TN,),
        in_specs=[
            pl.BlockSpec((MOD_ROWS, D_MODEL), lambda j: (0, 0)),
            pl.BlockSpec((D_MODEL, ADALN_TN), lambda j: (0, j)),
            pl.BlockSpec((1, ADALN_TN), lambda j: (0, j)),
        ],
        out_specs=pl.BlockSpec((MOD_ROWS, ADALN_TN), lambda j: (0, j)),
        out_shape=jax.ShapeDtypeStruct((MOD_ROWS, n), F32),
        compiler_params=_params(("arbitrary",)),
        name="adaln",
    )(c_all, ada_w, ada_b.reshape(1, n))


def _ffn_body(x_ref, sh_ref, sc_ref, gt_ref, g_ref, wg_ref, wu_ref, wd_ref, gf_ref, *rest,
              nr, er, rb, row0, batch_shift, tile0, final_norm, emit, aliased):
    if aliased:
        rest = rest[1:]
    if emit:
        o_ref, wgb_ref, wub_ref, wdb_ref, h_ref = rest
    else:
        o_ref, h_ref = rest
    i = pl.program_id(0) + tile0
    j = pl.program_id(1)
    nj = pl.num_programs(1)
    tm = x_ref.shape[0]
    tf = wd_ref.shape[0]

    if emit:
        valid = D_FF - j * tf
        cmask = lax.broadcasted_iota(jnp.int32, (1, tf), 1) < valid
        rmask = lax.broadcasted_iota(jnp.int32, (tf, 1), 0) < valid
        wgb_ref[...] = jnp.where(cmask, wg_ref[...], 0.0).astype(BF16)
        wub_ref[...] = jnp.where(cmask, wu_ref[...], 0.0).astype(BF16)
        wdb_ref[...] = jnp.where(rmask, wd_ref[...], 0.0).astype(BF16)
        wg_ref, wu_ref, wd_ref = wgb_ref, wub_ref, wdb_ref

    def swiglu(rows):
        h = h_ref[rows, :].astype(BF16)
        g = jnp.dot(h, wg_ref[...], preferred_element_type=F32)
        u = jnp.dot(h, wu_ref[...], preferred_element_type=F32)
        a = (g * _sigmoid(g) * u).astype(BF16)
        return jnp.dot(a, wd_ref[...], preferred_element_type=F32)

    def prologue(rows, token):
        h = _normmod(x_ref[rows, :], g_ref[...], _mod_row(sc_ref, token, row0, batch_shift),
                     _mod_row(sh_ref, token, row0, batch_shift))
        h_ref[rows, :] = h.astype(h_ref.dtype)

    def epilogue(rows, token, acc):
        y = x_ref[rows, :] + (0.5 * _mod_row(gt_ref, token, row0, batch_shift)) * acc
        if final_norm:
            ms = jnp.mean(y * y, axis=-1, keepdims=True)
            y = y * lax.rsqrt(ms + EPS) * gf_ref[...]
        o_ref[rows, :] = y

    if rb:
        blocks = range(0, tm, rb)

        def block_prologue(r0):
            for c in range(r0, r0 + rb, nr):
                prologue(slice(c, c + nr), i * tm + c)

        @pl.when(j == 0)
        def _():
            block_prologue(0)
            for r0 in blocks:
                o_ref[r0:r0 + rb, :] = swiglu(slice(r0, r0 + rb))
                if r0 + rb < tm:
                    block_prologue(r0 + rb)

        @pl.when((j > 0) & (j < nj - 1))
        def _():
            o_ref[...] += swiglu(slice(None))

        @pl.when(j == nj - 1)
        def _():
            for r0 in blocks:
                acc = o_ref[r0:r0 + rb, :] + swiglu(slice(r0, r0 + rb))
                for c in range(0, rb, FFN_FINISH_ROWS):
                    epilogue(slice(r0 + c, r0 + c + FFN_FINISH_ROWS), i * tm + r0 + c,
                             acc[c:c + FFN_FINISH_ROWS, :])
    else:
        @pl.when(j == 0)
        def _():
            def body(r, carry):
                rows = pl.ds(pl.multiple_of(r * nr, nr), nr)
                prologue(rows, i * tm + r * nr)
                o_ref[rows, :] = jnp.zeros((nr, D_MODEL), F32)
                return carry

            lax.fori_loop(0, tm // nr, body, 0, unroll=4)

        o_ref[...] += swiglu(slice(None))

        @pl.when(j == nj - 1)
        def _():
            def body(r, carry):
                rows = pl.ds(pl.multiple_of(r * er, er), er)
                epilogue(rows, i * tm + r * er, o_ref[rows, :])
                return carry

            lax.fori_loop(0, tm // er, body, 0)


def _ffn(x, mod, chunks, g, wg, wu, wd, gf, *, tm, nr, row0, rows_per_batch, final_norm, emit, name,
         tile0=0, ntiles=None, into=None):
    t = x.shape[0]
    ntiles = t // tm if ntiles is None else ntiles
    tf = FFN_EMIT_TF if emit else FFN_TF
    sub = FFN_TF // tf
    nj_store = pl.cdiv(D_FF, FFN_TF)
    nj = nj_store * sub
    er = min(FFN_EPILOGUE_ROWS, rows_per_batch)
    assert rows_per_batch & (rows_per_batch - 1) == 0 and rows_per_batch % nr == 0 and tm % nr == 0
    assert rows_per_batch % er == 0 and tm % er == 0
    rb = FFN_ROW_BLOCK if rows_per_batch % FFN_ROW_BLOCK == 0 and tm % FFN_ROW_BLOCK == 0 else 0
    body = functools.partial(_ffn_body, nr=nr, er=er, rb=rb, row0=row0,
                             batch_shift=rows_per_batch.bit_length() - 1, tile0=tile0,
                             final_norm=final_norm, emit=emit, aliased=into is not None)

    row = pl.BlockSpec((1, D_MODEL), lambda i, j: (0, 0))
    tile_major = pl.BlockSpec((None, D_MODEL, tf), lambda i, j: (j // sub, 0, j % sub))
    if emit:
        assert ntiles == 1, "bf16 weight tiles are written once, by a single token tile"
        wspecs = [pl.BlockSpec((D_MODEL, tf), lambda i, j: (0, j)),
                  pl.BlockSpec((D_MODEL, tf), lambda i, j: (0, j)),
                  pl.BlockSpec((tf, D_MODEL), lambda i, j: (j, 0))]
    else:
        wspecs = [tile_major, tile_major, pl.BlockSpec((tf, D_MODEL), lambda i, j: (j, 0))]
    xmode = dict(pipeline_mode=pl.Buffered(1)) if ntiles == 1 else {}
    in_specs = [pl.BlockSpec((tm, D_MODEL), lambda i, j: (i + tile0, 0), **xmode),
                _mod_spec(chunks[0]), _mod_spec(chunks[1]), _mod_spec(chunks[2]), row] + wspecs + [row]
    args = [x, mod, mod, mod, g.reshape(1, D_MODEL), wg, wu, wd, gf.reshape(1, D_MODEL)]
    aliases = {}
    if into is not None:
        in_specs.append(pl.BlockSpec(memory_space=pl.ANY))
        aliases = {len(args): 0}
        args.append(into)
    out_specs = [pl.BlockSpec((tm, D_MODEL), lambda i, j: (i + tile0, 0))]
    out_shape = [jax.ShapeDtypeStruct((t, D_MODEL), F32)]
    if emit:
        out_specs += [tile_major, tile_major, pl.BlockSpec((tf, D_MODEL), lambda i, j: (j, 0))]
        out_shape += [jax.ShapeDtypeStruct((nj_store, D_MODEL, FFN_TF), BF16),
                      jax.ShapeDtypeStruct((nj_store, D_MODEL, FFN_TF), BF16),
                      jax.ShapeDtypeStruct((nj_store * FFN_TF, D_MODEL), BF16)]
    return pl.pallas_call(
        body,
        grid=(ntiles, nj),
        in_specs=in_specs,
        out_specs=out_specs,
        out_shape=out_shape,
        input_output_aliases=aliases,
        scratch_shapes=[pltpu.VMEM((tm, D_MODEL), BF16 if nr % BF16_ROWS == 0 else F32)],
        compiler_params=_params(("parallel", "arbitrary")),
        name=name,
    )(*args)


def _inproj_body(x_ref, sh_ref, sc_ref, g_ref, w_ref, *rest, nr, row0, batch_shift, emit):
    if emit:
        o_ref, wb_ref, h_ref = rest
    else:
        o_ref, h_ref = rest
    i = pl.program_id(0)
    j = pl.program_id(1)
    tm = x_ref.shape[0]

    @pl.when(j == 0)
    def _():
        g = g_ref[...]

        def body(r, carry):
            rows = pl.ds(pl.multiple_of(r * nr, nr), nr)
            tok = i * tm + r * nr
            h = _normmod(x_ref[rows, :], g, _mod_row(sc_ref, tok, row0, batch_shift),
                         _mod_row(sh_ref, tok, row0, batch_shift))
            h_ref[rows, :] = h.astype(h_ref.dtype)
            return carry

        lax.fori_loop(0, tm // nr, body, 0, unroll=4)

    w = w_ref[...].astype(BF16)
    if emit:
        wb_ref[...] = w
    o_ref[...] = jnp.dot(h_ref[...].astype(BF16), w, preferred_element_type=F32)


def _inproj(x, mod, chunks, g, w_in, *, tm, tn, nr, row0, rows_per_batch, emit, name):
    t = x.shape[0]
    assert rows_per_batch & (rows_per_batch - 1) == 0 and rows_per_batch % nr == 0 and tm % nr == 0
    body = functools.partial(_inproj_body, nr=nr, row0=row0, batch_shift=rows_per_batch.bit_length() - 1,
                             emit=emit)
    wspec = pl.BlockSpec((D_MODEL, tn), lambda i, j: (0, j))
    out_specs = [pl.BlockSpec((tm, tn), lambda i, j: (i, j))]
    out_shape = [jax.ShapeDtypeStruct((t, IN_COLS), F32)]
    if emit:
        assert t == tm, "the bf16 weight copy is written once, by a single token tile"
        out_specs.append(wspec)
        out_shape.append(jax.ShapeDtypeStruct((D_MODEL, IN_COLS), BF16))
    return pl.pallas_call(
        body,
        grid=(t // tm, IN_COLS // tn),
        in_specs=[
            pl.BlockSpec((tm, D_MODEL), lambda i, j: (i, 0)),
            _mod_spec(chunks[0]), _mod_spec(chunks[1]),
            pl.BlockSpec((1, D_MODEL), lambda i, j: (0, 0)),
            wspec,
        ],
        out_specs=out_specs,
        out_shape=out_shape,
        scratch_shapes=[pltpu.VMEM((tm, D_MODEL), F32 if emit else BF16)],
        compiler_params=_params(("parallel", "arbitrary")),
        name=name,
    )(x, mod, mod, g.reshape(1, D_MODEL), w_in)


def _outproj_body(m_ref, x_ref, gt_ref, w_ref, *rest, nr, row0, batch_shift, emit):
    if emit:
        o_ref, wb_ref = rest
    else:
        o_ref, = rest
    i = pl.program_id(0)
    tm = x_ref.shape[0]
    w = w_ref[...].astype(BF16)
    if emit:
        wb_ref[...] = w
    d = jnp.dot(m_ref[...].astype(BF16), w, preferred_element_type=F32)
    if nr == tm:
        o_ref[...] = x_ref[...] + _mod_row(gt_ref, i * tm, row0, batch_shift) * d
    else:
        o_ref[...] = d

        def body(r, carry):
            rows = pl.ds(pl.multiple_of(r * nr, nr), nr)
            gt = _mod_row(gt_ref, i * tm + r * nr, row0, batch_shift)
            o_ref[rows, :] = x_ref[rows, :] + gt * o_ref[rows, :]
            return carry

        lax.fori_loop(0, tm // nr, body, 0)


def _outproj(mix, x, mod, gt_c, w_out, *, tm, tn, nr, row0, rows_per_batch, emit, name):
    t = x.shape[0]
    nn = D_MODEL // tn
    assert rows_per_batch & (rows_per_batch - 1) == 0 and rows_per_batch % nr == 0 and tm % nr == 0
    body = functools.partial(_outproj_body, nr=nr, row0=row0, batch_shift=rows_per_batch.bit_length() - 1,
                             emit=emit)
    wspec = pl.BlockSpec((D_MODEL, tn), lambda i, j: (0, j))
    out_specs = [pl.BlockSpec((tm, tn), lambda i, j: (i, j))]
    out_shape = [jax.ShapeDtypeStruct((t, D_MODEL), F32)]
    if emit:
        assert t == tm, "the bf16 weight copy is written once, by a single token tile"
        out_specs.append(wspec)
        out_shape.append(jax.ShapeDtypeStruct((D_MODEL, D_MODEL), BF16))
    return pl.pallas_call(
        body,
        grid=(t // tm, nn),
        in_specs=[
            pl.BlockSpec((tm, D_MODEL), lambda i, j: (i, 0)),
            pl.BlockSpec((tm, tn), lambda i, j: (i, j)),
            pl.BlockSpec((MOD_ROWS, tn), lambda i, j: (0, gt_c * nn + j)),
            wspec,
        ],
        out_specs=out_specs,
        out_shape=out_shape,
        compiler_params=_params(("parallel", "arbitrary")),
        name=name,
    )(mix, x, mod, w_out)


def _rope(x, cos, sin_signed):
    return x * cos + pltpu.roll(x, HEAD_DIM // 2, 1) * sin_signed


def _lru_gates(xc_ref, a_ref, wa_ref, ba_ref, wx_ref, bx_ref, lam_ref, midway=None):
    nlam = -lam_ref[...]
    softplus = jnp.maximum(nlam, 0.0) + jnp.log1p(jnp.exp(-jnp.abs(nlam)))
    rate = -LRU_C * softplus
    for hh in range(LRU_HEADS):
        if midway is not None:
            midway()
        cols = slice(hh * LRU_BLOCK, (hh + 1) * LRU_BLOCK)
        xc = xc_ref[:, cols]
        xcb = xc.astype(BF16)
        ra = jnp.dot(xcb, wa_ref[hh].astype(BF16), preferred_element_type=F32) + ba_ref[:, cols]
        rx = jnp.dot(xcb, wx_ref[hh].astype(BF16), preferred_element_type=F32) + bx_ref[:, cols]
        r = _sigmoid(ra)
        gi = _sigmoid(rx)
        a = jnp.exp(r * rate[:, cols])
        a_ref[:, cols] = a
        xc_ref[:, cols] = jnp.sqrt(1.0 - a * a) * (gi * xc)


def _scan_group(a, u, carry, row):
    for s in (1, 2, 4):
        a_sh = pltpu.roll(a, s, 0)
        u_sh = pltpu.roll(u, s, 0)
        m = row >= s
        u = jnp.where(m, a * u_sh + u, u)
        a = jnp.where(m, a * a_sh, a)
    return a * carry + u


def _softmax_pv(s, mask, sink, v):
    s = jnp.where(mask, s, NEG)
    m = jnp.maximum(jnp.max(s, axis=-1, keepdims=True), sink)
    p = jnp.exp(s - m)
    den = jnp.sum(p, axis=-1, keepdims=True) + jnp.exp(sink - m)
    return jnp.dot(p.astype(BF16), v, preferred_element_type=F32) / den


def _sink_column(sink_ref, kh, rows_per_head):
    n = GROUP * rows_per_head
    ri = lax.broadcasted_iota(jnp.int32, (n, 1), 0)
    col = jnp.full((n, 1), sink_ref[kh * GROUP + GROUP - 1], F32)
    for g in range(GROUP - 2, -1, -1):
        col = jnp.where(ri < (g + 1) * rows_per_head, sink_ref[kh * GROUP + g], col)
    return col


def _mix_tile(proj, mix, cs_ref, sn_ref, cw_ref, cb_ref, wa_ref, ba_ref, wx_ref, bx_ref, lam_ref, sink_ref,
              xpad, hc, kpad, vpad, a_s, u_s, klast, *, ts, pos0, mxu_fill):
    fill = iter(mxu_fill)

    def take(n):
        for _ in range(n):
            f = next(fill, None)
            if f is not None:
                f()

    w = LRU_WIDTH
    xpad[SUBLANES:SUBLANES + ts, :] = proj[:, 0:w]
    half = ts // 2
    for r0 in (0, half):
        take(1)
        xc = cb_ref[...]
        for jj in range(CONV_W):
            off = r0 + SUBLANES - (CONV_W - 1) + jj
            xc = xc + xpad[off:off + half, :] * cw_ref[jj:jj + 1, :]
        u_s[r0:r0 + half, :] = xc
    xpad[0:SUBLANES, :] = xpad[ts:ts + SUBLANES, :]

    _lru_gates(u_s, a_s, wa_ref, ba_ref, wx_ref, bx_ref, lam_ref, midway=lambda: take(1))

    row = lax.broadcasted_iota(jnp.int32, (SUBLANES, w), 0)
    carry = hc[...]
    ngroups = ts // SUBLANES
    for g in range(ngroups):
        if g % (ngroups // 8) == 0:
            take(1)
        rows = slice(g * SUBLANES, (g + 1) * SUBLANES)
        h = _scan_group(a_s[rows, :], u_s[rows, :], carry, row)
        a_s[rows, :] = h
        carry = jnp.broadcast_to(h[SUBLANES - 1:SUBLANES, :], (SUBLANES, w))
    hc[...] = carry
    for r0 in (0, half):
        take(1)
        rows = slice(r0, r0 + half)
        mix[rows, 0:w] = (a_s[rows, :] * jax.nn.gelu(proj[rows, w:2 * w])).astype(BF16)

    take(1)
    cos = cs_ref[...]
    sin = sn_ref[...]
    for kh in range(N_KV_HEADS):
        cols = slice(kh * HEAD_DIM, (kh + 1) * HEAD_DIM)
        kr = _rope(proj[:, K_COL + kh * HEAD_DIM:K_COL + (kh + 1) * HEAD_DIM], cos, sin)
        kpad[WINDOW:WINDOW + ts, cols] = kr.astype(BF16)
        klast[:, cols] = kr[ts - WINDOW:, :]
    vpad[WINDOW:WINDOW + ts, :] = proj[:, V_COL:V_COL + KV_WIDTH].astype(BF16)

    nq = GROUP * WINDOW
    r_i = lax.broadcasted_iota(jnp.int32, (nq, 2 * WINDOW), 0) & (WINDOW - 1)
    c_i = lax.broadcasted_iota(jnp.int32, (nq, 2 * WINDOW), 1)
    band = (c_i >= r_i) & (c_i <= r_i + WINDOW)
    for n in range(ts // WINDOW):
        qrows = slice(n * WINDOW, (n + 1) * WINDOW)
        kpos0 = pos0 + (n - 1) * WINDOW
        mask = band & (c_i + kpos0 >= 0)
        for kh in range(N_KV_HEADS):
            cols = slice(kh * HEAD_DIM, (kh + 1) * HEAD_DIM)
            qs = []
            for g in range(GROUP):
                hd = kh * GROUP + g
                qh = proj[qrows, Q_COL + hd * HEAD_DIM:Q_COL + (hd + 1) * HEAD_DIM]
                qs.append(_rope(qh, cos[qrows, :], sin[qrows, :]).astype(BF16))
            q4 = jnp.concatenate(qs, axis=0)
            kk = kpad[n * WINDOW:(n + 2) * WINDOW, cols]
            vv = vpad[n * WINDOW:(n + 2) * WINDOW, cols]
            s = lax.dot_general(q4, kk, (((1,), (1,)), ((), ())), preferred_element_type=F32) * SCALE
            take(1)
            o = _softmax_pv(s, mask, _sink_column(sink_ref, kh, WINDOW), vv)
            for g in range(GROUP):
                hd = kh * GROUP + g
                mix[qrows, w + hd * HEAD_DIM:w + (hd + 1) * HEAD_DIM] = (
                    o[g * WINDOW:(g + 1) * WINDOW, :].astype(BF16))

    take(len(mxu_fill))
    kpad[0:WINDOW, :] = kpad[ts:ts + WINDOW, :]
    vpad[0:WINDOW, :] = vpad[ts:ts + WINDOW, :]


def _mixer_body(xa_ref, xc_ref, sh_ref, sc_ref, gt_ref, gm_ref, win_ref, wout_ref, cs_ref, sn_ref,
                cw_ref, cb_ref, wa_ref, ba_ref, wx_ref, bx_ref, lam_ref, sink_ref,
                o_ref, lru_ref, k_ref, v_ref, conv_ref,
                proj0, proj1, mix0, mix1, h_s, xpad, hc, kpad, vpad, a_s, u_s, klast,
                *, ts, nt, ntiles, batch_shift):
    s = pl.program_id(0)
    tile_a = jnp.minimum(s, ntiles - 1)
    tile_b = jnp.clip(s - 1, 0, ntiles - 1)
    tile_c = jnp.clip(s - 2, 0, ntiles - 1)
    t_in = tile_b & (nt - 1)

    @pl.when(s == 0)
    def _():
        proj1[...] = jnp.zeros(proj1.shape, F32)
        mix0[...] = jnp.zeros(mix0.shape, BF16)

    @pl.when(t_in == 0)
    def _():
        xpad[0:SUBLANES, :] = jnp.zeros((SUBLANES, LRU_WIDTH), F32)
        hc[...] = jnp.zeros((SUBLANES, LRU_WIDTH), F32)
        kpad[0:WINDOW, :] = jnp.zeros((WINDOW, KV_WIDTH), BF16)
        vpad[0:WINDOW, :] = jnp.zeros((WINDOW, KV_WIDTH), BF16)

    def step(par):
        proj_w, proj_r = (proj0, proj1) if par == 0 else (proj1, proj0)
        mix_w, mix_r = (mix1, mix0) if par == 0 else (mix0, mix1)

        gt = _mod_row(gt_ref, tile_c * ts, 0, batch_shift)

        def out_chunk(k):
            cols = slice(k * MIXER_TN, (k + 1) * MIXER_TN)

            def run():
                d = jnp.dot(mix_r[...], wout_ref[:, cols], preferred_element_type=F32)
                o_ref[:, cols] = xc_ref[:, cols] + gt[:, cols] * d
            return run

        def in_chunk(k):
            cols = slice(k * MIXER_TN, (k + 1) * MIXER_TN)

            def run():
                if k == 0:
                    g = gm_ref[...]
                    sc = _mod_row(sc_ref, tile_a * ts, 0, batch_shift)
                    sh = _mod_row(sh_ref, tile_a * ts, 0, batch_shift)
                    for r in range(ts // FFN_ROWS):
                        rows = slice(r * FFN_ROWS, (r + 1) * FFN_ROWS)
                        h_s[rows, :] = _normmod(xa_ref[rows, :], g, sc, sh).astype(BF16)
                proj_w[:, cols] = jnp.dot(h_s[...], win_ref[:, cols], preferred_element_type=F32)
            return run

        fill = ([in_chunk(k) for k in range(IN_COLS // MIXER_TN)]
                + [out_chunk(k) for k in range(D_MODEL // MIXER_TN)])
        _mix_tile(proj_r, mix_w, cs_ref, sn_ref, cw_ref, cb_ref, wa_ref, ba_ref, wx_ref, bx_ref, lam_ref,
                  sink_ref, xpad, hc, kpad, vpad, a_s, u_s, klast, ts=ts, pos0=t_in * ts, mxu_fill=fill)

        @pl.when((s >= 1) & (s <= ntiles))
        def _():
            lru_ref[0] = hc[0:1, :]
            k_ref[0] = klast[...]
            v_ref[0] = proj_r[ts - WINDOW:, V_COL:V_COL + KV_WIDTH]
            conv_ref[0] = xpad[0:SUBLANES, :]

    @pl.when((s & 1) == 0)
    def _():
        step(0)

    @pl.when((s & 1) == 1)
    def _():
        step(1)


def _mixer_prompt(x, mod, g_mix, w_in_b, w_out_b, cos, sin, p, *, batch, seq):
    ts = MIX_TS
    nt = seq // ts
    ntiles = batch * nt
    assert nt & (nt - 1) == 0 and seq & (seq - 1) == 0
    w = LRU_WIDTH
    body = functools.partial(_mixer_body, ts=ts, nt=nt, ntiles=ntiles, batch_shift=seq.bit_length() - 1)

    def tile_a(s):
        return jnp.minimum(s, ntiles - 1)

    def tile_b(s):
        return jnp.clip(s - 1, 0, ntiles - 1)

    def tile_c(s):
        return jnp.clip(s - 2, 0, ntiles - 1)

    def whole(shape, **kw):
        return pl.BlockSpec(shape, lambda s: (0,) * len(shape), **kw)

    def mspec(c):
        return pl.BlockSpec((MOD_ROWS, D_MODEL), lambda s: (0, c))

    def state(shape):
        return pl.BlockSpec((1,) + shape, lambda s: (tile_b(s) // nt, 0, 0))

    once = dict(pipeline_mode=pl.Buffered(1))
    return pl.pallas_call(
        body,
        grid=(ntiles + 2,),
        in_specs=[
            pl.BlockSpec((ts, D_MODEL), lambda s: (tile_a(s), 0)),
            pl.BlockSpec((ts, D_MODEL), lambda s: (tile_c(s), 0)),
            mspec(3), mspec(4), mspec(5),
            whole((1, D_MODEL)),
            whole((D_MODEL, IN_COLS), **once),
            whole((D_MODEL, D_MODEL), **once),
            pl.BlockSpec((ts, HEAD_DIM), lambda s: (tile_b(s) % nt, 0)),
            pl.BlockSpec((ts, HEAD_DIM), lambda s: (tile_b(s) % nt, 0)),
            whole((CONV_W, w)), whole((1, w)),
            whole((LRU_HEADS, LRU_BLOCK, LRU_BLOCK)), whole((1, w)),
            whole((LRU_HEADS, LRU_BLOCK, LRU_BLOCK)), whole((1, w)),
            whole((1, w)),
            pl.BlockSpec(memory_space=pltpu.SMEM),
        ],
        out_specs=[
            pl.BlockSpec((ts, D_MODEL), lambda s: (tile_c(s), 0)),
            state((1, w)), state((WINDOW, KV_WIDTH)), state((WINDOW, KV_WIDTH)), state((SUBLANES, w)),
        ],
        out_shape=[
            jax.ShapeDtypeStruct((batch * seq, D_MODEL), F32),
            jax.ShapeDtypeStruct((batch, 1, w), F32),
            jax.ShapeDtypeStruct((batch, WINDOW, KV_WIDTH), F32),
            jax.ShapeDtypeStruct((batch, WINDOW, KV_WIDTH), F32),
            jax.ShapeDtypeStruct((batch, SUBLANES, w), F32),
        ],
        scratch_shapes=[
            pltpu.VMEM((ts, IN_COLS), F32), pltpu.VMEM((ts, IN_COLS), F32),
            pltpu.VMEM((ts, D_MODEL), BF16), pltpu.VMEM((ts, D_MODEL), BF16),
            pltpu.VMEM((ts, D_MODEL), BF16),
            pltpu.VMEM((SUBLANES + ts, w), F32),
            pltpu.VMEM((SUBLANES, w), F32),
            pltpu.VMEM((WINDOW + ts, KV_WIDTH), BF16),
            pltpu.VMEM((WINDOW + ts, KV_WIDTH), BF16),
            pltpu.VMEM((ts, w), F32),
            pltpu.VMEM((ts, w), F32),
            pltpu.VMEM((WINDOW, KV_WIDTH), F32),
        ],
        compiler_params=_params(("arbitrary",)),
        name="mixer_prompt",
    )(x, x, mod, mod, mod, g_mix.reshape(1, D_MODEL), w_in_b, w_out_b, cos, sin,
      p["conv_w"], p["conv_b"], p["wa"], p["ba"], p["wx"], p["bx"], p["lam"], p["sinks"])


def _mix_sample_body(proj_ref, prev_ref, h0_ref, ck_ref, cv_ref, cs_ref, sn_ref, cw_ref, cb_ref,
                     wa_ref, ba_ref, wx_ref, bx_ref, lam_ref, sink_ref,
                     mix_ref, lru_ref, knew_ref,
                     a_s, u_s, q_s, *, batch, seq):
    w = LRU_WIDTH
    row = lax.broadcasted_iota(jnp.int32, (SUBLANES, w), 0)

    def group(b):
        return pl.ds(pl.multiple_of(b * SUBLANES, SUBLANES), SUBLANES)

    def conv_body(b, carry):
        rows = group(b)
        cur = proj_ref[rows, 0:w]
        prev = prev_ref[rows, :]
        xc = cb_ref[...]
        for jj in range(CONV_W):
            d = CONV_W - 1 - jj
            if d == 0:
                term = cur
            else:
                term = jnp.where(row >= d, pltpu.roll(cur, d, 0), pltpu.roll(prev, d, 0))
            xc = xc + term * cw_ref[jj:jj + 1, :]
        u_s[rows, :] = xc
        return carry

    lax.fori_loop(0, batch, conv_body, 0)

    _lru_gates(u_s, a_s, wa_ref, ba_ref, wx_ref, bx_ref, lam_ref)

    def scan_body(b, carry):
        rows = group(b)
        h0 = jnp.broadcast_to(h0_ref[pl.ds(b, 1), :], (SUBLANES, w))
        h = _scan_group(a_s[rows, :], u_s[rows, :], h0, row)
        a_s[rows, :] = h
        lru_ref[pl.ds(b, 1), :] = h[SUBLANES - 1:SUBLANES, :]
        return carry

    lax.fori_loop(0, batch, scan_body, 0)
    mix_ref[:, 0:w] = a_s[...] * jax.nn.gelu(proj_ref[:, w:2 * w])

    cos = cs_ref[...]
    sin = sn_ref[...]
    for hd in range(N_HEADS):
        cols = slice(hd * HEAD_DIM, (hd + 1) * HEAD_DIM)
        q_s[:, cols] = _rope(proj_ref[:, Q_COL + hd * HEAD_DIM:Q_COL + (hd + 1) * HEAD_DIM], cos, sin)
    for kh in range(N_KV_HEADS):
        cols = slice(kh * HEAD_DIM, (kh + 1) * HEAD_DIM)
        knew_ref[:, cols] = _rope(proj_ref[:, K_COL + kh * HEAD_DIM:K_COL + (kh + 1) * HEAD_DIM], cos, sin)

    nq = GROUP * seq
    nk = 2 * WINDOW
    r_i = lax.broadcasted_iota(jnp.int32, (nq, nk), 0) & (seq - 1)
    c_i = lax.broadcasted_iota(jnp.int32, (nq, nk), 1)
    mask = (c_i >= r_i) & (c_i <= r_i + WINDOW)
    pad = jnp.zeros((nk - WINDOW - seq, HEAD_DIM), F32)

    def attn_body(b, carry):
        rows = group(b)
        for kh in range(N_KV_HEADS):
            cols = slice(kh * HEAD_DIM, (kh + 1) * HEAD_DIM)
            q4 = jnp.concatenate(
                [q_s[rows, (kh * GROUP + g) * HEAD_DIM:(kh * GROUP + g + 1) * HEAD_DIM] for g in range(GROUP)],
                axis=0).astype(BF16)
            kk = jnp.concatenate([ck_ref[b, :, cols], knew_ref[rows, cols], pad], axis=0).astype(BF16)
            vv = jnp.concatenate(
                [cv_ref[b, :, cols], proj_ref[rows, V_COL + kh * HEAD_DIM:V_COL + (kh + 1) * HEAD_DIM], pad],
                axis=0).astype(BF16)
            s = lax.dot_general(q4, kk, (((1,), (1,)), ((), ())), preferred_element_type=F32) * SCALE
            o = _softmax_pv(s, mask, _sink_column(sink_ref, kh, seq), vv)
            for g in range(GROUP):
                hd = kh * GROUP + g
                mix_ref[rows, w + hd * HEAD_DIM:w + (hd + 1) * HEAD_DIM] = o[g * seq:(g + 1) * seq, :]
        return carry

    lax.fori_loop(0, batch, attn_body, 0)


def _mix_sample(proj, prev, h0, cache_k, cache_v, cos, sin, p, *, batch, seq):
    assert seq == SUBLANES, "each sample batch must be exactly one sublane group"
    t = batch * seq
    body = functools.partial(_mix_sample_body, batch=batch, seq=seq)
    vmem = pl.BlockSpec(memory_space=pltpu.VMEM)
    return pl.pallas_call(
        body,
        in_specs=[vmem] * 14 + [pl.BlockSpec(memory_space=pltpu.SMEM)],
        out_specs=[vmem, vmem, vmem],
        out_shape=[
            jax.ShapeDtypeStruct((t, D_MODEL), F32),
            jax.ShapeDtypeStruct((batch, LRU_WIDTH), F32),
            jax.ShapeDtypeStruct((t, KV_WIDTH), F32),
        ],
        scratch_shapes=[
            pltpu.VMEM((t, LRU_WIDTH), F32),
            pltpu.VMEM((t, LRU_WIDTH), F32),
            pltpu.VMEM((t, ATTN_WIDTH), F32),
        ],
        compiler_params=pltpu.CompilerParams(vmem_limit_bytes=V7X_VMEM_LIMIT_BYTES),
        name="mix_sample",
    )(proj, prev, h0, cache_k, cache_v, cos, sin, p["conv_w"], p["conv_b"], p["wa"], p["ba"], p["wx"], p["bx"],
      p["lam"], p["sinks"])


def _layer(w, l):
    return w.reshape(w.shape[1:]) if w.shape[0] == 1 else w[l]


def _rope_tables(pos):
    half = HEAD_DIM // 2
    inv = ROPE_THETA ** (-jnp.arange(half, dtype=F32) / half)
    ang = pos.astype(F32)[:, None] * inv[None, :]
    cos = jnp.cos(ang)
    sin = jnp.sin(ang)
    return jnp.concatenate([cos, cos], axis=-1), jnp.concatenate([-sin, sin], axis=-1)


def kernel(x_prompt, x_sample, c_prompt, c_sample, state_conv, state_lru, cache_k_win, cache_v_win, ada_w, ada_b, norm_ffn1, norm_mix, norm_ffn2, ffn1_w_gate, ffn1_w_up, ffn1_w_down, w_in, conv_w, conv_b, lru_w_a, lru_b_a, lru_w_x, lru_b_x, lru_lambda, attn_sinks, w_out, ffn2_w_gate, ffn2_w_up, ffn2_w_down, norm_final):
    bp, sp, _ = x_prompt.shape
    bs, ss, _ = x_sample.shape
    depth = ada_w.shape[0]
    tp, tsn = bp * sp, bs * ss
    assert bp <= MOD_PROMPT_ROWS and bp + bs <= MOD_ROWS
    tm_p = 1024
    assert sp % tm_p == 0

    xp = x_prompt.reshape(tp, D_MODEL)
    xs = x_sample.reshape(tsn, D_MODEL)
    c_all = jnp.concatenate(
        [c_prompt, jnp.zeros((MOD_PROMPT_ROWS - bp, D_MODEL), F32), c_sample,
         jnp.zeros((MOD_ROWS - MOD_PROMPT_ROWS - bs, D_MODEL), F32)], axis=0)
    cos_p, sin_p = _rope_tables(jnp.arange(sp))
    cos_s, sin_s = _rope_tables(PAST_LEN + jnp.arange(ss))
    cos_s = jnp.tile(cos_s, (bs, 1))
    sin_s = jnp.tile(sin_s, (bs, 1))

    outs_p, outs_s = [], []
    for l in range(depth):
        last = l == depth - 1
        L = functools.partial(_layer, l=l)
        cache_k, cache_v = L(cache_k_win), L(cache_v_win)
        mod = _adaln(c_all, L(ada_w), L(ada_b))
        p = dict(conv_w=L(conv_w), conv_b=L(conv_b).reshape(1, LRU_WIDTH), wa=L(lru_w_a),
                 ba=L(lru_b_a).reshape(1, LRU_WIDTH), wx=L(lru_w_x), bx=L(lru_b_x).reshape(1, LRU_WIDTH),
                 lam=L(lru_lambda).reshape(1, LRU_WIDTH), sinks=L(attn_sinks))
        ffn1 = (L(norm_ffn1), L(ffn1_w_gate), L(ffn1_w_up), L(ffn1_w_down), norm_final)
        ffn2 = (L(norm_ffn2), L(ffn2_w_gate), L(ffn2_w_up), L(ffn2_w_down), norm_final)
        ffn_p = dict(tm=tm_p, nr=FFN_ROWS, row0=0, rows_per_batch=sp)
        ffn_s = dict(tm=tsn, nr=ss, row0=MOD_PROMPT_ROWS, rows_per_batch=ss)
        grp_s = dict(emit=True, **ffn_s)

        def ffn(xp, xs, chunks, weights, final_norm, tag):
            g, wg, wu, wd, gf = weights
            head, wg_b, wu_b, wd_b = _ffn(xp, mod, chunks, g, wg, wu, wd, gf, emit=True, ntiles=1,
                                          final_norm=final_norm, name=tag + "_head", **ffn_p)
            xp, = _ffn(xp, mod, chunks, g, wg_b, wu_b, wd_b, gf, emit=False, tile0=1, ntiles=tp // tm_p - 1,
                       into=head, final_norm=final_norm, name=tag + "_prompt", **ffn_p)
            xs, = _ffn(xs, mod, chunks, g, wg_b, wu_b, wd_b, gf, emit=False, final_norm=final_norm,
                       name=tag + "_sample", **ffn_s)
            return xp, xs

        xp, xs = ffn(xp, xs, (0, 1, 2), ffn1, False, "ffn1")

        proj_s, w_in_b = _inproj(xs, mod, (3, 4), L(norm_mix), L(w_in), tn=PROJ_TN, name="inproj_sample",
                                 **grp_s)
        prev_s = jnp.pad(L(state_conv), ((0, 0), (SUBLANES - (CONV_W - 1), 0), (0, 0))).reshape(tsn, LRU_WIDTH)
        mix_s, lru_s, knew_s = _mix_sample(
            proj_s, prev_s, L(state_lru), cache_k.reshape(bs, WINDOW, KV_WIDTH),
            cache_v.reshape(bs, WINDOW, KV_WIDTH), cos_s, sin_s, p, batch=bs, seq=ss)
        xs, w_out_b = _outproj(mix_s, xs, mod, 5, L(w_out), tn=PROJ_TN, name="outproj_sample", **grp_s)
        xp, lru_p, k_p, v_p, conv_p = _mixer_prompt(xp, mod, L(norm_mix), w_in_b, w_out_b, cos_p, sin_p, p,
                                                    batch=bp, seq=sp)

        xp, xs = ffn(xp, xs, (6, 7, 8), ffn2, last, "ffn2")

        ps = proj_s.reshape(bs, ss, IN_COLS)
        outs_p.append((
            conv_p[:, SUBLANES - (CONV_W - 1):, :],
            lru_p.reshape(bp, LRU_WIDTH),
            k_p.reshape(bp, WINDOW, N_KV_HEADS, HEAD_DIM),
            v_p.reshape(bp, WINDOW, N_KV_HEADS, HEAD_DIM),
        ))
        k_all = jnp.concatenate([cache_k, knew_s.reshape(bs, ss, N_KV_HEADS, HEAD_DIM)], axis=1)
        v_all = jnp.concatenate([cache_v, ps[:, :, V_COL:].reshape(bs, ss, N_KV_HEADS, HEAD_DIM)], axis=1)
        outs_s.append((
            ps[:, ss - (CONV_W - 1):, 0:LRU_WIDTH],
            lru_s,
            k_all[:, -WINDOW:],
            v_all[:, -WINDOW:],
        ))

    y_prompt = xp.reshape(bp, sp, D_MODEL)
    y_sample = xs.reshape(bs, ss, D_MODEL)
    st_p = [jnp.stack([o[i] for o in outs_p]) for i in range(4)]
    st_s = [jnp.stack([o[i] for o in outs_s]) for i in range(4)]
    return (y_prompt, y_sample, st_p[0], st_p[1], st_p[2], st_p[3], st_s[0], st_s[1], st_s[2], st_s[3])
```

```python
import functools

import jax
import jax.numpy as jnp
from jax import lax
from jax.experimental import pallas as pl
from jax.experimental.pallas import tpu as pltpu

F32 = jnp.float32
BF16 = jnp.bfloat16

D_MODEL = 2048
D_FF = 5504
LRU_WIDTH = 1024
LRU_HEADS = 8
LRU_BLOCK = 128
CONV_W = 4
LRU_C = 8.0
HEAD_DIM = 128
N_HEADS = 8
N_KV_HEADS = 2
GROUP = N_HEADS // N_KV_HEADS
ATTN_WIDTH = N_HEADS * HEAD_DIM
KV_WIDTH = N_KV_HEADS * HEAD_DIM
WINDOW = 128
ROPE_THETA = 10000.0
N_MOD = 9
EPS = 1e-6
NEG = -1e30
PAST_LEN = 16384
IN_COLS = 2 * LRU_WIDTH + ATTN_WIDTH + 2 * KV_WIDTH
Q_COL = 2 * LRU_WIDTH
K_COL = Q_COL + ATTN_WIDTH
V_COL = K_COL + KV_WIDTH
SCALE = HEAD_DIM ** -0.5

SUBLANES = 8
LANES = 128
V7X_VMEM_LIMIT_BYTES = 56 * 1024 * 1024

MOD_ROWS = 48
MOD_PROMPT_ROWS = 8
ADALN_TN = 1024
BF16_ROWS = 16
FFN_TF = 512
FFN_EMIT_TF = 256
FFN_ROWS = 16
FFN_EPILOGUE_ROWS = 128
FFN_ROW_BLOCK = 256
FFN_FINISH_ROWS = 32
PROJ_TN = 512
OUT_TM = 512
MIX_TS = 256
MIXER_TN = 256


def _params(sem, vmem=V7X_VMEM_LIMIT_BYTES):
    return pltpu.CompilerParams(dimension_semantics=sem, vmem_limit_bytes=vmem)


def _sigmoid(x):
    return 1.0 / (1.0 + jnp.exp(-x))


def _normmod(x, g, sc, sh):
    ms = jnp.mean(x * x, axis=-1, keepdims=True)
    xn = x * lax.rsqrt(ms + EPS) * g
    return xn * (1.0 + sc) + sh


def _mod_row(ref, token, row0, batch_shift):
    return ref[pl.ds(row0 + lax.shift_right_logical(token, batch_shift), 1), :]


def _mod_spec(chunk):
    return pl.BlockSpec((MOD_ROWS, D_MODEL), lambda i, j: (0, chunk))


def _adaln_body(c_ref, w_ref, b_ref, o_ref):
    c = c_ref[...]
    s = (c * _sigmoid(c)).astype(BF16)
    o_ref[...] = jnp.dot(s, w_ref[...].astype(BF16), preferred_element_type=F32) + b_ref[...]


def _adaln(c_all, ada_w, ada_b):
    n = ada_w.shape[1]
    return pl.pallas_call(
        _adaln_body,
        grid=(n // ADALN_TN,),
        in_specs=[
            pl.BlockSpec((MOD_ROWS, D_MODEL), lambda j: (0, 0)),
            pl.BlockSpec((D_MODEL, ADALN_TN), lambda j: (0, j)),
            pl.BlockSpec((1, ADALN_TN), lambda j: (0, j)),
        ],
        out_specs=pl.BlockSpec((MOD_ROWS, ADALN_TN), lambda j: (0, j)),
        out_shape=jax.ShapeDtypeStruct((MOD_ROWS, n), F32),
        compiler_params=_params(("arbitrary",)),
        name="adaln",
    )(c_all, ada_w, ada_b.reshape(1, n))


def _ffn_body(x_ref, sh_ref, sc_ref, gt_ref, g_ref, wg_ref, wu_ref, wd_ref, gf_ref, *rest,
              nr, er, rb, row0, batch_shift, tile0, final_norm, emit):
    if emit:
        o_ref, wgb_ref, wub_ref, wdb_ref, h_ref = rest
    else:
        o_ref, h_ref = rest
    i = pl.program_id(0) + tile0
    j = pl.program_id(1)
    nj = pl.num_programs(1)
    tm = x_ref.shape[0]
    tf = wd_ref.shape[0]

    if emit:
        valid = D_FF - j * tf
        cmask = lax.broadcasted_iota(jnp.int32, (1, tf), 1) < valid
        rmask = lax.broadcasted_iota(jnp.int32, (tf, 1), 0) < valid
        wgb_ref[...] = jnp.where(cmask, wg_ref[...], 0.0).astype(BF16)
        wub_ref[...] = jnp.where(cmask, wu_ref[...], 0.0).astype(BF16)
        wdb_ref[...] = jnp.where(rmask, wd_ref[...], 0.0).astype(BF16)
        wg_ref, wu_ref, wd_ref = wgb_ref, wub_ref, wdb_ref

    def swiglu(rows):
        h = h_ref[rows, :].astype(BF16)
        g = jnp.dot(h, wg_ref[...], preferred_element_type=F32)
        u = jnp.dot(h, wu_ref[...], preferred_element_type=F32)
        a = (g * _sigmoid(g) * u).astype(BF16)
        return jnp.dot(a, wd_ref[...], preferred_element_type=F32)

    def prologue(rows, token):
        h = _normmod(x_ref[rows, :], g_ref[...], _mod_row(sc_ref, token, row0, batch_shift),
                     _mod_row(sh_ref, token, row0, batch_shift))
        h_ref[rows, :] = h.astype(h_ref.dtype)

    def epilogue(rows, token, acc):
        y = x_ref[rows, :] + (0.5 * _mod_row(gt_ref, token, row0, batch_shift)) * acc
        if final_norm:
            ms = jnp.mean(y * y, axis=-1, keepdims=True)
            y = y * lax.rsqrt(ms + EPS) * gf_ref[...]
        o_ref[rows, :] = y

    if rb:
        blocks = range(0, tm, rb)

        def block_prologue(r0):
            for c in range(r0, r0 + rb, nr):
                prologue(slice(c, c + nr), i * tm + c)

        @pl.when(j == 0)
        def _():
            block_prologue(0)
            for r0 in blocks:
                o_ref[r0:r0 + rb, :] = swiglu(slice(r0, r0 + rb))
                if r0 + rb < tm:
                    block_prologue(r0 + rb)

        @pl.when((j > 0) & (j < nj - 1))
        def _():
            o_ref[...] += swiglu(slice(None))

        @pl.when(j == nj - 1)
        def _():
            for r0 in blocks:
                acc = o_ref[r0:r0 + rb, :] + swiglu(slice(r0, r0 + rb))
                for c in range(0, rb, FFN_FINISH_ROWS):
                    epilogue(slice(r0 + c, r0 + c + FFN_FINISH_ROWS), i * tm + r0 + c,
                             acc[c:c + FFN_FINISH_ROWS, :])
    else:
        @pl.when(j == 0)
        def _():
            def body(r, carry):
                rows = pl.ds(pl.multiple_of(r * nr, nr), nr)
                prologue(rows, i * tm + r * nr)
                o_ref[rows, :] = jnp.zeros((nr, D_MODEL), F32)
                return carry

            lax.fori_loop(0, tm // nr, body, 0, unroll=4)

        o_ref[...] += swiglu(slice(None))

        @pl.when(j == nj - 1)
        def _():
            def body(r, carry):
                rows = pl.ds(pl.multiple_of(r * er, er), er)
                epilogue(rows, i * tm + r * er, o_ref[rows, :])
                return carry

            lax.fori_loop(0, tm // er, body, 0)


def _ffn(x, mod, chunks, g, wg, wu, wd, gf, *, tm, nr, row0, rows_per_batch, final_norm, emit, name,
         tf=FFN_TF, tile0=0, ntiles=None, inplace=False):
    t = x.shape[0]
    ntiles = t // tm if ntiles is None else ntiles
    assert tf == FFN_TF or emit
    sub = FFN_TF // tf
    nj_store = pl.cdiv(D_FF, FFN_TF)
    nj = nj_store * sub
    er = min(FFN_EPILOGUE_ROWS, rows_per_batch)
    assert rows_per_batch & (rows_per_batch - 1) == 0 and rows_per_batch % nr == 0 and tm % nr == 0
    assert rows_per_batch % er == 0 and tm % er == 0
    rb = FFN_ROW_BLOCK if rows_per_batch % FFN_ROW_BLOCK == 0 and tm % FFN_ROW_BLOCK == 0 else 0
    body = functools.partial(_ffn_body, nr=nr, er=er, rb=rb, row0=row0,
                             batch_shift=rows_per_batch.bit_length() - 1, tile0=tile0,
                             final_norm=final_norm, emit=emit)

    row = pl.BlockSpec((1, D_MODEL), lambda i, j: (0, 0))
    tile_major = pl.BlockSpec((None, D_MODEL, tf), lambda i, j: (j // sub, 0, j % sub))
    if emit:
        assert ntiles == 1, "bf16 weight tiles are written once, by a single token tile"
        wspecs = [pl.BlockSpec((D_MODEL, tf), lambda i, j: (0, j)),
                  pl.BlockSpec((D_MODEL, tf), lambda i, j: (0, j)),
                  pl.BlockSpec((tf, D_MODEL), lambda i, j: (j, 0))]
    else:
        wspecs = [tile_major, tile_major, pl.BlockSpec((tf, D_MODEL), lambda i, j: (j, 0))]
    xmode = dict(pipeline_mode=pl.Buffered(1)) if ntiles == 1 else {}
    in_specs = [pl.BlockSpec((tm, D_MODEL), lambda i, j: (i + tile0, 0), **xmode),
                _mod_spec(chunks[0]), _mod_spec(chunks[1]), _mod_spec(chunks[2]), row] + wspecs + [row]
    args = [x, mod, mod, mod, g.reshape(1, D_MODEL), wg, wu, wd, gf.reshape(1, D_MODEL)]
    out_specs = [pl.BlockSpec((tm, D_MODEL), lambda i, j: (i + tile0, 0))]
    out_shape = [jax.ShapeDtypeStruct((t, D_MODEL), F32)]
    if emit:
        out_specs += [tile_major, tile_major, pl.BlockSpec((tf, D_MODEL), lambda i, j: (j, 0))]
        out_shape += [jax.ShapeDtypeStruct((nj_store, D_MODEL, FFN_TF), BF16),
                      jax.ShapeDtypeStruct((nj_store, D_MODEL, FFN_TF), BF16),
                      jax.ShapeDtypeStruct((nj_store * FFN_TF, D_MODEL), BF16)]
    return pl.pallas_call(
        body,
        grid=(ntiles, nj),
        in_specs=in_specs,
        out_specs=out_specs,
        out_shape=out_shape,
        input_output_aliases={0: 0} if inplace else {},
        scratch_shapes=[pltpu.VMEM((tm, D_MODEL), BF16 if nr % BF16_ROWS == 0 else F32)],
        compiler_params=_params(("parallel", "arbitrary")),
        name=name,
    )(*args)


def _inproj_body(x_ref, sh_ref, sc_ref, g_ref, w_ref, *rest, nr, row0, batch_shift, emit):
    if emit:
        o_ref, wb_ref, h_ref = rest
    else:
        o_ref, h_ref = rest
    i = pl.program_id(0)
    j = pl.program_id(1)
    tm = x_ref.shape[0]

    @pl.when(j == 0)
    def _():
        g = g_ref[...]

        def body(r, carry):
            rows = pl.ds(pl.multiple_of(r * nr, nr), nr)
            tok = i * tm + r * nr
            h = _normmod(x_ref[rows, :], g, _mod_row(sc_ref, tok, row0, batch_shift),
                         _mod_row(sh_ref, tok, row0, batch_shift))
            h_ref[rows, :] = h.astype(h_ref.dtype)
            return carry

        lax.fori_loop(0, tm // nr, body, 0, unroll=4)

    w = w_ref[...].astype(BF16)
    if emit:
        wb_ref[...] = w
    o_ref[...] = jnp.dot(h_ref[...].astype(BF16), w, preferred_element_type=F32)


def _inproj(x, mod, chunks, g, w_in, *, tm, tn, nr, row0, rows_per_batch, emit, name):
    t = x.shape[0]
    assert rows_per_batch & (rows_per_batch - 1) == 0 and rows_per_batch % nr == 0 and tm % nr == 0
    body = functools.partial(_inproj_body, nr=nr, row0=row0, batch_shift=rows_per_batch.bit_length() - 1,
                             emit=emit)
    wspec = pl.BlockSpec((D_MODEL, tn), lambda i, j: (0, j))
    out_specs = [pl.BlockSpec((tm, tn), lambda i, j: (i, j))]
    out_shape = [jax.ShapeDtypeStruct((t, IN_COLS), F32)]
    if emit:
        assert t == tm, "the bf16 weight copy is written once, by a single token tile"
        out_specs.append(wspec)
        out_shape.append(jax.ShapeDtypeStruct((D_MODEL, IN_COLS), BF16))
    return pl.pallas_call(
        body,
        grid=(t // tm, IN_COLS // tn),
        in_specs=[
            pl.BlockSpec((tm, D_MODEL), lambda i, j: (i, 0)),
            _mod_spec(chunks[0]), _mod_spec(chunks[1]),
            pl.BlockSpec((1, D_MODEL), lambda i, j: (0, 0)),
            wspec,
        ],
        out_specs=out_specs,
        out_shape=out_shape,
        scratch_shapes=[pltpu.VMEM((tm, D_MODEL), F32 if emit else BF16)],
        compiler_params=_params(("parallel", "arbitrary")),
        name=name,
    )(x, mod, mod, g.reshape(1, D_MODEL), w_in)


def _outproj_body(m_ref, x_ref, gt_ref, w_ref, *rest, nr, row0, batch_shift, emit):
    if emit:
        o_ref, wb_ref = rest
    else:
        o_ref, = rest
    i = pl.program_id(0)
    tm = x_ref.shape[0]
    w = w_ref[...].astype(BF16)
    if emit:
        wb_ref[...] = w
    d = jnp.dot(m_ref[...].astype(BF16), w, preferred_element_type=F32)
    if nr == tm:
        o_ref[...] = x_ref[...] + _mod_row(gt_ref, i * tm, row0, batch_shift) * d
    else:
        o_ref[...] = d

        def body(r, carry):
            rows = pl.ds(pl.multiple_of(r * nr, nr), nr)
            gt = _mod_row(gt_ref, i * tm + r * nr, row0, batch_shift)
            o_ref[rows, :] = x_ref[rows, :] + gt * o_ref[rows, :]
            return carry

        lax.fori_loop(0, tm // nr, body, 0)


def _outproj(mix, x, mod, gt_c, w_out, *, tm, tn, nr, row0, rows_per_batch, emit, name):
    t = x.shape[0]
    nn = D_MODEL // tn
    assert rows_per_batch & (rows_per_batch - 1) == 0 and rows_per_batch % nr == 0 and tm % nr == 0
    body = functools.partial(_outproj_body, nr=nr, row0=row0, batch_shift=rows_per_batch.bit_length() - 1,
                             emit=emit)
    wspec = pl.BlockSpec((D_MODEL, tn), lambda i, j: (0, j))
    out_specs = [pl.BlockSpec((tm, tn), lambda i, j: (i, j))]
    out_shape = [jax.ShapeDtypeStruct((t, D_MODEL), F32)]
    if emit:
        assert t == tm, "the bf16 weight copy is written once, by a single token tile"
        out_specs.append(wspec)
        out_shape.append(jax.ShapeDtypeStruct((D_MODEL, D_MODEL), BF16))
    return pl.pallas_call(
        body,
        grid=(t // tm, nn),
        in_specs=[
            pl.BlockSpec((tm, D_MODEL), lambda i, j: (i, 0)),
            pl.BlockSpec((tm, tn), lambda i, j: (i, j)),
            pl.BlockSpec((MOD_ROWS, tn), lambda i, j: (0, gt_c * nn + j)),
            wspec,
        ],
        out_specs=out_specs,
        out_shape=out_shape,
        compiler_params=_params(("parallel", "arbitrary")),
        name=name,
    )(mix, x, mod, w_out)


def _rope(x, cos, sin_signed):
    return x * cos + pltpu.roll(x, HEAD_DIM // 2, 1) * sin_signed


def _lru_gates(xc_ref, a_ref, wa_ref, ba_ref, wx_ref, bx_ref, lam_ref, midway=None):
    nlam = -lam_ref[...]
    softplus = jnp.maximum(nlam, 0.0) + jnp.log1p(jnp.exp(-jnp.abs(nlam)))
    rate = -LRU_C * softplus
    for hh in range(LRU_HEADS):
        if midway is not None:
            midway()
        cols = slice(hh * LRU_BLOCK, (hh + 1) * LRU_BLOCK)
        xc = xc_ref[:, cols]
        xcb = xc.astype(BF16)
        ra = jnp.dot(xcb, wa_ref[hh].astype(BF16), preferred_element_type=F32) + ba_ref[:, cols]
        rx = jnp.dot(xcb, wx_ref[hh].astype(BF16), preferred_element_type=F32) + bx_ref[:, cols]
        r = _sigmoid(ra)
        gi = _sigmoid(rx)
        a = jnp.exp(r * rate[:, cols])
        a_ref[:, cols] = a
        xc_ref[:, cols] = jnp.sqrt(1.0 - a * a) * (gi * xc)


def _scan_group(a, u, carry, row):
    for s in (1, 2, 4):
        a_sh = pltpu.roll(a, s, 0)
        u_sh = pltpu.roll(u, s, 0)
        m = row >= s
        u = jnp.where(m, a * u_sh + u, u)
        a = jnp.where(m, a * a_sh, a)
    return a * carry + u


def _softmax_pv(s, mask, sink, v):
    s = jnp.where(mask, s, NEG)
    m = jnp.maximum(jnp.max(s, axis=-1, keepdims=True), sink)
    p = jnp.exp(s - m)
    den = jnp.sum(p, axis=-1, keepdims=True) + jnp.exp(sink - m)
    return jnp.dot(p.astype(BF16), v, preferred_element_type=F32) / den


def _sink_column(sink_ref, kh, rows_per_head):
    n = GROUP * rows_per_head
    ri = lax.broadcasted_iota(jnp.int32, (n, 1), 0)
    col = jnp.full((n, 1), sink_ref[kh * GROUP + GROUP - 1], F32)
    for g in range(GROUP - 2, -1, -1):
        col = jnp.where(ri < (g + 1) * rows_per_head, sink_ref[kh * GROUP + g], col)
    return col


def _mix_tile(proj, mix, cs_ref, sn_ref, cw_ref, cb_ref, wa_ref, ba_ref, wx_ref, bx_ref, lam_ref, sink_ref,
              xpad, hc, kpad, vpad, a_s, u_s, klast, *, ts, pos0, mxu_fill):
    fill = iter(mxu_fill)

    def take(n):
        for _ in range(n):
            f = next(fill, None)
            if f is not None:
                f()

    w = LRU_WIDTH
    xpad[SUBLANES:SUBLANES + ts, :] = proj[:, 0:w]
    half = ts // 2
    for r0 in (0, half):
        take(1)
        xc = cb_ref[...]
        for jj in range(CONV_W):
            off = r0 + SUBLANES - (CONV_W - 1) + jj
            xc = xc + xpad[off:off + half, :] * cw_ref[jj:jj + 1, :]
        u_s[r0:r0 + half, :] = xc
    xpad[0:SUBLANES, :] = xpad[ts:ts + SUBLANES, :]

    _lru_gates(u_s, a_s, wa_ref, ba_ref, wx_ref, bx_ref, lam_ref, midway=lambda: take(1))

    row = lax.broadcasted_iota(jnp.int32, (SUBLANES, w), 0)
    carry = hc[...]
    ngroups = ts // SUBLANES
    for g in range(ngroups):
        if g % (ngroups // 8) == 0:
            take(1)
        rows = slice(g * SUBLANES, (g + 1) * SUBLANES)
        h = _scan_group(a_s[rows, :], u_s[rows, :], carry, row)
        a_s[rows, :] = h
        carry = jnp.broadcast_to(h[SUBLANES - 1:SUBLANES, :], (SUBLANES, w))
    hc[...] = carry
    for r0 in (0, half):
        take(1)
        rows = slice(r0, r0 + half)
        mix[rows, 0:w] = (a_s[rows, :] * jax.nn.gelu(proj[rows, w:2 * w])).astype(BF16)

    take(1)
    cos = cs_ref[...]
    sin = sn_ref[...]
    for kh in range(N_KV_HEADS):
        cols = slice(kh * HEAD_DIM, (kh + 1) * HEAD_DIM)
        kr = _rope(proj[:, K_COL + kh * HEAD_DIM:K_COL + (kh + 1) * HEAD_DIM], cos, sin)
        kpad[WINDOW:WINDOW + ts, cols] = kr.astype(BF16)
        klast[:, cols] = kr[ts - WINDOW:, :]
    vpad[WINDOW:WINDOW + ts, :] = proj[:, V_COL:V_COL + KV_WIDTH].astype(BF16)

    nq = GROUP * WINDOW
    r_i = lax.broadcasted_iota(jnp.int32, (nq, 2 * WINDOW), 0) & (WINDOW - 1)
    c_i = lax.broadcasted_iota(jnp.int32, (nq, 2 * WINDOW), 1)
    band = (c_i >= r_i) & (c_i <= r_i + WINDOW)
    for n in range(ts // WINDOW):
        qrows = slice(n * WINDOW, (n + 1) * WINDOW)
        kpos0 = pos0 + (n - 1) * WINDOW
        mask = band & (c_i + kpos0 >= 0)
        for kh in range(N_KV_HEADS):
            cols = slice(kh * HEAD_DIM, (kh + 1) * HEAD_DIM)
            qs = []
            for g in range(GROUP):
                hd = kh * GROUP + g
                qh = proj[qrows, Q_COL + hd * HEAD_DIM:Q_COL + (hd + 1) * HEAD_DIM]
                qs.append(_rope(qh, cos[qrows, :], sin[qrows, :]).astype(BF16))
            q4 = jnp.concatenate(qs, axis=0)
            kk = kpad[n * WINDOW:(n + 2) * WINDOW, cols]
            vv = vpad[n * WINDOW:(n + 2) * WINDOW, cols]
            s = lax.dot_general(q4, kk, (((1,), (1,)), ((), ())), preferred_element_type=F32) * SCALE
            take(1)
            o = _softmax_pv(s, mask, _sink_column(sink_ref, kh, WINDOW), vv)
            for g in range(GROUP):
                hd = kh * GROUP + g
                mix[qrows, w + hd * HEAD_DIM:w + (hd + 1) * HEAD_DIM] = (
                    o[g * WINDOW:(g + 1) * WINDOW, :].astype(BF16))

    take(len(mxu_fill))
    kpad[0:WINDOW, :] = kpad[ts:ts + WINDOW, :]
    vpad[0:WINDOW, :] = vpad[ts:ts + WINDOW, :]


def _mixer_body(xa_ref, xc_ref, sh_ref, sc_ref, gt_ref, gm_ref, win_ref, wout_ref, cs_ref, sn_ref,
                cw_ref, cb_ref, wa_ref, ba_ref, wx_ref, bx_ref, lam_ref, sink_ref,
                o_ref, lru_ref, k_ref, v_ref, conv_ref,
                proj0, proj1, mix0, mix1, h_s, xpad, hc, kpad, vpad, a_s, u_s, klast,
                *, ts, nt, ntiles, batch_shift):
    s = pl.program_id(0)
    tile_a = jnp.minimum(s, ntiles - 1)
    tile_b = jnp.clip(s - 1, 0, ntiles - 1)
    tile_c = jnp.clip(s - 2, 0, ntiles - 1)
    t_in = tile_b & (nt - 1)

    @pl.when(s == 0)
    def _():
        proj1[...] = jnp.zeros(proj1.shape, F32)
        mix0[...] = jnp.zeros(mix0.shape, BF16)

    @pl.when(t_in == 0)
    def _():
        xpad[0:SUBLANES, :] = jnp.zeros((SUBLANES, LRU_WIDTH), F32)
        hc[...] = jnp.zeros((SUBLANES, LRU_WIDTH), F32)
        kpad[0:WINDOW, :] = jnp.zeros((WINDOW, KV_WIDTH), BF16)
        vpad[0:WINDOW, :] = jnp.zeros((WINDOW, KV_WIDTH), BF16)

    def step(par):
        proj_w, proj_r = (proj0, proj1) if par == 0 else (proj1, proj0)
        mix_w, mix_r = (mix1, mix0) if par == 0 else (mix0, mix1)

        gt = _mod_row(gt_ref, tile_c * ts, 0, batch_shift)

        def out_chunk(k):
            cols = slice(k * MIXER_TN, (k + 1) * MIXER_TN)

            def run():
                d = jnp.dot(mix_r[...], wout_ref[:, cols], preferred_element_type=F32)
                o_ref[:, cols] = xc_ref[:, cols] + gt[:, cols] * d
            return run

        def in_chunk(k):
            cols = slice(k * MIXER_TN, (k + 1) * MIXER_TN)

            def run():
                if k == 0:
                    g = gm_ref[...]
                    sc = _mod_row(sc_ref, tile_a * ts, 0, batch_shift)
                    sh = _mod_row(sh_ref, tile_a * ts, 0, batch_shift)
                    for r in range(ts // FFN_ROWS):
                        rows = slice(r * FFN_ROWS, (r + 1) * FFN_ROWS)
                        h_s[rows, :] = _normmod(xa_ref[rows, :], g, sc, sh).astype(BF16)
                proj_w[:, cols] = jnp.dot(h_s[...], win_ref[:, cols], preferred_element_type=F32)
            return run

        fill = ([in_chunk(k) for k in range(IN_COLS // MIXER_TN)]
                + [out_chunk(k) for k in range(D_MODEL // MIXER_TN)])
        _mix_tile(proj_r, mix_w, cs_ref, sn_ref, cw_ref, cb_ref, wa_ref, ba_ref, wx_ref, bx_ref, lam_ref,
                  sink_ref, xpad, hc, kpad, vpad, a_s, u_s, klast, ts=ts, pos0=t_in * ts, mxu_fill=fill)

        @pl.when((s >= 1) & (s <= ntiles))
        def _():
            lru_ref[0] = hc[0:1, :]
            k_ref[0] = klast[...]
            v_ref[0] = proj_r[ts - WINDOW:, V_COL:V_COL + KV_WIDTH]
            conv_ref[0] = xpad[0:SUBLANES, :]

    @pl.when((s & 1) == 0)
    def _():
        step(0)

    @pl.when((s & 1) == 1)
    def _():
        step(1)


def _mixer_prompt(x, mod, g_mix, w_in_b, w_out_b, cos, sin, p, *, batch, seq):
    ts = MIX_TS
    nt = seq // ts
    ntiles = batch * nt
    assert nt & (nt - 1) == 0 and seq & (seq - 1) == 0
    w = LRU_WIDTH
    body = functools.partial(_mixer_body, ts=ts, nt=nt, ntiles=ntiles, batch_shift=seq.bit_length() - 1)

    def tile_a(s):
        return jnp.minimum(s, ntiles - 1)

    def tile_b(s):
        return jnp.clip(s - 1, 0, ntiles - 1)

    def tile_c(s):
        return jnp.clip(s - 2, 0, ntiles - 1)

    def whole(shape, **kw):
        return pl.BlockSpec(shape, lambda s: (0,) * len(shape), **kw)

    def mspec(c):
        return pl.BlockSpec((MOD_ROWS, D_MODEL), lambda s: (0, c))

    def state(shape):
        return pl.BlockSpec((1,) + shape, lambda s: (tile_b(s) // nt, 0, 0))

    once = dict(pipeline_mode=pl.Buffered(1))
    return pl.pallas_call(
        body,
        grid=(ntiles + 2,),
        in_specs=[
            pl.BlockSpec((ts, D_MODEL), lambda s: (tile_a(s), 0)),
            pl.BlockSpec((ts, D_MODEL), lambda s: (tile_c(s), 0)),
            mspec(3), mspec(4), mspec(5),
            whole((1, D_MODEL)),
            whole((D_MODEL, IN_COLS), **once),
            whole((D_MODEL, D_MODEL), **once),
            pl.BlockSpec((ts, HEAD_DIM), lambda s: (tile_b(s) % nt, 0)),
            pl.BlockSpec((ts, HEAD_DIM), lambda s: (tile_b(s) % nt, 0)),
            whole((CONV_W, w)), whole((1, w)),
            whole((LRU_HEADS, LRU_BLOCK, LRU_BLOCK)), whole((1, w)),
            whole((LRU_HEADS, LRU_BLOCK, LRU_BLOCK)), whole((1, w)),
            whole((1, w)),
            pl.BlockSpec(memory_space=pltpu.SMEM),
        ],
        out_specs=[
            pl.BlockSpec((ts, D_MODEL), lambda s: (tile_c(s), 0)),
            state((1, w)), state((WINDOW, KV_WIDTH)), state((WINDOW, KV_WIDTH)), state((SUBLANES, w)),
        ],
        out_shape=[
            jax.ShapeDtypeStruct((batch * seq, D_MODEL), F32),
            jax.ShapeDtypeStruct((batch, 1, w), F32),
            jax.ShapeDtypeStruct((batch, WINDOW, KV_WIDTH), F32),
            jax.ShapeDtypeStruct((batch, WINDOW, KV_WIDTH), F32),
            jax.ShapeDtypeStruct((batch, SUBLANES, w), F32),
        ],
        scratch_shapes=[
            pltpu.VMEM((ts, IN_COLS), F32), pltpu.VMEM((ts, IN_COLS), F32),
            pltpu.VMEM((ts, D_MODEL), BF16), pltpu.VMEM((ts, D_MODEL), BF16),
            pltpu.VMEM((ts, D_MODEL), BF16),
            pltpu.VMEM((SUBLANES + ts, w), F32),
            pltpu.VMEM((SUBLANES, w), F32),
            pltpu.VMEM((WINDOW + ts, KV_WIDTH), BF16),
            pltpu.VMEM((WINDOW + ts, KV_WIDTH), BF16),
            pltpu.VMEM((ts, w), F32),
            pltpu.VMEM((ts, w), F32),
            pltpu.VMEM((WINDOW, KV_WIDTH), F32),
        ],
        compiler_params=_params(("arbitrary",)),
        name="mixer_prompt",
    )(x, x, mod, mod, mod, g_mix.reshape(1, D_MODEL), w_in_b, w_out_b, cos, sin,
      p["conv_w"], p["conv_b"], p["wa"], p["ba"], p["wx"], p["bx"], p["lam"], p["sinks"])


def _mix_sample_body(proj_ref, prev_ref, h0_ref, ck_ref, cv_ref, cs_ref, sn_ref, cw_ref, cb_ref,
                     wa_ref, ba_ref, wx_ref, bx_ref, lam_ref, sink_ref,
                     mix_ref, lru_ref, knew_ref,
                     a_s, u_s, q_s, *, batch, seq):
    w = LRU_WIDTH
    row = lax.broadcasted_iota(jnp.int32, (SUBLANES, w), 0)

    def group(b):
        return pl.ds(pl.multiple_of(b * SUBLANES, SUBLANES), SUBLANES)

    def conv_body(b, carry):
        rows = group(b)
        cur = proj_ref[rows, 0:w]
        prev = prev_ref[rows, :]
        xc = cb_ref[...]
        for jj in range(CONV_W):
            d = CONV_W - 1 - jj
            if d == 0:
                term = cur
            else:
                term = jnp.where(row >= d, pltpu.roll(cur, d, 0), pltpu.roll(prev, d, 0))
            xc = xc + term * cw_ref[jj:jj + 1, :]
        u_s[rows, :] = xc
        return carry

    lax.fori_loop(0, batch, conv_body, 0)

    _lru_gates(u_s, a_s, wa_ref, ba_ref, wx_ref, bx_ref, lam_ref)

    def scan_body(b, carry):
        rows = group(b)
        h0 = jnp.broadcast_to(h0_ref[pl.ds(b, 1), :], (SUBLANES, w))
        h = _scan_group(a_s[rows, :], u_s[rows, :], h0, row)
        a_s[rows, :] = h
        lru_ref[pl.ds(b, 1), :] = h[SUBLANES - 1:SUBLANES, :]
        return carry

    lax.fori_loop(0, batch, scan_body, 0)
    mix_ref[:, 0:w] = a_s[...] * jax.nn.gelu(proj_ref[:, w:2 * w])

    cos = cs_ref[...]
    sin = sn_ref[...]
    for hd in range(N_HEADS):
        cols = slice(hd * HEAD_DIM, (hd + 1) * HEAD_DIM)
        q_s[:, cols] = _rope(proj_ref[:, Q_COL + hd * HEAD_DIM:Q_COL + (hd + 1) * HEAD_DIM], cos, sin)
    for kh in range(N_KV_HEADS):
        cols = slice(kh * HEAD_DIM, (kh + 1) * HEAD_DIM)
        knew_ref[:, cols] = _rope(proj_ref[:, K_COL + kh * HEAD_DIM:K_COL + (kh + 1) * HEAD_DIM], cos, sin)

    nq = GROUP * seq
    nk = 2 * WINDOW
    r_i = lax.broadcasted_iota(jnp.int32, (nq, nk), 0) & (seq - 1)
    c_i = lax.broadcasted_iota(jnp.int32, (nq, nk), 1)
    mask = (c_i >= r_i) & (c_i <= r_i + WINDOW)
    pad = jnp.zeros((nk - WINDOW - seq, HEAD_DIM), F32)

    def attn_body(b, carry):
        rows = group(b)
        for kh in range(N_KV_HEADS):
            cols = slice(kh * HEAD_DIM, (kh + 1) * HEAD_DIM)
            q4 = jnp.concatenate(
                [q_s[rows, (kh * GROUP + g) * HEAD_DIM:(kh * GROUP + g + 1) * HEAD_DIM] for g in range(GROUP)],
                axis=0).astype(BF16)
            kk = jnp.concatenate([ck_ref[b, :, cols], knew_ref[rows, cols], pad], axis=0).astype(BF16)
            vv = jnp.concatenate(
                [cv_ref[b, :, cols], proj_ref[rows, V_COL + kh * HEAD_DIM:V_COL + (kh + 1) * HEAD_DIM], pad],
                axis=0).astype(BF16)
            s = lax.dot_general(q4, kk, (((1,), (1,)), ((), ())), preferred_element_type=F32) * SCALE
            o = _softmax_pv(s, mask, _sink_column(sink_ref, kh, seq), vv)
            for g in range(GROUP):
                hd = kh * GROUP + g
                mix_ref[rows, w + hd * HEAD_DIM:w + (hd + 1) * HEAD_DIM] = o[g * seq:(g + 1) * seq, :]
        return carry

    lax.fori_loop(0, batch, attn_body, 0)


def _mix_sample(proj, prev, h0, cache_k, cache_v, cos, sin, p, *, batch, seq):
    assert seq == SUBLANES, "each sample batch must be exactly one sublane group"
    t = batch * seq
    body = functools.partial(_mix_sample_body, batch=batch, seq=seq)
    vmem = pl.BlockSpec(memory_space=pltpu.VMEM)
    return pl.pallas_call(
        body,
        in_specs=[vmem] * 14 + [pl.BlockSpec(memory_space=pltpu.SMEM)],
        out_specs=[vmem, vmem, vmem],
        out_shape=[
            jax.ShapeDtypeStruct((t, D_MODEL), F32),
            jax.ShapeDtypeStruct((batch, LRU_WIDTH), F32),
            jax.ShapeDtypeStruct((t, KV_WIDTH), F32),
        ],
        scratch_shapes=[
            pltpu.VMEM((t, LRU_WIDTH), F32),
            pltpu.VMEM((t, LRU_WIDTH), F32),
            pltpu.VMEM((t, ATTN_WIDTH), F32),
        ],
        compiler_params=pltpu.CompilerParams(vmem_limit_bytes=V7X_VMEM_LIMIT_BYTES),
        name="mix_sample",
    )(proj, prev, h0, cache_k, cache_v, cos, sin, p["conv_w"], p["conv_b"], p["wa"], p["ba"], p["wx"], p["bx"],
      p["lam"], p["sinks"])


def _layer(w, l):
    return w.reshape(w.shape[1:]) if w.shape[0] == 1 else w[l]


def _rope_tables(pos):
    half = HEAD_DIM // 2
    inv = ROPE_THETA ** (-jnp.arange(half, dtype=F32) / half)
    ang = pos.astype(F32)[:, None] * inv[None, :]
    cos = jnp.cos(ang)
    sin = jnp.sin(ang)
    return jnp.concatenate([cos, cos], axis=-1), jnp.concatenate([-sin, sin], axis=-1)


def kernel(x_prompt, x_sample, c_prompt, c_sample, state_conv, state_lru, cache_k_win, cache_v_win, ada_w, ada_b, norm_ffn1, norm_mix, norm_ffn2, ffn1_w_gate, ffn1_w_up, ffn1_w_down, w_in, conv_w, conv_b, lru_w_a, lru_b_a, lru_w_x, lru_b_x, lru_lambda, attn_sinks, w_out, ffn2_w_gate, ffn2_w_up, ffn2_w_down, norm_final):
    bp, sp, _ = x_prompt.shape
    bs, ss, _ = x_sample.shape
    depth = ada_w.shape[0]
    tp, tsn = bp * sp, bs * ss
    assert bp <= MOD_PROMPT_ROWS and bp + bs <= MOD_ROWS
    tm_p = 1024
    assert sp % tm_p == 0

    xp = x_prompt.reshape(tp, D_MODEL)
    xs = x_sample.reshape(tsn, D_MODEL)
    c_all = jnp.concatenate(
        [c_prompt, jnp.zeros((MOD_PROMPT_ROWS - bp, D_MODEL), F32), c_sample,
         jnp.zeros((MOD_ROWS - MOD_PROMPT_ROWS - bs, D_MODEL), F32)], axis=0)
    cos_p, sin_p = _rope_tables(jnp.arange(sp))
    cos_s, sin_s = _rope_tables(PAST_LEN + jnp.arange(ss))
    cos_s = jnp.tile(cos_s, (bs, 1))
    sin_s = jnp.tile(sin_s, (bs, 1))

    outs_p, outs_s = [], []
    for l in range(depth):
        last = l == depth - 1
        L = functools.partial(_layer, l=l)
        cache_k, cache_v = L(cache_k_win), L(cache_v_win)
        mod = _adaln(c_all, L(ada_w), L(ada_b))
        p = dict(conv_w=L(conv_w), conv_b=L(conv_b).reshape(1, LRU_WIDTH), wa=L(lru_w_a),
                 ba=L(lru_b_a).reshape(1, LRU_WIDTH), wx=L(lru_w_x), bx=L(lru_b_x).reshape(1, LRU_WIDTH),
                 lam=L(lru_lambda).reshape(1, LRU_WIDTH), sinks=L(attn_sinks))
        ffn1 = (L(norm_ffn1), L(ffn1_w_gate), L(ffn1_w_up), L(ffn1_w_down), norm_final)
        ffn2 = (L(norm_ffn2), L(ffn2_w_gate), L(ffn2_w_up), L(ffn2_w_down), norm_final)
        ffn_p = dict(tm=tm_p, nr=FFN_ROWS, row0=0, rows_per_batch=sp)
        ffn_s = dict(tm=tsn, nr=ss, row0=MOD_PROMPT_ROWS, rows_per_batch=ss)
        grp_s = dict(emit=True, **ffn_s)

        g, wg, wu, wd, gf = ffn1
        xs, wg_b, wu_b, wd_b = _ffn(xs, mod, (0, 1, 2), g, wg, wu, wd, gf, emit=True, final_norm=False,
                                    name="ffn1_sample", **ffn_s)
        xp, = _ffn(xp, mod, (0, 1, 2), g, wg_b, wu_b, wd_b, gf, emit=False, final_norm=False,
                   name="ffn1_prompt", **ffn_p)

        proj_s, w_in_b = _inproj(xs, mod, (3, 4), L(norm_mix), L(w_in), tn=PROJ_TN, name="inproj_sample",
                                 **grp_s)
        prev_s = jnp.pad(L(state_conv), ((0, 0), (SUBLANES - (CONV_W - 1), 0), (0, 0))).reshape(tsn, LRU_WIDTH)
        mix_s, lru_s, knew_s = _mix_sample(
            proj_s, prev_s, L(state_lru), cache_k.reshape(bs, WINDOW, KV_WIDTH),
            cache_v.reshape(bs, WINDOW, KV_WIDTH), cos_s, sin_s, p, batch=bs, seq=ss)
        xs, w_out_b = _outproj(mix_s, xs, mod, 5, L(w_out), tn=PROJ_TN, name="outproj_sample", **grp_s)
        xp, lru_p, k_p, v_p, conv_p = _mixer_prompt(xp, mod, L(norm_mix), w_in_b, w_out_b, cos_p, sin_p, p,
                                                    batch=bp, seq=sp)

        g, wg, wu, wd, gf = ffn2
        xp, wg_b, wu_b, wd_b = _ffn(xp, mod, (6, 7, 8), g, wg, wu, wd, gf, emit=True, tf=FFN_EMIT_TF, ntiles=1,
                                    inplace=True, final_norm=last, name="ffn2_head", **ffn_p)
        xp, = _ffn(xp, mod, (6, 7, 8), g, wg_b, wu_b, wd_b, gf, emit=False, tile0=1, ntiles=tp // tm_p - 1,
                   inplace=True, final_norm=last, name="ffn2_prompt", **ffn_p)
        xs, = _ffn(xs, mod, (6, 7, 8), g, wg_b, wu_b, wd_b, gf, emit=False, final_norm=last,
                   name="ffn2_sample", **ffn_s)

        ps = proj_s.reshape(bs, ss, IN_COLS)
        outs_p.append((
            conv_p[:, SUBLANES - (CONV_W - 1):, :],
            lru_p.reshape(bp, LRU_WIDTH),
            k_p.reshape(bp, WINDOW, N_KV_HEADS, HEAD_DIM),
            v_p.reshape(bp, WINDOW, N_KV_HEADS, HEAD_DIM),
        ))
        k_all = jnp.concatenate([cache_k, knew_s.reshape(bs, ss, N_KV_HEADS, HEAD_DIM)], axis=1)
        v_all = jnp.concatenate([cache_v, ps[:, :, V_COL:].reshape(bs, ss, N_KV_HEADS, HEAD_DIM)], axis=1)
        outs_s.append((
            ps[:, ss - (CONV_W - 1):, 0:LRU_WIDTH],
            lru_s,
            k_all[:, -WINDOW:],
            v_all[:, -WINDOW:],
        ))

    y_prompt = xp.reshape(bp, sp, D_MODEL)
    y_sample = xs.reshape(bs, ss, D_MODEL)
    st_p = [jnp.stack([o[i] for o in outs_p]) for i in range(4)]
    st_s = [jnp.stack([o[i] for o in outs_s]) for i in range(4)]
    return (y_prompt, y_sample, st_p[0], st_p[1], st_p[2], st_p[3], st_s[0], st_s[1], st_s[2], st_s[3])
```

```python
import functools

import jax
import jax.numpy as jnp
from jax import lax
from jax.experimental import pallas as pl
from jax.experimental.pallas import tpu as pltpu

F32 = jnp.float32
BF16 = jnp.bfloat16

D_MODEL = 2048
D_FF = 5504
LRU_WIDTH = 1024
LRU_HEADS = 8
LRU_BLOCK = 128
CONV_W = 4
LRU_C = 8.0
HEAD_DIM = 128
N_HEADS = 8
N_KV_HEADS = 2
GROUP = N_HEADS // N_KV_HEADS
ATTN_WIDTH = N_HEADS * HEAD_DIM
KV_WIDTH = N_KV_HEADS * HEAD_DIM
WINDOW = 128
ROPE_THETA = 10000.0
N_MOD = 9
EPS = 1e-6
NEG = -1e30
PAST_LEN = 16384
IN_COLS = 2 * LRU_WIDTH + ATTN_WIDTH + 2 * KV_WIDTH
Q_COL = 2 * LRU_WIDTH
K_COL = Q_COL + ATTN_WIDTH
V_COL = K_COL + KV_WIDTH
SCALE = HEAD_DIM ** -0.5

SUBLANES = 8
LANES = 128
V7X_VMEM_LIMIT_BYTES = 56 * 1024 * 1024

MOD_ROWS = 48
MOD_PROMPT_ROWS = 8
ADALN_TN = 1024
BF16_ROWS = 16
FFN_TF = 512
FFN_EMIT_TF = 256
FFN_ROWS = 16
FFN_EPILOGUE_ROWS = 128
FFN_ROW_BLOCK = 256
FFN_FINISH_ROWS = 32
INPROJ_TN = IN_COLS // 2
OUTPROJ_TN = D_MODEL // 2
SAMPLE_ATTN_BATCHES = 4
MIX_TS = 256
MIXER_TN = 256


def _params(sem, vmem=V7X_VMEM_LIMIT_BYTES):
    return pltpu.CompilerParams(dimension_semantics=sem, vmem_limit_bytes=vmem)


def _sigmoid(x):
    return 1.0 / (1.0 + jnp.exp(-x))


def _normmod(x, g, sc, sh):
    ms = jnp.mean(x * x, axis=-1, keepdims=True)
    xn = x * lax.rsqrt(ms + EPS) * g
    return xn * (1.0 + sc) + sh


def _mod_row(ref, token, row0, batch_shift):
    return ref[pl.ds(row0 + lax.shift_right_logical(token, batch_shift), 1), :]


def _mod_spec(chunk):
    return pl.BlockSpec((MOD_ROWS, D_MODEL), lambda i, j: (0, chunk))


def _adaln_body(c_ref, w_ref, b_ref, o_ref):
    c = c_ref[...]
    s = (c * _sigmoid(c)).astype(BF16)
    o_ref[...] = jnp.dot(s, w_ref[...].astype(BF16), preferred_element_type=F32) + b_ref[...]


def _adaln(c_all, ada_w, ada_b):
    n = ada_w.shape[1]
    return pl.pallas_call(
        _adaln_body,
        grid=(n // ADALN_TN,),
        in_specs=[
            pl.BlockSpec((MOD_ROWS, D_MODEL), lambda j: (0, 0)),
            pl.BlockSpec((D_MODEL, ADALN_TN), lambda j: (0, j)),
            pl.BlockSpec((1, ADALN_TN), lambda j: (0, j)),
        ],
        out_specs=pl.BlockSpec((MOD_ROWS, ADALN_TN), lambda j: (0, j)),
        out_shape=jax.ShapeDtypeStruct((MOD_ROWS, n), F32),
        compiler_params=_params(("arbitrary",)),
        name="adaln",
    )(c_all, ada_w, ada_b.reshape(1, n))


def _ffn_body(x_ref, sh_ref, sc_ref, gt_ref, g_ref, wg_ref, wu_ref, wd_ref, gf_ref, *rest,
              nr, er, rb, row0, batch_shift, tile0, final_norm, emit):
    if emit:
        o_ref, wgb_ref, wub_ref, wdb_ref, h_ref = rest
    else:
        o_ref, h_ref = rest
    i = pl.program_id(0) + tile0
    j = pl.program_id(1)
    nj = pl.num_programs(1)
    tm = x_ref.shape[0]
    tf = wd_ref.shape[0]

    if emit:
        valid = D_FF - j * tf
        cmask = lax.broadcasted_iota(jnp.int32, (1, tf), 1) < valid
        rmask = lax.broadcasted_iota(jnp.int32, (tf, 1), 0) < valid
        wgb_ref[...] = jnp.where(cmask, wg_ref[...], 0.0).astype(BF16)
        wub_ref[...] = jnp.where(cmask, wu_ref[...], 0.0).astype(BF16)
        wdb_ref[...] = jnp.where(rmask, wd_ref[...], 0.0).astype(BF16)
        wg_ref, wu_ref, wd_ref = wgb_ref, wub_ref, wdb_ref

    def swiglu(rows):
        h = h_ref[rows, :].astype(BF16)
        g = jnp.dot(h, wg_ref[...], preferred_element_type=F32)
        u = jnp.dot(h, wu_ref[...], preferred_element_type=F32)
        a = (g * _sigmoid(g) * u).astype(BF16)
        return jnp.dot(a, wd_ref[...], preferred_element_type=F32)

    def prologue(rows, token):
        h = _normmod(x_ref[rows, :], g_ref[...], _mod_row(sc_ref, token, row0, batch_shift),
                     _mod_row(sh_ref, token, row0, batch_shift))
        h_ref[rows, :] = h.astype(h_ref.dtype)

    def epilogue(rows, token, acc):
        y = x_ref[rows, :] + (0.5 * _mod_row(gt_ref, token, row0, batch_shift)) * acc
        if final_norm:
            ms = jnp.mean(y * y, axis=-1, keepdims=True)
            y = y * lax.rsqrt(ms + EPS) * gf_ref[...]
        o_ref[rows, :] = y

    if rb:
        blocks = range(0, tm, rb)

        def block_prologue(r0):
            for c in range(r0, r0 + rb, nr):
                prologue(slice(c, c + nr), i * tm + c)

        @pl.when(j == 0)
        def _():
            block_prologue(0)
            for r0 in blocks:
                o_ref[r0:r0 + rb, :] = swiglu(slice(r0, r0 + rb))
                if r0 + rb < tm:
                    block_prologue(r0 + rb)

        @pl.when((j > 0) & (j < nj - 1))
        def _():
            o_ref[...] += swiglu(slice(None))

        @pl.when(j == nj - 1)
        def _():
            for r0 in blocks:
                acc = o_ref[r0:r0 + rb, :] + swiglu(slice(r0, r0 + rb))
                for c in range(0, rb, FFN_FINISH_ROWS):
                    epilogue(slice(r0 + c, r0 + c + FFN_FINISH_ROWS), i * tm + r0 + c,
                             acc[c:c + FFN_FINISH_ROWS, :])
    else:
        @pl.when(j == 0)
        def _():
            def body(r, carry):
                rows = pl.ds(pl.multiple_of(r * nr, nr), nr)
                prologue(rows, i * tm + r * nr)
                o_ref[rows, :] = jnp.zeros((nr, D_MODEL), F32)
                return carry

            lax.fori_loop(0, tm // nr, body, 0, unroll=4)

        o_ref[...] += swiglu(slice(None))

        @pl.when(j == nj - 1)
        def _():
            def body(r, carry):
                rows = pl.ds(pl.multiple_of(r * er, er), er)
                epilogue(rows, i * tm + r * er, o_ref[rows, :])
                return carry

            lax.fori_loop(0, tm // er, body, 0)


def _ffn(x, mod, chunks, g, wg, wu, wd, gf, *, tm, nr, row0, rows_per_batch, final_norm, emit, name,
         tf=FFN_TF, tile0=0, ntiles=None, inplace=False):
    t = x.shape[0]
    ntiles = t // tm if ntiles is None else ntiles
    assert tf == FFN_TF or emit
    sub = FFN_TF // tf
    nj_store = pl.cdiv(D_FF, FFN_TF)
    nj = nj_store * sub
    er = min(FFN_EPILOGUE_ROWS, rows_per_batch)
    assert rows_per_batch & (rows_per_batch - 1) == 0 and rows_per_batch % nr == 0 and tm % nr == 0
    assert rows_per_batch % er == 0 and tm % er == 0
    rb = FFN_ROW_BLOCK if rows_per_batch % FFN_ROW_BLOCK == 0 and tm % FFN_ROW_BLOCK == 0 else 0
    body = functools.partial(_ffn_body, nr=nr, er=er, rb=rb, row0=row0,
                             batch_shift=rows_per_batch.bit_length() - 1, tile0=tile0,
                             final_norm=final_norm, emit=emit)

    row = pl.BlockSpec((1, D_MODEL), lambda i, j: (0, 0))
    tile_major = pl.BlockSpec((None, D_MODEL, tf), lambda i, j: (j // sub, 0, j % sub))
    if emit:
        assert ntiles == 1, "bf16 weight tiles are written once, by a single token tile"
        wspecs = [pl.BlockSpec((D_MODEL, tf), lambda i, j: (0, j)),
                  pl.BlockSpec((D_MODEL, tf), lambda i, j: (0, j)),
                  pl.BlockSpec((tf, D_MODEL), lambda i, j: (j, 0))]
    else:
        wspecs = [tile_major, tile_major, pl.BlockSpec((tf, D_MODEL), lambda i, j: (j, 0))]
    xmode = dict(pipeline_mode=pl.Buffered(1)) if ntiles == 1 else {}
    in_specs = [pl.BlockSpec((tm, D_MODEL), lambda i, j: (i + tile0, 0), **xmode),
                _mod_spec(chunks[0]), _mod_spec(chunks[1]), _mod_spec(chunks[2]), row] + wspecs + [row]
    args = [x, mod, mod, mod, g.reshape(1, D_MODEL), wg, wu, wd, gf.reshape(1, D_MODEL)]
    out_specs = [pl.BlockSpec((tm, D_MODEL), lambda i, j: (i + tile0, 0))]
    out_shape = [jax.ShapeDtypeStruct((t, D_MODEL), F32)]
    if emit:
        out_specs += [tile_major, tile_major, pl.BlockSpec((tf, D_MODEL), lambda i, j: (j, 0))]
        out_shape += [jax.ShapeDtypeStruct((nj_store, D_MODEL, FFN_TF), BF16),
                      jax.ShapeDtypeStruct((nj_store, D_MODEL, FFN_TF), BF16),
                      jax.ShapeDtypeStruct((nj_store * FFN_TF, D_MODEL), BF16)]
    return pl.pallas_call(
        body,
        grid=(ntiles, nj),
        in_specs=in_specs,
        out_specs=out_specs,
        out_shape=out_shape,
        input_output_aliases={0: 0} if inplace else {},
        scratch_shapes=[pltpu.VMEM((tm, D_MODEL), BF16 if nr % BF16_ROWS == 0 else F32)],
        compiler_params=_params(("parallel", "arbitrary")),
        name=name,
    )(*args)


def _inproj_body(x_ref, sh_ref, sc_ref, g_ref, w_ref, o_ref, wb_ref, h_ref, *, nr, row0, batch_shift):
    j = pl.program_id(1)
    tm = x_ref.shape[0]

    @pl.when(j == 0)
    def _():
        g = g_ref[...]

        def body(r, carry):
            rows = pl.ds(pl.multiple_of(r * nr, nr), nr)
            tok = r * nr
            h = _normmod(x_ref[rows, :], g, _mod_row(sc_ref, tok, row0, batch_shift),
                         _mod_row(sh_ref, tok, row0, batch_shift))
            h_ref[rows, :] = h
            return carry

        lax.fori_loop(0, tm // nr, body, 0, unroll=4)

    w = w_ref[...].astype(BF16)
    wb_ref[...] = w
    o_ref[...] = jnp.dot(h_ref[...].astype(BF16), w, preferred_element_type=F32)


def _inproj(x, mod, chunks, g, w_in, *, tn, nr, row0, rows_per_batch, name):
    tm = x.shape[0]
    assert rows_per_batch & (rows_per_batch - 1) == 0 and rows_per_batch % nr == 0 and tm % nr == 0
    body = functools.partial(_inproj_body, nr=nr, row0=row0, batch_shift=rows_per_batch.bit_length() - 1)
    wspec = pl.BlockSpec((D_MODEL, tn), lambda i, j: (0, j))
    return pl.pallas_call(
        body,
        grid=(1, IN_COLS // tn),
        in_specs=[
            pl.BlockSpec((tm, D_MODEL), lambda i, j: (0, 0)),
            _mod_spec(chunks[0]), _mod_spec(chunks[1]),
            pl.BlockSpec((1, D_MODEL), lambda i, j: (0, 0)),
            wspec,
        ],
        out_specs=[pl.BlockSpec((tm, tn), lambda i, j: (0, j)), wspec],
        out_shape=[jax.ShapeDtypeStruct((tm, IN_COLS), F32), jax.ShapeDtypeStruct((D_MODEL, IN_COLS), BF16)],
        scratch_shapes=[pltpu.VMEM((tm, D_MODEL), F32)],
        compiler_params=_params(("arbitrary", "arbitrary")),
        name=name,
    )(x, mod, mod, g.reshape(1, D_MODEL), w_in)


def _outproj_body(m_ref, x_ref, gt_ref, w_ref, o_ref, wb_ref, *, nr, row0, batch_shift):
    tm = x_ref.shape[0]
    w = w_ref[...].astype(BF16)
    wb_ref[...] = w
    o_ref[...] = jnp.dot(m_ref[...].astype(BF16), w, preferred_element_type=F32)

    def body(r, carry):
        rows = pl.ds(pl.multiple_of(r * nr, nr), nr)
        gt = _mod_row(gt_ref, r * nr, row0, batch_shift)
        o_ref[rows, :] = x_ref[rows, :] + gt * o_ref[rows, :]
        return carry

    lax.fori_loop(0, tm // nr, body, 0)


def _outproj(mix, x, mod, gt_c, w_out, *, tn, nr, row0, rows_per_batch, name):
    tm = x.shape[0]
    nn = D_MODEL // tn
    assert rows_per_batch & (rows_per_batch - 1) == 0 and rows_per_batch % nr == 0 and tm % nr == 0
    body = functools.partial(_outproj_body, nr=nr, row0=row0, batch_shift=rows_per_batch.bit_length() - 1)
    wspec = pl.BlockSpec((D_MODEL, tn), lambda i, j: (0, j))
    tile = pl.BlockSpec((tm, tn), lambda i, j: (0, j))
    return pl.pallas_call(
        body,
        grid=(1, nn),
        in_specs=[
            pl.BlockSpec((tm, D_MODEL), lambda i, j: (0, 0)),
            tile,
            pl.BlockSpec((MOD_ROWS, tn), lambda i, j: (0, gt_c * nn + j)),
            wspec,
        ],
        out_specs=[tile, wspec],
        out_shape=[jax.ShapeDtypeStruct((tm, D_MODEL), F32), jax.ShapeDtypeStruct((D_MODEL, D_MODEL), BF16)],
        compiler_params=_params(("arbitrary", "arbitrary")),
        name=name,
    )(mix, x, mod, w_out)


def _rope(x, cos, sin_signed):
    return x * cos + pltpu.roll(x, HEAD_DIM // 2, 1) * sin_signed


def _lru_gates(xc_ref, a_ref, wa_ref, ba_ref, wx_ref, bx_ref, lam_ref, midway=None):
    nlam = -lam_ref[...]
    softplus = jnp.maximum(nlam, 0.0) + jnp.log1p(jnp.exp(-jnp.abs(nlam)))
    rate = -LRU_C * softplus
    for hh in range(LRU_HEADS):
        if midway is not None:
            midway()
        cols = slice(hh * LRU_BLOCK, (hh + 1) * LRU_BLOCK)
        xc = xc_ref[:, cols]
        xcb = xc.astype(BF16)
        ra = jnp.dot(xcb, wa_ref[hh].astype(BF16), preferred_element_type=F32) + ba_ref[:, cols]
        rx = jnp.dot(xcb, wx_ref[hh].astype(BF16), preferred_element_type=F32) + bx_ref[:, cols]
        r = _sigmoid(ra)
        gi = _sigmoid(rx)
        a = jnp.exp(r * rate[:, cols])
        a_ref[:, cols] = a
        xc_ref[:, cols] = jnp.sqrt(1.0 - a * a) * (gi * xc)


def _scan_group(a, u, carry, row):
    for s in (1, 2, 4):
        a_sh = pltpu.roll(a, s, 0)
        u_sh = pltpu.roll(u, s, 0)
        m = row >= s
        u = jnp.where(m, a * u_sh + u, u)
        a = jnp.where(m, a * a_sh, a)
    return a * carry + u


def _softmax_pv(s, mask, sink, v):
    s = jnp.where(mask, s, NEG)
    m = jnp.maximum(jnp.max(s, axis=-1, keepdims=True), sink)
    p = jnp.exp(s - m)
    den = jnp.sum(p, axis=-1, keepdims=True) + jnp.exp(sink - m)
    return jnp.dot(p.astype(BF16), v, preferred_element_type=F32) / den


def _sink_column(sink_ref, kh, rows_per_head):
    n = GROUP * rows_per_head
    ri = lax.broadcasted_iota(jnp.int32, (n, 1), 0)
    col = jnp.full((n, 1), sink_ref[kh * GROUP + GROUP - 1], F32)
    for g in range(GROUP - 2, -1, -1):
        col = jnp.where(ri < (g + 1) * rows_per_head, sink_ref[kh * GROUP + g], col)
    return col


def _mix_tile(proj, mix, cs_ref, sn_ref, cw_ref, cb_ref, wa_ref, ba_ref, wx_ref, bx_ref, lam_ref, sink_ref,
              xpad, hc, kpad, vpad, a_s, u_s, klast, *, ts, pos0, mxu_fill):
    fill = iter(mxu_fill)

    def take(n):
        for _ in range(n):
            f = next(fill, None)
            if f is not None:
                f()

    w = LRU_WIDTH
    xpad[SUBLANES:SUBLANES + ts, :] = proj[:, 0:w]
    half = ts // 2
    for r0 in (0, half):
        take(1)
        xc = cb_ref[...]
        for jj in range(CONV_W):
            off = r0 + SUBLANES - (CONV_W - 1) + jj
            xc = xc + xpad[off:off + half, :] * cw_ref[jj:jj + 1, :]
        u_s[r0:r0 + half, :] = xc
    xpad[0:SUBLANES, :] = xpad[ts:ts + SUBLANES, :]

    _lru_gates(u_s, a_s, wa_ref, ba_ref, wx_ref, bx_ref, lam_ref, midway=lambda: take(1))

    row = lax.broadcasted_iota(jnp.int32, (SUBLANES, w), 0)
    carry = hc[...]
    ngroups = ts // SUBLANES
    for g in range(ngroups):
        if g % (ngroups // 8) == 0:
            take(1)
        rows = slice(g * SUBLANES, (g + 1) * SUBLANES)
        h = _scan_group(a_s[rows, :], u_s[rows, :], carry, row)
        a_s[rows, :] = h
        carry = jnp.broadcast_to(h[SUBLANES - 1:SUBLANES, :], (SUBLANES, w))
    hc[...] = carry
    for r0 in (0, half):
        take(1)
        rows = slice(r0, r0 + half)
        mix[rows, 0:w] = (a_s[rows, :] * jax.nn.gelu(proj[rows, w:2 * w])).astype(BF16)

    take(1)
    cos = cs_ref[...]
    sin = sn_ref[...]
    for kh in range(N_KV_HEADS):
        cols = slice(kh * HEAD_DIM, (kh + 1) * HEAD_DIM)
        kr = _rope(proj[:, K_COL + kh * HEAD_DIM:K_COL + (kh + 1) * HEAD_DIM], cos, sin)
        kpad[WINDOW:WINDOW + ts, cols] = kr.astype(BF16)
        klast[:, cols] = kr[ts - WINDOW:, :]
    vpad[WINDOW:WINDOW + ts, :] = proj[:, V_COL:V_COL + KV_WIDTH].astype(BF16)

    nq = GROUP * WINDOW
    r_i = lax.broadcasted_iota(jnp.int32, (nq, 2 * WINDOW), 0) & (WINDOW - 1)
    c_i = lax.broadcasted_iota(jnp.int32, (nq, 2 * WINDOW), 1)
    band = (c_i >= r_i) & (c_i <= r_i + WINDOW)
    for n in range(ts // WINDOW):
        qrows = slice(n * WINDOW, (n + 1) * WINDOW)
        kpos0 = pos0 + (n - 1) * WINDOW
        mask = band & (c_i + kpos0 >= 0)
        for kh in range(N_KV_HEADS):
            cols = slice(kh * HEAD_DIM, (kh + 1) * HEAD_DIM)
            qs = []
            for g in range(GROUP):
                hd = kh * GROUP + g
                qh = proj[qrows, Q_COL + hd * HEAD_DIM:Q_COL + (hd + 1) * HEAD_DIM]
                qs.append(_rope(qh, cos[qrows, :], sin[qrows, :]).astype(BF16))
            q4 = jnp.concatenate(qs, axis=0)
            kk = kpad[n * WINDOW:(n + 2) * WINDOW, cols]
            vv = vpad[n * WINDOW:(n + 2) * WINDOW, cols]
            s = lax.dot_general(q4, kk, (((1,), (1,)), ((), ())), preferred_element_type=F32) * SCALE
            take(1)
            o = _softmax_pv(s, mask, _sink_column(sink_ref, kh, WINDOW), vv)
            for g in range(GROUP):
                hd = kh * GROUP + g
                mix[qrows, w + hd * HEAD_DIM:w + (hd + 1) * HEAD_DIM] = (
                    o[g * WINDOW:(g + 1) * WINDOW, :].astype(BF16))

    take(len(mxu_fill))
    kpad[0:WINDOW, :] = kpad[ts:ts + WINDOW, :]
    vpad[0:WINDOW, :] = vpad[ts:ts + WINDOW, :]


def _mixer_body(xa_ref, xc_ref, sh_ref, sc_ref, gt_ref, gm_ref, win_ref, wout_ref, cs_ref, sn_ref,
                cw_ref, cb_ref, wa_ref, ba_ref, wx_ref, bx_ref, lam_ref, sink_ref,
                o_ref, lru_ref, k_ref, v_ref, conv_ref,
                proj0, proj1, mix0, mix1, h_s, xpad, hc, kpad, vpad, a_s, u_s, klast,
                *, ts, nt, ntiles, batch_shift):
    s = pl.program_id(0)
    tile_a = jnp.minimum(s, ntiles - 1)
    tile_b = jnp.clip(s - 1, 0, ntiles - 1)
    tile_c = jnp.clip(s - 2, 0, ntiles - 1)
    t_in = tile_b & (nt - 1)

    @pl.when(s == 0)
    def _():
        proj1[...] = jnp.zeros(proj1.shape, F32)
        mix0[...] = jnp.zeros(mix0.shape, BF16)

    @pl.when(t_in == 0)
    def _():
        xpad[0:SUBLANES, :] = jnp.zeros((SUBLANES, LRU_WIDTH), F32)
        hc[...] = jnp.zeros((SUBLANES, LRU_WIDTH), F32)
        kpad[0:WINDOW, :] = jnp.zeros((WINDOW, KV_WIDTH), BF16)
        vpad[0:WINDOW, :] = jnp.zeros((WINDOW, KV_WIDTH), BF16)

    def step(par):
        proj_w, proj_r = (proj0, proj1) if par == 0 else (proj1, proj0)
        mix_w, mix_r = (mix1, mix0) if par == 0 else (mix0, mix1)

        gt = _mod_row(gt_ref, tile_c * ts, 0, batch_shift)

        def out_chunk(k):
            cols = slice(k * MIXER_TN, (k + 1) * MIXER_TN)

            def run():
                d = jnp.dot(mix_r[...], wout_ref[:, cols], preferred_element_type=F32)
                o_ref[:, cols] = xc_ref[:, cols] + gt[:, cols] * d
            return run

        def in_chunk(k):
            cols = slice(k * MIXER_TN, (k + 1) * MIXER_TN)

            def run():
                if k == 0:
                    g = gm_ref[...]
                    sc = _mod_row(sc_ref, tile_a * ts, 0, batch_shift)
                    sh = _mod_row(sh_ref, tile_a * ts, 0, batch_shift)
                    for r in range(ts // FFN_ROWS):
                        rows = slice(r * FFN_ROWS, (r + 1) * FFN_ROWS)
                        h_s[rows, :] = _normmod(xa_ref[rows, :], g, sc, sh).astype(BF16)
                proj_w[:, cols] = jnp.dot(h_s[...], win_ref[:, cols], preferred_element_type=F32)
            return run

        fill = ([in_chunk(k) for k in range(IN_COLS // MIXER_TN)]
                + [out_chunk(k) for k in range(D_MODEL // MIXER_TN)])
        _mix_tile(proj_r, mix_w, cs_ref, sn_ref, cw_ref, cb_ref, wa_ref, ba_ref, wx_ref, bx_ref, lam_ref,
                  sink_ref, xpad, hc, kpad, vpad, a_s, u_s, klast, ts=ts, pos0=t_in * ts, mxu_fill=fill)

        @pl.when((s >= 1) & (s <= ntiles))
        def _():
            lru_ref[0] = hc[0:1, :]
            k_ref[0] = klast[...]
            v_ref[0] = proj_r[ts - WINDOW:, V_COL:V_COL + KV_WIDTH]
            conv_ref[0] = xpad[0:SUBLANES, :]

    @pl.when((s & 1) == 0)
    def _():
        step(0)

    @pl.when((s & 1) == 1)
    def _():
        step(1)


def _mixer_prompt(x, mod, g_mix, w_in_b, w_out_b, cos, sin, p, *, batch, seq):
    ts = MIX_TS
    nt = seq // ts
    ntiles = batch * nt
    assert nt & (nt - 1) == 0 and seq & (seq - 1) == 0
    w = LRU_WIDTH
    body = functools.partial(_mixer_body, ts=ts, nt=nt, ntiles=ntiles, batch_shift=seq.bit_length() - 1)

    def tile_a(s):
        return jnp.minimum(s, ntiles - 1)

    def tile_b(s):
        return jnp.clip(s - 1, 0, ntiles - 1)

    def tile_c(s):
        return jnp.clip(s - 2, 0, ntiles - 1)

    def whole(shape, **kw):
        return pl.BlockSpec(shape, lambda s: (0,) * len(shape), **kw)

    def mspec(c):
        return pl.BlockSpec((MOD_ROWS, D_MODEL), lambda s: (0, c))

    def state(shape):
        return pl.BlockSpec((1,) + shape, lambda s: (tile_b(s) // nt, 0, 0))

    once = dict(pipeline_mode=pl.Buffered(1))
    return pl.pallas_call(
        body,
        grid=(ntiles + 2,),
        in_specs=[
            pl.BlockSpec((ts, D_MODEL), lambda s: (tile_a(s), 0)),
            pl.BlockSpec((ts, D_MODEL), lambda s: (tile_c(s), 0)),
            mspec(3), mspec(4), mspec(5),
            whole((1, D_MODEL)),
            whole((D_MODEL, IN_COLS), **once),
            whole((D_MODEL, D_MODEL), **once),
            pl.BlockSpec((ts, HEAD_DIM), lambda s: (tile_b(s) % nt, 0)),
            pl.BlockSpec((ts, HEAD_DIM), lambda s: (tile_b(s) % nt, 0)),
            whole((CONV_W, w)), whole((1, w)),
            whole((LRU_HEADS, LRU_BLOCK, LRU_BLOCK)), whole((1, w)),
            whole((LRU_HEADS, LRU_BLOCK, LRU_BLOCK)), whole((1, w)),
            whole((1, w)),
            pl.BlockSpec(memory_space=pltpu.SMEM),
        ],
        out_specs=[
            pl.BlockSpec((ts, D_MODEL), lambda s: (tile_c(s), 0)),
            state((1, w)), state((WINDOW, KV_WIDTH)), state((WINDOW, KV_WIDTH)), state((SUBLANES, w)),
        ],
        out_shape=[
            jax.ShapeDtypeStruct((batch * seq, D_MODEL), F32),
            jax.ShapeDtypeStruct((batch, 1, w), F32),
            jax.ShapeDtypeStruct((batch, WINDOW, KV_WIDTH), F32),
            jax.ShapeDtypeStruct((batch, WINDOW, KV_WIDTH), F32),
            jax.ShapeDtypeStruct((batch, SUBLANES, w), F32),
        ],
        scratch_shapes=[
            pltpu.VMEM((ts, IN_COLS), F32), pltpu.VMEM((ts, IN_COLS), F32),
            pltpu.VMEM((ts, D_MODEL), BF16), pltpu.VMEM((ts, D_MODEL), BF16),
            pltpu.VMEM((ts, D_MODEL), BF16),
            pltpu.VMEM((SUBLANES + ts, w), F32),
            pltpu.VMEM((SUBLANES, w), F32),
            pltpu.VMEM((WINDOW + ts, KV_WIDTH), BF16),
            pltpu.VMEM((WINDOW + ts, KV_WIDTH), BF16),
            pltpu.VMEM((ts, w), F32),
            pltpu.VMEM((ts, w), F32),
            pltpu.VMEM((WINDOW, KV_WIDTH), F32),
        ],
        compiler_params=_params(("arbitrary",)),
        name="mixer_prompt",
    )(x, x, mod, mod, mod, g_mix.reshape(1, D_MODEL), w_in_b, w_out_b, cos, sin,
      p["conv_w"], p["conv_b"], p["wa"], p["ba"], p["wx"], p["bx"], p["lam"], p["sinks"])


def _mix_sample_body(proj_ref, prev_ref, h0_ref, ck_ref, cv_ref, cs_ref, sn_ref, cw_ref, cb_ref,
                     wa_ref, ba_ref, wx_ref, bx_ref, lam_ref, sink_ref,
                     mix_ref, lru_ref, knew_ref,
                     a_s, u_s, q_s, *, batch, seq):
    w = LRU_WIDTH
    row = lax.broadcasted_iota(jnp.int32, (SUBLANES, w), 0)

    def group(b):
        return pl.ds(pl.multiple_of(b * SUBLANES, SUBLANES), SUBLANES)

    def conv_body(b, carry):
        rows = group(b)
        cur = proj_ref[rows, 0:w]
        prev = prev_ref[rows, :]
        xc = cb_ref[...]
        for jj in range(CONV_W):
            d = CONV_W - 1 - jj
            if d == 0:
                term = cur
            else:
                term = jnp.where(row >= d, pltpu.roll(cur, d, 0), pltpu.roll(prev, d, 0))
            xc = xc + term * cw_ref[jj:jj + 1, :]
        u_s[rows, :] = xc
        return carry

    lax.fori_loop(0, batch, conv_body, 0, unroll=4)

    _lru_gates(u_s, a_s, wa_ref, ba_ref, wx_ref, bx_ref, lam_ref)

    def scan_body(b, carry):
        rows = group(b)
        h0 = jnp.broadcast_to(h0_ref[pl.ds(b, 1), :], (SUBLANES, w))
        h = _scan_group(a_s[rows, :], u_s[rows, :], h0, row)
        a_s[rows, :] = h
        lru_ref[pl.ds(b, 1), :] = h[SUBLANES - 1:SUBLANES, :]
        return carry

    lax.fori_loop(0, batch, scan_body, 0, unroll=4)
    mix_ref[:, 0:w] = a_s[...] * jax.nn.gelu(proj_ref[:, w:2 * w])

    cos = cs_ref[...]
    sin = sn_ref[...]
    for hd in range(N_HEADS):
        cols = slice(hd * HEAD_DIM, (hd + 1) * HEAD_DIM)
        q_s[:, cols] = _rope(proj_ref[:, Q_COL + hd * HEAD_DIM:Q_COL + (hd + 1) * HEAD_DIM], cos, sin)
    for kh in range(N_KV_HEADS):
        cols = slice(kh * HEAD_DIM, (kh + 1) * HEAD_DIM)
        knew_ref[:, cols] = _rope(proj_ref[:, K_COL + kh * HEAD_DIM:K_COL + (kh + 1) * HEAD_DIM], cos, sin)

    nq = GROUP * seq
    nk = 2 * WINDOW
    r_i = lax.broadcasted_iota(jnp.int32, (nq, nk), 0) & (seq - 1)
    c_i = lax.broadcasted_iota(jnp.int32, (nq, nk), 1)
    mask = (c_i >= r_i) & (c_i <= r_i + WINDOW)
    pad = jnp.zeros((nk - WINDOW - seq, HEAD_DIM), F32)

    def attn_body(bg, carry):
        work = [(bg * SAMPLE_ATTN_BATCHES + k, kh) for k in range(SAMPLE_ATTN_BATCHES)
                for kh in range(N_KV_HEADS)]
        scores = []
        for b, kh in work:
            rows = group(b)
            cols = slice(kh * HEAD_DIM, (kh + 1) * HEAD_DIM)
            q4 = jnp.concatenate(
                [q_s[rows, (kh * GROUP + g) * HEAD_DIM:(kh * GROUP + g + 1) * HEAD_DIM] for g in range(GROUP)],
                axis=0).astype(BF16)
            cached = pl.ds(kh, WINDOW, stride=N_KV_HEADS)
            kk = jnp.concatenate([ck_ref[b, cached, :], knew_ref[rows, cols], pad], axis=0).astype(BF16)
            scores.append(lax.dot_general(q4, kk, (((1,), (1,)), ((), ())), preferred_element_type=F32) * SCALE)
        for (b, kh), s in zip(work, scores):
            rows = group(b)
            cached = pl.ds(kh, WINDOW, stride=N_KV_HEADS)
            vv = jnp.concatenate(
                [cv_ref[b, cached, :], proj_ref[rows, V_COL + kh * HEAD_DIM:V_COL + (kh + 1) * HEAD_DIM], pad],
                axis=0).astype(BF16)
            o = _softmax_pv(s, mask, _sink_column(sink_ref, kh, seq), vv)
            for g in range(GROUP):
                hd = kh * GROUP + g
                mix_ref[rows, w + hd * HEAD_DIM:w + (hd + 1) * HEAD_DIM] = o[g * seq:(g + 1) * seq, :]
        return carry

    lax.fori_loop(0, batch // SAMPLE_ATTN_BATCHES, attn_body, 0)


def _mix_sample(proj, prev, h0, cache_k, cache_v, cos, sin, p, *, batch, seq):
    assert seq == SUBLANES, "each sample batch must be exactly one sublane group"
    assert batch % SAMPLE_ATTN_BATCHES == 0
    t = batch * seq
    body = functools.partial(_mix_sample_body, batch=batch, seq=seq)
    vmem = pl.BlockSpec(memory_space=pltpu.VMEM)
    return pl.pallas_call(
        body,
        in_specs=[vmem] * 14 + [pl.BlockSpec(memory_space=pltpu.SMEM)],
        out_specs=[vmem, vmem, vmem],
        out_shape=[
            jax.ShapeDtypeStruct((t, D_MODEL), F32),
            jax.ShapeDtypeStruct((batch, LRU_WIDTH), F32),
            jax.ShapeDtypeStruct((t, KV_WIDTH), F32),
        ],
        scratch_shapes=[
            pltpu.VMEM((t, LRU_WIDTH), F32),
            pltpu.VMEM((t, LRU_WIDTH), F32),
            pltpu.VMEM((t, ATTN_WIDTH), F32),
        ],
        compiler_params=pltpu.CompilerParams(vmem_limit_bytes=V7X_VMEM_LIMIT_BYTES),
        name="mix_sample",
    )(proj, prev, h0, cache_k, cache_v, cos, sin, p["conv_w"], p["conv_b"], p["wa"], p["ba"], p["wx"], p["bx"],
      p["lam"], p["sinks"])


def _layer(w, l):
    return w.reshape(w.shape[1:]) if w.shape[0] == 1 else w[l]


def _rope_tables(pos):
    half = HEAD_DIM // 2
    inv = ROPE_THETA ** (-jnp.arange(half, dtype=F32) / half)
    ang = pos.astype(F32)[:, None] * inv[None, :]
    cos = jnp.cos(ang)
    sin = jnp.sin(ang)
    return jnp.concatenate([cos, cos], axis=-1), jnp.concatenate([-sin, sin], axis=-1)


def kernel(x_prompt, x_sample, c_prompt, c_sample, state_conv, state_lru, cache_k_win, cache_v_win, ada_w, ada_b, norm_ffn1, norm_mix, norm_ffn2, ffn1_w_gate, ffn1_w_up, ffn1_w_down, w_in, conv_w, conv_b, lru_w_a, lru_b_a, lru_w_x, lru_b_x, lru_lambda, attn_sinks, w_out, ffn2_w_gate, ffn2_w_up, ffn2_w_down, norm_final):
    bp, sp, _ = x_prompt.shape
    bs, ss, _ = x_sample.shape
    depth = ada_w.shape[0]
    tp, tsn = bp * sp, bs * ss
    assert bp <= MOD_PROMPT_ROWS and bp + bs <= MOD_ROWS
    tm_p = 1024
    assert sp % tm_p == 0

    xp = x_prompt.reshape(tp, D_MODEL)
    xs = x_sample.reshape(tsn, D_MODEL)
    c_all = jnp.concatenate(
        [c_prompt, jnp.zeros((MOD_PROMPT_ROWS - bp, D_MODEL), F32), c_sample,
         jnp.zeros((MOD_ROWS - MOD_PROMPT_ROWS - bs, D_MODEL), F32)], axis=0)
    cos_p, sin_p = _rope_tables(jnp.arange(sp))
    cos_s, sin_s = _rope_tables(PAST_LEN + jnp.arange(ss))
    cos_s = jnp.tile(cos_s, (bs, 1))
    sin_s = jnp.tile(sin_s, (bs, 1))

    outs_p, outs_s = [], []
    for l in range(depth):
        last = l == depth - 1
        L = functools.partial(_layer, l=l)
        cache_k, cache_v = L(cache_k_win), L(cache_v_win)
        mod = _adaln(c_all, L(ada_w), L(ada_b))
        p = dict(conv_w=L(conv_w), conv_b=L(conv_b).reshape(1, LRU_WIDTH), wa=L(lru_w_a),
                 ba=L(lru_b_a).reshape(1, LRU_WIDTH), wx=L(lru_w_x), bx=L(lru_b_x).reshape(1, LRU_WIDTH),
                 lam=L(lru_lambda).reshape(1, LRU_WIDTH), sinks=L(attn_sinks))
        ffn1 = (L(norm_ffn1), L(ffn1_w_gate), L(ffn1_w_up), L(ffn1_w_down), norm_final)
        ffn2 = (L(norm_ffn2), L(ffn2_w_gate), L(ffn2_w_up), L(ffn2_w_down), norm_final)
        ffn_p = dict(tm=tm_p, nr=FFN_ROWS, row0=0, rows_per_batch=sp)
        ffn_s = dict(tm=tsn, nr=ss, row0=MOD_PROMPT_ROWS, rows_per_batch=ss)
        proj_s_kw = dict(nr=ss, row0=MOD_PROMPT_ROWS, rows_per_batch=ss)

        g, wg, wu, wd, gf = ffn1
        xs, wg_b, wu_b, wd_b = _ffn(xs, mod, (0, 1, 2), g, wg, wu, wd, gf, emit=True, final_norm=False,
                                    name="ffn1_sample", **ffn_s)
        xp, = _ffn(xp, mod, (0, 1, 2), g, wg_b, wu_b, wd_b, gf, emit=False, final_norm=False,
                   name="ffn1_prompt", **ffn_p)

        proj_s, w_in_b = _inproj(xs, mod, (3, 4), L(norm_mix), L(w_in), tn=INPROJ_TN, name="inproj_sample",
                                 **proj_s_kw)
        prev_s = jnp.pad(L(state_conv), ((0, 0), (SUBLANES - (CONV_W - 1), 0), (0, 0))).reshape(tsn, LRU_WIDTH)
        mix_s, lru_s, knew_s = _mix_sample(
            proj_s, prev_s, L(state_lru), cache_k.reshape(bs, WINDOW * N_KV_HEADS, HEAD_DIM),
            cache_v.reshape(bs, WINDOW * N_KV_HEADS, HEAD_DIM), cos_s, sin_s, p, batch=bs, seq=ss)
        xs, w_out_b = _outproj(mix_s, xs, mod, 5, L(w_out), tn=OUTPROJ_TN, name="outproj_sample", **proj_s_kw)
        xp, lru_p, k_p, v_p, conv_p = _mixer_prompt(xp, mod, L(norm_mix), w_in_b, w_out_b, cos_p, sin_p, p,
                                                    batch=bp, seq=sp)

        g, wg, wu, wd, gf = ffn2
        xp, wg_b, wu_b, wd_b = _ffn(xp, mod, (6, 7, 8), g, wg, wu, wd, gf, emit=True, tf=FFN_EMIT_TF, ntiles=1,
                                    inplace=True, final_norm=last, name="ffn2_head", **ffn_p)
        xp, = _ffn(xp, mod, (6, 7, 8), g, wg_b, wu_b, wd_b, gf, emit=False, tile0=1, ntiles=tp // tm_p - 1,
                   inplace=True, final_norm=last, name="ffn2_prompt", **ffn_p)
        xs, = _ffn(xs, mod, (6, 7, 8), g, wg_b, wu_b, wd_b, gf, emit=False, final_norm=last,
                   name="ffn2_sample", **ffn_s)

        ps = proj_s.reshape(bs, ss, IN_COLS)
        outs_p.append((
            conv_p[:, SUBLANES - (CONV_W - 1):, :],
            lru_p.reshape(bp, LRU_WIDTH),
            k_p.reshape(bp, WINDOW, N_KV_HEADS, HEAD_DIM),
            v_p.reshape(bp, WINDOW, N_KV_HEADS, HEAD_DIM),
        ))
        k_all = jnp.concatenate([cache_k, knew_s.reshape(bs, ss, N_KV_HEADS, HEAD_DIM)], axis=1)
        v_all = jnp.concatenate([cache_v, ps[:, :, V_COL:].reshape(bs, ss, N_KV_HEADS, HEAD_DIM)], axis=1)
        outs_s.append((
            ps[:, ss - (CONV_W - 1):, 0:LRU_WIDTH],
            lru_s,
            k_all[:, -WINDOW:],
            v_all[:, -WINDOW:],
        ))

    y_prompt = xp.reshape(bp, sp, D_MODEL)
    y_sample = xs.reshape(bs, ss, D_MODEL)
    st_p = [jnp.stack([o[i] for o in outs_p]) for i in range(4)]
    st_s = [jnp.stack([o[i] for o in outs_s]) for i in range(4)]
    return (y_prompt, y_sample, st_p[0], st_p[1], st_p[2], st_p[3], st_s[0], st_s[1], st_s[2], st_s[3])
```

```python
import functools

import jax
import jax.numpy as jnp
from jax import lax
from jax.experimental import pallas as pl
from jax.experimental.pallas import tpu as pltpu

F32 = jnp.float32
BF16 = jnp.bfloat16

D_MODEL = 2048
D_FF = 5504
LRU_WIDTH = 1024
LRU_HEADS = 8
LRU_BLOCK = 128
CONV_W = 4
LRU_C = 8.0
HEAD_DIM = 128
N_HEADS = 8
N_KV_HEADS = 2
GROUP = N_HEADS // N_KV_HEADS
ATTN_WIDTH = N_HEADS * HEAD_DIM
KV_WIDTH = N_KV_HEADS * HEAD_DIM
WINDOW = 128
ROPE_THETA = 10000.0
N_MOD = 9
EPS = 1e-6
NEG = -1e30
PAST_LEN = 16384
IN_COLS = 2 * LRU_WIDTH + ATTN_WIDTH + 2 * KV_WIDTH
Q_COL = 2 * LRU_WIDTH
K_COL = Q_COL + ATTN_WIDTH
V_COL = K_COL + KV_WIDTH
SCALE = HEAD_DIM ** -0.5

SUBLANES = 8
LANES = 128
V7X_VMEM_LIMIT_BYTES = 56 * 1024 * 1024

MOD_ROWS = 48
MOD_PROMPT_ROWS = 8
ADALN_TN = 2048
BF16_ROWS = 16
FFN_TF = 512
FFN_EMIT_TF = 256
FFN_ROWS = 16
FFN_EPILOGUE_ROWS = 128
FFN_ROW_BLOCK = 256
FFN_FINISH_ROWS = 32
INPROJ_TN = IN_COLS // 2
OUTPROJ_TN = D_MODEL // 2
SAMPLE_ATTN_BATCHES = 4
MIX_TS = 256
MIXER_TN = 256


def _params(sem, vmem=V7X_VMEM_LIMIT_BYTES):
    return pltpu.CompilerParams(dimension_semantics=sem, vmem_limit_bytes=vmem)


def _sigmoid(x):
    return 1.0 / (1.0 + jnp.exp(-x))


def _normmod(x, g, sc, sh):
    ms = jnp.mean(x * x, axis=-1, keepdims=True)
    xn = x * lax.rsqrt(ms + EPS) * g
    return xn * (1.0 + sc) + sh


def _mod_row(ref, token, row0, batch_shift):
    return ref[pl.ds(row0 + lax.shift_right_logical(token, batch_shift), 1), :]


def _mod_spec(chunk):
    return pl.BlockSpec((MOD_ROWS, D_MODEL), lambda i, j: (0, chunk))


def _adaln_body(c_ref, w_ref, b_ref, o_ref):
    c = c_ref[...]
    s = (c * _sigmoid(c)).astype(BF16)
    o_ref[...] = jnp.dot(s, w_ref[...].astype(BF16), preferred_element_type=F32) + b_ref[...]


def _adaln(c_all, ada_w, ada_b):
    n = ada_w.shape[1]
    return pl.pallas_call(
        _adaln_body,
        grid=(n // ADALN_TN,),
        in_specs=[
            pl.BlockSpec((MOD_ROWS, D_MODEL), lambda j: (0, 0)),
            pl.BlockSpec((D_MODEL, ADALN_TN), lambda j: (0, j)),
            pl.BlockSpec((1, ADALN_TN), lambda j: (0, j)),
        ],
        out_specs=pl.BlockSpec((MOD_ROWS, ADALN_TN), lambda j: (0, j)),
        out_shape=jax.ShapeDtypeStruct((MOD_ROWS, n), F32),
        compiler_params=_params(("arbitrary",)),
        name="adaln",
    )(c_all, ada_w, ada_b.reshape(1, n))


def _ffn_body(x_ref, sh_ref, sc_ref, gt_ref, g_ref, wg_ref, wu_ref, wd_ref, gf_ref, *rest,
              nr, er, rb, row0, batch_shift, tile0, final_norm, emit):
    if emit:
        o_ref, wgb_ref, wub_ref, wdb_ref, h_ref = rest
    else:
        o_ref, h_ref = rest
    i = pl.program_id(0) + tile0
    j = pl.program_id(1)
    nj = pl.num_programs(1)
    tm = x_ref.shape[0]
    tf = wd_ref.shape[0]

    if emit:
        valid = D_FF - j * tf
        cmask = lax.broadcasted_iota(jnp.int32, (1, tf), 1) < valid
        rmask = lax.broadcasted_iota(jnp.int32, (tf, 1), 0) < valid
        wgb_ref[...] = jnp.where(cmask, wg_ref[...], 0.0).astype(BF16)
        wub_ref[...] = jnp.where(cmask, wu_ref[...], 0.0).astype(BF16)
        wdb_ref[...] = jnp.where(rmask, wd_ref[...], 0.0).astype(BF16)
        wg_ref, wu_ref, wd_ref = wgb_ref, wub_ref, wdb_ref

    def swiglu(rows):
        h = h_ref[rows, :].astype(BF16)
        g = jnp.dot(h, wg_ref[...], preferred_element_type=F32)
        u = jnp.dot(h, wu_ref[...], preferred_element_type=F32)
        a = (g * _sigmoid(g) * u).astype(BF16)
        return jnp.dot(a, wd_ref[...], preferred_element_type=F32)

    def prologue(rows, token):
        h = _normmod(x_ref[rows, :], g_ref[...], _mod_row(sc_ref, token, row0, batch_shift),
                     _mod_row(sh_ref, token, row0, batch_shift))
        h_ref[rows, :] = h.astype(h_ref.dtype)

    def epilogue(rows, token, acc):
        y = x_ref[rows, :] + (0.5 * _mod_row(gt_ref, token, row0, batch_shift)) * acc
        if final_norm:
            ms = jnp.mean(y * y, axis=-1, keepdims=True)
            y = y * lax.rsqrt(ms + EPS) * gf_ref[...]
        o_ref[rows, :] = y

    if rb:
        blocks = range(0, tm, rb)

        def block_prologue(r0):
            for c in range(r0, r0 + rb, nr):
                prologue(slice(c, c + nr), i * tm + c)

        @pl.when(j == 0)
        def _():
            block_prologue(0)
            for r0 in blocks:
                o_ref[r0:r0 + rb, :] = swiglu(slice(r0, r0 + rb))
                if r0 + rb < tm:
                    block_prologue(r0 + rb)

        @pl.when((j > 0) & (j < nj - 1))
        def _():
            o_ref[...] += swiglu(slice(None))

        @pl.when(j == nj - 1)
        def _():
            for r0 in blocks:
                acc = o_ref[r0:r0 + rb, :] + swiglu(slice(r0, r0 + rb))
                for c in range(0, rb, FFN_FINISH_ROWS):
                    epilogue(slice(r0 + c, r0 + c + FFN_FINISH_ROWS), i * tm + r0 + c,
                             acc[c:c + FFN_FINISH_ROWS, :])
    else:
        @pl.when(j == 0)
        def _():
            def body(r, carry):
                rows = pl.ds(pl.multiple_of(r * nr, nr), nr)
                prologue(rows, i * tm + r * nr)
                o_ref[rows, :] = jnp.zeros((nr, D_MODEL), F32)
                return carry

            lax.fori_loop(0, tm // nr, body, 0, unroll=4)

        o_ref[...] += swiglu(slice(None))

        @pl.when(j == nj - 1)
        def _():
            def body(r, carry):
                rows = pl.ds(pl.multiple_of(r * er, er), er)
                epilogue(rows, i * tm + r * er, o_ref[rows, :])
                return carry

            lax.fori_loop(0, tm // er, body, 0)


def _ffn(x, mod, chunks, g, wg, wu, wd, gf, *, tm, nr, row0, rows_per_batch, final_norm, emit, name,
         tf=FFN_TF, tile0=0, ntiles=None, inplace=False):
    t = x.shape[0]
    ntiles = t // tm if ntiles is None else ntiles
    assert tf == FFN_TF or emit
    sub = FFN_TF // tf
    nj_store = pl.cdiv(D_FF, FFN_TF)
    nj = nj_store * sub
    er = min(FFN_EPILOGUE_ROWS, rows_per_batch)
    assert rows_per_batch & (rows_per_batch - 1) == 0 and rows_per_batch % nr == 0 and tm % nr == 0
    assert rows_per_batch % er == 0 and tm % er == 0
    rb = FFN_ROW_BLOCK if rows_per_batch % FFN_ROW_BLOCK == 0 and tm % FFN_ROW_BLOCK == 0 else 0
    body = functools.partial(_ffn_body, nr=nr, er=er, rb=rb, row0=row0,
                             batch_shift=rows_per_batch.bit_length() - 1, tile0=tile0,
                             final_norm=final_norm, emit=emit)

    row = pl.BlockSpec((1, D_MODEL), lambda i, j: (0, 0))
    tile_major = pl.BlockSpec((None, D_MODEL, tf), lambda i, j: (j // sub, 0, j % sub))
    if emit:
        assert ntiles == 1, "bf16 weight tiles are written once, by a single token tile"
        wspecs = [pl.BlockSpec((D_MODEL, tf), lambda i, j: (0, j)),
                  pl.BlockSpec((D_MODEL, tf), lambda i, j: (0, j)),
                  pl.BlockSpec((tf, D_MODEL), lambda i, j: (j, 0))]
    else:
        wspecs = [tile_major, tile_major, pl.BlockSpec((tf, D_MODEL), lambda i, j: (j, 0))]
    xmode = dict(pipeline_mode=pl.Buffered(1)) if ntiles == 1 else {}
    in_specs = [pl.BlockSpec((tm, D_MODEL), lambda i, j: (i + tile0, 0), **xmode),
                _mod_spec(chunks[0]), _mod_spec(chunks[1]), _mod_spec(chunks[2]), row] + wspecs + [row]
    args = [x, mod, mod, mod, g.reshape(1, D_MODEL), wg, wu, wd, gf.reshape(1, D_MODEL)]
    out_specs = [pl.BlockSpec((tm, D_MODEL), lambda i, j: (i + tile0, 0))]
    out_shape = [jax.ShapeDtypeStruct((t, D_MODEL), F32)]
    if emit:
        out_specs += [tile_major, tile_major, pl.BlockSpec((tf, D_MODEL), lambda i, j: (j, 0))]
        out_shape += [jax.ShapeDtypeStruct((nj_store, D_MODEL, FFN_TF), BF16),
                      jax.ShapeDtypeStruct((nj_store, D_MODEL, FFN_TF), BF16),
                      jax.ShapeDtypeStruct((nj_store * FFN_TF, D_MODEL), BF16)]
    return pl.pallas_call(
        body,
        grid=(ntiles, nj),
        in_specs=in_specs,
        out_specs=out_specs,
        out_shape=out_shape,
        input_output_aliases={0: 0} if inplace else {},
        scratch_shapes=[pltpu.VMEM((tm, D_MODEL), BF16 if nr % BF16_ROWS == 0 else F32)],
        compiler_params=_params(("parallel", "arbitrary")),
        name=name,
    )(*args)


def _inproj_body(x_ref, sh_ref, sc_ref, g_ref, w_ref, o_ref, wb_ref, h_ref, *, nr, row0, batch_shift):
    j = pl.program_id(1)
    tm = x_ref.shape[0]

    @pl.when(j == 0)
    def _():
        g = g_ref[...]

        def body(r, carry):
            rows = pl.ds(pl.multiple_of(r * nr, nr), nr)
            tok = r * nr
            h = _normmod(x_ref[rows, :], g, _mod_row(sc_ref, tok, row0, batch_shift),
                         _mod_row(sh_ref, tok, row0, batch_shift))
            h_ref[rows, :] = h
            return carry

        lax.fori_loop(0, tm // nr, body, 0, unroll=4)

    w = w_ref[...].astype(BF16)
    wb_ref[...] = w
    o_ref[...] = jnp.dot(h_ref[...].astype(BF16), w, preferred_element_type=F32)


def _inproj(x, mod, chunks, g, w_in, *, tn, nr, row0, rows_per_batch, name):
    tm = x.shape[0]
    assert rows_per_batch & (rows_per_batch - 1) == 0 and rows_per_batch % nr == 0 and tm % nr == 0
    body = functools.partial(_inproj_body, nr=nr, row0=row0, batch_shift=rows_per_batch.bit_length() - 1)
    wspec = pl.BlockSpec((D_MODEL, tn), lambda i, j: (0, j))
    return pl.pallas_call(
        body,
        grid=(1, IN_COLS // tn),
        in_specs=[
            pl.BlockSpec((tm, D_MODEL), lambda i, j: (0, 0)),
            _mod_spec(chunks[0]), _mod_spec(chunks[1]),
            pl.BlockSpec((1, D_MODEL), lambda i, j: (0, 0)),
            wspec,
        ],
        out_specs=[pl.BlockSpec((tm, tn), lambda i, j: (0, j)), wspec],
        out_shape=[jax.ShapeDtypeStruct((tm, IN_COLS), F32), jax.ShapeDtypeStruct((D_MODEL, IN_COLS), BF16)],
        scratch_shapes=[pltpu.VMEM((tm, D_MODEL), F32)],
        compiler_params=_params(("arbitrary", "arbitrary")),
        name=name,
    )(x, mod, mod, g.reshape(1, D_MODEL), w_in)


def _outproj_body(m_ref, x_ref, gt_ref, w_ref, o_ref, wb_ref, *, nr, row0, batch_shift):
    tm = x_ref.shape[0]
    w = w_ref[...].astype(BF16)
    wb_ref[...] = w
    o_ref[...] = jnp.dot(m_ref[...].astype(BF16), w, preferred_element_type=F32)

    def body(r, carry):
        rows = pl.ds(pl.multiple_of(r * nr, nr), nr)
        gt = _mod_row(gt_ref, r * nr, row0, batch_shift)
        o_ref[rows, :] = x_ref[rows, :] + gt * o_ref[rows, :]
        return carry

    lax.fori_loop(0, tm // nr, body, 0)


def _outproj(mix, x, mod, gt_c, w_out, *, tn, nr, row0, rows_per_batch, name):
    tm = x.shape[0]
    nn = D_MODEL // tn
    assert rows_per_batch & (rows_per_batch - 1) == 0 and rows_per_batch % nr == 0 and tm % nr == 0
    body = functools.partial(_outproj_body, nr=nr, row0=row0, batch_shift=rows_per_batch.bit_length() - 1)
    wspec = pl.BlockSpec((D_MODEL, tn), lambda i, j: (0, j))
    tile = pl.BlockSpec((tm, tn), lambda i, j: (0, j))
    return pl.pallas_call(
        body,
        grid=(1, nn),
        in_specs=[
            pl.BlockSpec((tm, D_MODEL), lambda i, j: (0, 0)),
            tile,
            pl.BlockSpec((MOD_ROWS, tn), lambda i, j: (0, gt_c * nn + j)),
            wspec,
        ],
        out_specs=[tile, wspec],
        out_shape=[jax.ShapeDtypeStruct((tm, D_MODEL), F32), jax.ShapeDtypeStruct((D_MODEL, D_MODEL), BF16)],
        compiler_params=_params(("arbitrary", "arbitrary")),
        name=name,
    )(mix, x, mod, w_out)


def _rope(x, cos, sin_signed):
    return x * cos + pltpu.roll(x, HEAD_DIM // 2, 1) * sin_signed


def _lru_gates(xc_ref, a_ref, wa_ref, ba_ref, wx_ref, bx_ref, lam_ref, midway=None):
    nlam = -lam_ref[...]
    softplus = jnp.maximum(nlam, 0.0) + jnp.log1p(jnp.exp(-jnp.abs(nlam)))
    rate = -LRU_C * softplus
    for hh in range(LRU_HEADS):
        if midway is not None:
            midway()
        cols = slice(hh * LRU_BLOCK, (hh + 1) * LRU_BLOCK)
        xc = xc_ref[:, cols]
        xcb = xc.astype(BF16)
        ra = jnp.dot(xcb, wa_ref[hh].astype(BF16), preferred_element_type=F32) + ba_ref[:, cols]
        rx = jnp.dot(xcb, wx_ref[hh].astype(BF16), preferred_element_type=F32) + bx_ref[:, cols]
        r = _sigmoid(ra)
        gi = _sigmoid(rx)
        a = jnp.exp(r * rate[:, cols])
        a_ref[:, cols] = a
        xc_ref[:, cols] = jnp.sqrt(1.0 - a * a) * (gi * xc)


def _scan_group(a, u, carry, row):
    for s in (1, 2, 4):
        a_sh = pltpu.roll(a, s, 0)
        u_sh = pltpu.roll(u, s, 0)
        m = row >= s
        u = jnp.where(m, a * u_sh + u, u)
        a = jnp.where(m, a * a_sh, a)
    return a * carry + u


def _softmax_pv(s, mask, sink, v):
    s = jnp.where(mask, s, NEG)
    m = jnp.maximum(jnp.max(s, axis=-1, keepdims=True), sink)
    p = jnp.exp(s - m)
    den = jnp.sum(p, axis=-1, keepdims=True) + jnp.exp(sink - m)
    return jnp.dot(p.astype(BF16), v, preferred_element_type=F32) / den


def _sink_column(sink_ref, kh, rows_per_head):
    n = GROUP * rows_per_head
    ri = lax.broadcasted_iota(jnp.int32, (n, 1), 0)
    col = jnp.full((n, 1), sink_ref[kh * GROUP + GROUP - 1], F32)
    for g in range(GROUP - 2, -1, -1):
        col = jnp.where(ri < (g + 1) * rows_per_head, sink_ref[kh * GROUP + g], col)
    return col


def _mix_tile(proj, mix, cs_ref, sn_ref, cw_ref, cb_ref, wa_ref, ba_ref, wx_ref, bx_ref, lam_ref, sink_ref,
              xpad, hc, kpad, vpad, a_s, u_s, klast, *, ts, pos0, mxu_fill):
    fill = iter(mxu_fill)

    def take(n):
        for _ in range(n):
            f = next(fill, None)
            if f is not None:
                f()

    w = LRU_WIDTH
    xpad[SUBLANES:SUBLANES + ts, :] = proj[:, 0:w]
    half = ts // 2
    for r0 in (0, half):
        take(1)
        xc = cb_ref[...]
        for jj in range(CONV_W):
            off = r0 + SUBLANES - (CONV_W - 1) + jj
            xc = xc + xpad[off:off + half, :] * cw_ref[jj:jj + 1, :]
        u_s[r0:r0 + half, :] = xc
    xpad[0:SUBLANES, :] = xpad[ts:ts + SUBLANES, :]

    _lru_gates(u_s, a_s, wa_ref, ba_ref, wx_ref, bx_ref, lam_ref, midway=lambda: take(1))

    row = lax.broadcasted_iota(jnp.int32, (SUBLANES, w), 0)
    carry = hc[...]
    ngroups = ts // SUBLANES
    for g in range(ngroups):
        if g % (ngroups // 8) == 0:
            take(1)
        rows = slice(g * SUBLANES, (g + 1) * SUBLANES)
        h = _scan_group(a_s[rows, :], u_s[rows, :], carry, row)
        a_s[rows, :] = h
        carry = jnp.broadcast_to(h[SUBLANES - 1:SUBLANES, :], (SUBLANES, w))
    hc[...] = carry
    for r0 in (0, half):
        take(1)
        rows = slice(r0, r0 + half)
        mix[rows, 0:w] = (a_s[rows, :] * jax.nn.gelu(proj[rows, w:2 * w])).astype(BF16)

    take(1)
    cos = cs_ref[...]
    sin = sn_ref[...]
    for kh in range(N_KV_HEADS):
        cols = slice(kh * HEAD_DIM, (kh + 1) * HEAD_DIM)
        kr = _rope(proj[:, K_COL + kh * HEAD_DIM:K_COL + (kh + 1) * HEAD_DIM], cos, sin)
        kpad[WINDOW:WINDOW + ts, cols] = kr.astype(BF16)
        klast[:, cols] = kr[ts - WINDOW:, :]
    vpad[WINDOW:WINDOW + ts, :] = proj[:, V_COL:V_COL + KV_WIDTH].astype(BF16)

    nq = GROUP * WINDOW
    r_i = lax.broadcasted_iota(jnp.int32, (nq, 2 * WINDOW), 0) & (WINDOW - 1)
    c_i = lax.broadcasted_iota(jnp.int32, (nq, 2 * WINDOW), 1)
    band = (c_i >= r_i) & (c_i <= r_i + WINDOW)
    for n in range(ts // WINDOW):
        qrows = slice(n * WINDOW, (n + 1) * WINDOW)
        kpos0 = pos0 + (n - 1) * WINDOW
        mask = band & (c_i + kpos0 >= 0)
        for kh in range(N_KV_HEADS):
            cols = slice(kh * HEAD_DIM, (kh + 1) * HEAD_DIM)
            qs = []
            for g in range(GROUP):
                hd = kh * GROUP + g
                qh = proj[qrows, Q_COL + hd * HEAD_DIM:Q_COL + (hd + 1) * HEAD_DIM]
                qs.append(_rope(qh, cos[qrows, :], sin[qrows, :]).astype(BF16))
            q4 = jnp.concatenate(qs, axis=0)
            kk = kpad[n * WINDOW:(n + 2) * WINDOW, cols]
            vv = vpad[n * WINDOW:(n + 2) * WINDOW, cols]
            s = lax.dot_general(q4, kk, (((1,), (1,)), ((), ())), preferred_element_type=F32) * SCALE
            take(1)
            o = _softmax_pv(s, mask, _sink_column(sink_ref, kh, WINDOW), vv)
            for g in range(GROUP):
                hd = kh * GROUP + g
                mix[qrows, w + hd * HEAD_DIM:w + (hd + 1) * HEAD_DIM] = (
                    o[g * WINDOW:(g + 1) * WINDOW, :].astype(BF16))

    take(len(mxu_fill))
    kpad[0:WINDOW, :] = kpad[ts:ts + WINDOW, :]
    vpad[0:WINDOW, :] = vpad[ts:ts + WINDOW, :]


def _mixer_body(xa_ref, xc_ref, sh_ref, sc_ref, gt_ref, gm_ref, win_ref, wout_ref, cs_ref, sn_ref,
                cw_ref, cb_ref, wa_ref, ba_ref, wx_ref, bx_ref, lam_ref, sink_ref,
                o_ref, lru_ref, k_ref, v_ref, conv_ref,
                proj0, proj1, mix0, mix1, h_s, xpad, hc, kpad, vpad, a_s, u_s, klast,
                *, ts, nt, ntiles, batch_shift):
    s = pl.program_id(0)
    tile_a = jnp.minimum(s, ntiles - 1)
    tile_b = jnp.clip(s - 1, 0, ntiles - 1)
    tile_c = jnp.clip(s - 2, 0, ntiles - 1)
    t_in = tile_b & (nt - 1)
    proj = (proj0, proj1)
    mix = (mix0, mix1)

    def out_chunks(mix_r):
        gt = _mod_row(gt_ref, tile_c * ts, 0, batch_shift)

        def chunk(k):
            cols = slice(k * MIXER_TN, (k + 1) * MIXER_TN)

            def run():
                d = jnp.dot(mix_r[...], wout_ref[:, cols], preferred_element_type=F32)
                o_ref[:, cols] = xc_ref[:, cols] + gt[:, cols] * d
            return run
        return [chunk(k) for k in range(D_MODEL // MIXER_TN)]

    def in_chunks(proj_w):
        def chunk(k):
            cols = slice(k * MIXER_TN, (k + 1) * MIXER_TN)

            def run():
                if k == 0:
                    g = gm_ref[...]
                    sc = _mod_row(sc_ref, tile_a * ts, 0, batch_shift)
                    sh = _mod_row(sh_ref, tile_a * ts, 0, batch_shift)
                    for r in range(ts // FFN_ROWS):
                        rows = slice(r * FFN_ROWS, (r + 1) * FFN_ROWS)
                        h_s[rows, :] = _normmod(xa_ref[rows, :], g, sc, sh).astype(BF16)
                proj_w[:, cols] = jnp.dot(h_s[...], win_ref[:, cols], preferred_element_type=F32)
            return run
        return [chunk(k) for k in range(IN_COLS // MIXER_TN)]

    @pl.when(s == 0)
    def _():
        mix1[...] = jnp.zeros(mix1.shape, BF16)
        for run in in_chunks(proj[0]):
            run()

    @pl.when(s == ntiles + 1)
    def _():
        for run in out_chunks(mix[(ntiles - 1) & 1]):
            run()

    @pl.when(t_in == 0)
    def _():
        xpad[0:SUBLANES, :] = jnp.zeros((SUBLANES, LRU_WIDTH), F32)
        hc[...] = jnp.zeros((SUBLANES, LRU_WIDTH), F32)
        kpad[0:WINDOW, :] = jnp.zeros((WINDOW, KV_WIDTH), BF16)
        vpad[0:WINDOW, :] = jnp.zeros((WINDOW, KV_WIDTH), BF16)

    def step(par):
        proj_r, mix_w = proj[1 - par], mix[1 - par]
        _mix_tile(proj_r, mix_w, cs_ref, sn_ref, cw_ref, cb_ref, wa_ref, ba_ref, wx_ref, bx_ref, lam_ref,
                  sink_ref, xpad, hc, kpad, vpad, a_s, u_s, klast, ts=ts, pos0=t_in * ts,
                  mxu_fill=in_chunks(proj[par]) + out_chunks(mix[par]))
        lru_ref[0] = hc[0:1, :]
        k_ref[0] = klast[...]
        v_ref[0] = proj_r[ts - WINDOW:, V_COL:V_COL + KV_WIDTH]
        conv_ref[0] = xpad[0:SUBLANES, :]

    for par in (0, 1):
        @pl.when((s >= 1) & (s <= ntiles) & ((s & 1) == par))
        def _(par=par):
            step(par)


def _mixer_prompt(x, mod, g_mix, w_in_b, w_out_b, cos, sin, p, *, batch, seq):
    ts = MIX_TS
    nt = seq // ts
    ntiles = batch * nt
    assert nt & (nt - 1) == 0 and seq & (seq - 1) == 0
    w = LRU_WIDTH
    body = functools.partial(_mixer_body, ts=ts, nt=nt, ntiles=ntiles, batch_shift=seq.bit_length() - 1)

    def tile_a(s):
        return jnp.minimum(s, ntiles - 1)

    def tile_b(s):
        return jnp.clip(s - 1, 0, ntiles - 1)

    def tile_c(s):
        return jnp.clip(s - 2, 0, ntiles - 1)

    def whole(shape, **kw):
        return pl.BlockSpec(shape, lambda s: (0,) * len(shape), **kw)

    def mspec(c):
        return pl.BlockSpec((MOD_ROWS, D_MODEL), lambda s: (0, c))

    def state(shape):
        return pl.BlockSpec((1,) + shape, lambda s: (tile_b(s) // nt, 0, 0))

    once = dict(pipeline_mode=pl.Buffered(1))
    return pl.pallas_call(
        body,
        grid=(ntiles + 2,),
        in_specs=[
            pl.BlockSpec((ts, D_MODEL), lambda s: (tile_a(s), 0)),
            pl.BlockSpec((ts, D_MODEL), lambda s: (tile_c(s), 0)),
            mspec(3), mspec(4), mspec(5),
            whole((1, D_MODEL)),
            whole((D_MODEL, IN_COLS), **once),
            whole((D_MODEL, D_MODEL), **once),
            pl.BlockSpec((ts, HEAD_DIM), lambda s: (tile_b(s) % nt, 0)),
            pl.BlockSpec((ts, HEAD_DIM), lambda s: (tile_b(s) % nt, 0)),
            whole((CONV_W, w)), whole((1, w)),
            whole((LRU_HEADS, LRU_BLOCK, LRU_BLOCK)), whole((1, w)),
            whole((LRU_HEADS, LRU_BLOCK, LRU_BLOCK)), whole((1, w)),
            whole((1, w)),
            pl.BlockSpec(memory_space=pltpu.SMEM),
        ],
        out_specs=[
            pl.BlockSpec((ts, D_MODEL), lambda s: (tile_c(s), 0)),
            state((1, w)), state((WINDOW, KV_WIDTH)), state((WINDOW, KV_WIDTH)), state((SUBLANES, w)),
        ],
        out_shape=[
            jax.ShapeDtypeStruct((batch * seq, D_MODEL), F32),
            jax.ShapeDtypeStruct((batch, 1, w), F32),
            jax.ShapeDtypeStruct((batch, WINDOW, KV_WIDTH), F32),
            jax.ShapeDtypeStruct((batch, WINDOW, KV_WIDTH), F32),
            jax.ShapeDtypeStruct((batch, SUBLANES, w), F32),
        ],
        scratch_shapes=[
            pltpu.VMEM((ts, IN_COLS), F32), pltpu.VMEM((ts, IN_COLS), F32),
            pltpu.VMEM((ts, D_MODEL), BF16), pltpu.VMEM((ts, D_MODEL), BF16),
            pltpu.VMEM((ts, D_MODEL), BF16),
            pltpu.VMEM((SUBLANES + ts, w), F32),
            pltpu.VMEM((SUBLANES, w), F32),
            pltpu.VMEM((WINDOW + ts, KV_WIDTH), BF16),
            pltpu.VMEM((WINDOW + ts, KV_WIDTH), BF16),
            pltpu.VMEM((ts, w), F32),
            pltpu.VMEM((ts, w), F32),
            pltpu.VMEM((WINDOW, KV_WIDTH), F32),
        ],
        compiler_params=_params(("arbitrary",)),
        name="mixer_prompt",
    )(x, x, mod, mod, mod, g_mix.reshape(1, D_MODEL), w_in_b, w_out_b, cos, sin,
      p["conv_w"], p["conv_b"], p["wa"], p["ba"], p["wx"], p["bx"], p["lam"], p["sinks"])


def _mix_sample_body(proj_ref, prev_ref, h0_ref, ck_ref, cv_ref, cs_ref, sn_ref, cw_ref, cb_ref,
                     wa_ref, ba_ref, wx_ref, bx_ref, lam_ref, sink_ref,
                     mix_ref, lru_ref, knew_ref,
                     a_s, u_s, q_s, *, batch, seq):
    w = LRU_WIDTH
    row = lax.broadcasted_iota(jnp.int32, (SUBLANES, w), 0)

    def group(b):
        return pl.ds(pl.multiple_of(b * SUBLANES, SUBLANES), SUBLANES)

    def conv_body(b, carry):
        rows = group(b)
        cur = proj_ref[rows, 0:w]
        prev = prev_ref[rows, :]
        xc = cb_ref[...]
        for jj in range(CONV_W):
            d = CONV_W - 1 - jj
            if d == 0:
                term = cur
            else:
                term = jnp.where(row >= d, pltpu.roll(cur, d, 0), pltpu.roll(prev, d, 0))
            xc = xc + term * cw_ref[jj:jj + 1, :]
        u_s[rows, :] = xc
        return carry

    lax.fori_loop(0, batch, conv_body, 0, unroll=4)

    _lru_gates(u_s, a_s, wa_ref, ba_ref, wx_ref, bx_ref, lam_ref)

    def scan_body(b, carry):
        rows = group(b)
        h0 = jnp.broadcast_to(h0_ref[pl.ds(b, 1), :], (SUBLANES, w))
        h = _scan_group(a_s[rows, :], u_s[rows, :], h0, row)
        a_s[rows, :] = h
        lru_ref[pl.ds(b, 1), :] = h[SUBLANES - 1:SUBLANES, :]
        return carry

    lax.fori_loop(0, batch, scan_body, 0, unroll=4)
    mix_ref[:, 0:w] = a_s[...] * jax.nn.gelu(proj_ref[:, w:2 * w])

    cos = cs_ref[...]
    sin = sn_ref[...]
    for hd in range(N_HEADS):
        cols = slice(hd * HEAD_DIM, (hd + 1) * HEAD_DIM)
        q_s[:, cols] = _rope(proj_ref[:, Q_COL + hd * HEAD_DIM:Q_COL + (hd + 1) * HEAD_DIM], cos, sin)
    for kh in range(N_KV_HEADS):
        cols = slice(kh * HEAD_DIM, (kh + 1) * HEAD_DIM)
        knew_ref[:, cols] = _rope(proj_ref[:, K_COL + kh * HEAD_DIM:K_COL + (kh + 1) * HEAD_DIM], cos, sin)

    nq = GROUP * seq
    nk = 2 * WINDOW
    r_i = lax.broadcasted_iota(jnp.int32, (nq, nk), 0) & (seq - 1)
    c_i = lax.broadcasted_iota(jnp.int32, (nq, nk), 1)
    mask = (c_i >= r_i) & (c_i <= r_i + WINDOW)
    pad = jnp.zeros((nk - WINDOW - seq, HEAD_DIM), F32)

    def attn_body(bg, carry):
        work = [(bg * SAMPLE_ATTN_BATCHES + k, kh) for k in range(SAMPLE_ATTN_BATCHES)
                for kh in range(N_KV_HEADS)]
        scores = []
        for b, kh in work:
            rows = group(b)
            cols = slice(kh * HEAD_DIM, (kh + 1) * HEAD_DIM)
            q4 = jnp.concatenate(
                [q_s[rows, (kh * GROUP + g) * HEAD_DIM:(kh * GROUP + g + 1) * HEAD_DIM] for g in range(GROUP)],
                axis=0).astype(BF16)
            cached = pl.ds(kh, WINDOW, stride=N_KV_HEADS)
            kk = jnp.concatenate([ck_ref[b, cached, :], knew_ref[rows, cols], pad], axis=0).astype(BF16)
            scores.append(lax.dot_general(q4, kk, (((1,), (1,)), ((), ())), preferred_element_type=F32) * SCALE)
        for (b, kh), s in zip(work, scores):
            rows = group(b)
            cached = pl.ds(kh, WINDOW, stride=N_KV_HEADS)
            vv = jnp.concatenate(
                [cv_ref[b, cached, :], proj_ref[rows, V_COL + kh * HEAD_DIM:V_COL + (kh + 1) * HEAD_DIM], pad],
                axis=0).astype(BF16)
            o = _softmax_pv(s, mask, _sink_column(sink_ref, kh, seq), vv)
            for g in range(GROUP):
                hd = kh * GROUP + g
                mix_ref[rows, w + hd * HEAD_DIM:w + (hd + 1) * HEAD_DIM] = o[g * seq:(g + 1) * seq, :]
        return carry

    lax.fori_loop(0, batch // SAMPLE_ATTN_BATCHES, attn_body, 0)


def _mix_sample(proj, prev, h0, cache_k, cache_v, cos, sin, p, *, batch, seq):
    assert seq == SUBLANES, "each sample batch must be exactly one sublane group"
    assert batch % SAMPLE_ATTN_BATCHES == 0
    t = batch * seq
    body = functools.partial(_mix_sample_body, batch=batch, seq=seq)
    vmem = pl.BlockSpec(memory_space=pltpu.VMEM)
    return pl.pallas_call(
        body,
        in_specs=[vmem] * 14 + [pl.BlockSpec(memory_space=pltpu.SMEM)],
        out_specs=[vmem, vmem, vmem],
        out_shape=[
            jax.ShapeDtypeStruct((t, D_MODEL), F32),
            jax.ShapeDtypeStruct((batch, LRU_WIDTH), F32),
            jax.ShapeDtypeStruct((t, KV_WIDTH), F32),
        ],
        scratch_shapes=[
            pltpu.VMEM((t, LRU_WIDTH), F32),
            pltpu.VMEM((t, LRU_WIDTH), F32),
            pltpu.VMEM((t, ATTN_WIDTH), F32),
        ],
        compiler_params=pltpu.CompilerParams(vmem_limit_bytes=V7X_VMEM_LIMIT_BYTES),
        name="mix_sample",
    )(proj, prev, h0, cache_k, cache_v, cos, sin, p["conv_w"], p["conv_b"], p["wa"], p["ba"], p["wx"], p["bx"],
      p["lam"], p["sinks"])


def _layer(w, l):
    return w.reshape(w.shape[1:]) if w.shape[0] == 1 else w[l]


def _rope_tables(pos):
    half = HEAD_DIM // 2
    inv = ROPE_THETA ** (-jnp.arange(half, dtype=F32) / half)
    ang = pos.astype(F32)[:, None] * inv[None, :]
    cos = jnp.cos(ang)
    sin = jnp.sin(ang)
    return jnp.concatenate([cos, cos], axis=-1), jnp.concatenate([-sin, sin], axis=-1)


def kernel(x_prompt, x_sample, c_prompt, c_sample, state_conv, state_lru, cache_k_win, cache_v_win, ada_w, ada_b, norm_ffn1, norm_mix, norm_ffn2, ffn1_w_gate, ffn1_w_up, ffn1_w_down, w_in, conv_w, conv_b, lru_w_a, lru_b_a, lru_w_x, lru_b_x, lru_lambda, attn_sinks, w_out, ffn2_w_gate, ffn2_w_up, ffn2_w_down, norm_final):
    bp, sp, _ = x_prompt.shape
    bs, ss, _ = x_sample.shape
    depth = ada_w.shape[0]
    tp, tsn = bp * sp, bs * ss
    assert bp <= MOD_PROMPT_ROWS and bp + bs <= MOD_ROWS
    tm_p = 1024
    assert sp % tm_p == 0

    xp = x_prompt.reshape(tp, D_MODEL)
    xs = x_sample.reshape(tsn, D_MODEL)
    c_all = jnp.concatenate(
        [c_prompt, jnp.zeros((MOD_PROMPT_ROWS - bp, D_MODEL), F32), c_sample,
         jnp.zeros((MOD_ROWS - MOD_PROMPT_ROWS - bs, D_MODEL), F32)], axis=0)
    cos_p, sin_p = _rope_tables(jnp.arange(sp))
    cos_s, sin_s = _rope_tables(PAST_LEN + jnp.arange(ss))
    cos_s = jnp.tile(cos_s, (bs, 1))
    sin_s = jnp.tile(sin_s, (bs, 1))

    outs_p, outs_s = [], []
    for l in range(depth):
        last = l == depth - 1
        L = functools.partial(_layer, l=l)
        cache_k, cache_v = L(cache_k_win), L(cache_v_win)
        mod = _adaln(c_all, L(ada_w), L(ada_b))
        p = dict(conv_w=L(conv_w), conv_b=L(conv_b).reshape(1, LRU_WIDTH), wa=L(lru_w_a),
                 ba=L(lru_b_a).reshape(1, LRU_WIDTH), wx=L(lru_w_x), bx=L(lru_b_x).reshape(1, LRU_WIDTH),
                 lam=L(lru_lambda).reshape(1, LRU_WIDTH), sinks=L(attn_sinks))
        ffn1 = (L(norm_ffn1), L(ffn1_w_gate), L(ffn1_w_up), L(ffn1_w_down), norm_final)
        ffn2 = (L(norm_ffn2), L(ffn2_w_gate), L(ffn2_w_up), L(ffn2_w_down), norm_final)
        ffn_p = dict(tm=tm_p, nr=FFN_ROWS, row0=0, rows_per_batch=sp)
        ffn_s = dict(tm=tsn, nr=ss, row0=MOD_PROMPT_ROWS, rows_per_batch=ss)
        proj_s_kw = dict(nr=ss, row0=MOD_PROMPT_ROWS, rows_per_batch=ss)

        g, wg, wu, wd, gf = ffn1
        xs, wg_b, wu_b, wd_b = _ffn(xs, mod, (0, 1, 2), g, wg, wu, wd, gf, emit=True, final_norm=False,
                                    name="ffn1_sample", **ffn_s)
        xp, = _ffn(xp, mod, (0, 1, 2), g, wg_b, wu_b, wd_b, gf, emit=False, final_norm=False,
                   name="ffn1_prompt", **ffn_p)

        proj_s, w_in_b = _inproj(xs, mod, (3, 4), L(norm_mix), L(w_in), tn=INPROJ_TN, name="inproj_sample",
                                 **proj_s_kw)
        prev_s = jnp.pad(L(state_conv), ((0, 0), (SUBLANES - (CONV_W - 1), 0), (0, 0))).reshape(tsn, LRU_WIDTH)
        mix_s, lru_s, knew_s = _mix_sample(
            proj_s, prev_s, L(state_lru), cache_k.reshape(bs, WINDOW * N_KV_HEADS, HEAD_DIM),
            cache_v.reshape(bs, WINDOW * N_KV_HEADS, HEAD_DIM), cos_s, sin_s, p, batch=bs, seq=ss)
        xs, w_out_b = _outproj(mix_s, xs, mod, 5, L(w_out), tn=OUTPROJ_TN, name="outproj_sample", **proj_s_kw)
        xp, lru_p, k_p, v_p, conv_p = _mixer_prompt(xp, mod, L(norm_mix), w_in_b, w_out_b, cos_p, sin_p, p,
                                                    batch=bp, seq=sp)

        g, wg, wu, wd, gf = ffn2
        xp, wg_b, wu_b, wd_b = _ffn(xp, mod, (6, 7, 8), g, wg, wu, wd, gf, emit=True, tf=FFN_EMIT_TF, ntiles=1,
                                    inplace=True, final_norm=last, name="ffn2_head", **ffn_p)
        xp, = _ffn(xp, mod, (6, 7, 8), g, wg_b, wu_b, wd_b, gf, emit=False, tile0=1, ntiles=tp // tm_p - 1,
                   inplace=True, final_norm=last, name="ffn2_prompt", **ffn_p)
        xs, = _ffn(xs, mod, (6, 7, 8), g, wg_b, wu_b, wd_b, gf, emit=False, final_norm=last,
                   name="ffn2_sample", **ffn_s)

        ps = proj_s.reshape(bs, ss, IN_COLS)
        outs_p.append((
            conv_p[:, SUBLANES - (CONV_W - 1):, :],
            lru_p.reshape(bp, LRU_WIDTH),
            k_p.reshape(bp, WINDOW, N_KV_HEADS, HEAD_DIM),
            v_p.reshape(bp, WINDOW, N_KV_HEADS, HEAD_DIM),
        ))
        k_all = jnp.concatenate([cache_k, knew_s.reshape(bs, ss, N_KV_HEADS, HEAD_DIM)], axis=1)
        v_all = jnp.concatenate([cache_v, ps[:, :, V_COL:].reshape(bs, ss, N_KV_HEADS, HEAD_DIM)], axis=1)
        outs_s.append((
            ps[:, ss - (CONV_W - 1):, 0:LRU_WIDTH],
            lru_s,
            k_all[:, -WINDOW:],
            v_all[:, -WINDOW:],
        ))

    y_prompt = xp.reshape(bp, sp, D_MODEL)
    y_sample = xs.reshape(bs, ss, D_MODEL)
    st_p = [jnp.stack([o[i] for o in outs_p]) for i in range(4)]
    st_s = [jnp.stack([o[i] for o in outs_s]) for i in range(4)]
    return (y_prompt, y_sample, st_p[0], st_p[1], st_p[2], st_p[3], st_s[0], st_s[1], st_s[2], st_s[3])
```

```python
import functools

import jax
import jax.numpy as jnp
from jax import lax
from jax.experimental import pallas as pl
from jax.experimental.pallas import tpu as pltpu

F32 = jnp.float32
BF16 = jnp.bfloat16

D_MODEL = 2048
D_FF = 5504
LRU_WIDTH = 1024
LRU_HEADS = 8
LRU_BLOCK = 128
CONV_W = 4
LRU_C = 8.0
HEAD_DIM = 128
N_HEADS = 8
N_KV_HEADS = 2
GROUP = N_HEADS // N_KV_HEADS
ATTN_WIDTH = N_HEADS * HEAD_DIM
KV_WIDTH = N_KV_HEADS * HEAD_DIM
WINDOW = 128
ROPE_THETA = 10000.0
EPS = 1e-6
NEG = -1e30
PAST_LEN = 16384
IN_COLS = 2 * LRU_WIDTH + ATTN_WIDTH + 2 * KV_WIDTH
Q_COL = 2 * LRU_WIDTH
K_COL = Q_COL + ATTN_WIDTH
V_COL = K_COL + KV_WIDTH
SCALE = HEAD_DIM ** -0.5

SUBLANES = 8
V7X_VMEM_LIMIT_BYTES = 56 * 1024 * 1024

MOD_ROWS = 48
MOD_PROMPT_ROWS = 8
ADALN_TN = 1024
MIX_TS = 256
BF16_ROWS = 16
FFN_TF = 512
FFN_EMIT_TF = 256
FFN_ROWS = 16
FFN_EPILOGUE_ROWS = 128
FFN_ROW_BLOCK = 256
FFN_FINISH_ROWS = 32
INPROJ_TN = IN_COLS // 2
OUTPROJ_TN = D_MODEL // 2
SAMPLE_ATTN_BATCHES = 4
MIXER_TN = 256


def _params(sem, vmem=V7X_VMEM_LIMIT_BYTES):
    return pltpu.CompilerParams(dimension_semantics=sem, vmem_limit_bytes=vmem)


def _sigmoid(x):
    return 1.0 / (1.0 + jnp.exp(-x))


def _normmod(x, g, sc, sh):
    ms = jnp.mean(x * x, axis=-1, keepdims=True)
    xn = x * lax.rsqrt(ms + EPS) * g
    return xn * (1.0 + sc) + sh


def _mod_row(ref, token, row0, batch_shift):
    return ref[pl.ds(row0 + lax.shift_right_logical(token, batch_shift), 1), :]


def _mod_spec(chunk):
    return pl.BlockSpec((MOD_ROWS, D_MODEL), lambda i, j: (0, chunk))


def _adaln_body(c_ref, w_ref, b_ref, o_ref):
    c = c_ref[...]
    s = (c * _sigmoid(c)).astype(BF16)
    o_ref[...] = jnp.dot(s, w_ref[...].astype(BF16), preferred_element_type=F32) + b_ref[...]


def _adaln(c_all, ada_w, ada_b):
    n = ada_w.shape[1]
    return pl.pallas_call(
        _adaln_body,
        grid=(n // ADALN_TN,),
        in_specs=[
            pl.BlockSpec((MOD_ROWS, D_MODEL), lambda j: (0, 0)),
            pl.BlockSpec((D_MODEL, ADALN_TN), lambda j: (0, j)),
            pl.BlockSpec((1, ADALN_TN), lambda j: (0, j)),
        ],
        out_specs=pl.BlockSpec((MOD_ROWS, ADALN_TN), lambda j: (0, j)),
        out_shape=jax.ShapeDtypeStruct((MOD_ROWS, n), F32),
        compiler_params=_params(("arbitrary",)),
        name="adaln",
    )(c_all, ada_w, ada_b.reshape(1, n))


def _ffn_body(x_ref, sh_ref, sc_ref, gt_ref, g_ref, wg_ref, wu_ref, wd_ref, gf_ref, *rest,
              nr, er, rb, row0, batch_shift, tile0, final_norm, emit):
    if emit:
        o_ref, wgb_ref, wub_ref, wdb_ref, h_ref = rest
    else:
        o_ref, h_ref = rest
    i = pl.program_id(0) + tile0
    j = pl.program_id(1)
    nj = pl.num_programs(1)
    tm = x_ref.shape[0]
    tf = wd_ref.shape[0]

    if emit:
        valid = D_FF - j * tf
        cmask = lax.broadcasted_iota(jnp.int32, (1, tf), 1) < valid
        rmask = lax.broadcasted_iota(jnp.int32, (tf, 1), 0) < valid
        wgb_ref[...] = jnp.where(cmask, wg_ref[...], 0.0).astype(BF16)
        wub_ref[...] = jnp.where(cmask, wu_ref[...], 0.0).astype(BF16)
        wdb_ref[...] = jnp.where(rmask, wd_ref[...], 0.0).astype(BF16)
        wg_ref, wu_ref, wd_ref = wgb_ref, wub_ref, wdb_ref

    def swiglu(rows):
        h = h_ref[rows, :].astype(BF16)
        g = jnp.dot(h, wg_ref[...], preferred_element_type=F32)
        u = jnp.dot(h, wu_ref[...], preferred_element_type=F32)
        a = (g * _sigmoid(g) * u).astype(BF16)
        return jnp.dot(a, wd_ref[...], preferred_element_type=F32)

    def prologue(rows, token):
        h = _normmod(x_ref[rows, :], g_ref[...], _mod_row(sc_ref, token, row0, batch_shift),
                     _mod_row(sh_ref, token, row0, batch_shift))
        h_ref[rows, :] = h.astype(h_ref.dtype)

    def epilogue(rows, token, acc):
        y = x_ref[rows, :] + (0.5 * _mod_row(gt_ref, token, row0, batch_shift)) * acc
        if final_norm:
            ms = jnp.mean(y * y, axis=-1, keepdims=True)
            y = y * lax.rsqrt(ms + EPS) * gf_ref[...]
        o_ref[rows, :] = y

    if rb:
        blocks = range(0, tm, rb)

        def block_prologue(r0):
            for c in range(r0, r0 + rb, nr):
                prologue(slice(c, c + nr), i * tm + c)

        @pl.when(j == 0)
        def _():
            block_prologue(0)
            for r0 in blocks:
                o_ref[r0:r0 + rb, :] = swiglu(slice(r0, r0 + rb))
                if r0 + rb < tm:
                    block_prologue(r0 + rb)

        @pl.when((j > 0) & (j < nj - 1))
        def _():
            o_ref[...] += swiglu(slice(None))

        @pl.when(j == nj - 1)
        def _():
            for r0 in blocks:
                acc = o_ref[r0:r0 + rb, :] + swiglu(slice(r0, r0 + rb))
                for c in range(0, rb, FFN_FINISH_ROWS):
                    epilogue(slice(r0 + c, r0 + c + FFN_FINISH_ROWS), i * tm + r0 + c,
                             acc[c:c + FFN_FINISH_ROWS, :])
    else:
        @pl.when(j == 0)
        def _():
            def body(r, carry):
                rows = pl.ds(pl.multiple_of(r * nr, nr), nr)
                prologue(rows, i * tm + r * nr)
                o_ref[rows, :] = jnp.zeros((nr, D_MODEL), F32)
                return carry

            lax.fori_loop(0, tm // nr, body, 0, unroll=4)

        o_ref[...] += swiglu(slice(None))

        @pl.when(j == nj - 1)
        def _():
            def body(r, carry):
                rows = pl.ds(pl.multiple_of(r * er, er), er)
                epilogue(rows, i * tm + r * er, o_ref[rows, :])
                return carry

            lax.fori_loop(0, tm // er, body, 0)


def _ffn(x, mod, chunks, g, wg, wu, wd, gf, *, tm, nr, row0, rows_per_batch, final_norm, emit, name,
         tf=FFN_TF, tile0=0, ntiles=None, inplace=False):
    t = x.shape[0]
    ntiles = t // tm if ntiles is None else ntiles
    assert tf == FFN_TF or emit
    sub = FFN_TF // tf
    nj_store = pl.cdiv(D_FF, FFN_TF)
    nj = nj_store * sub
    er = min(FFN_EPILOGUE_ROWS, rows_per_batch)
    assert rows_per_batch & (rows_per_batch - 1) == 0 and rows_per_batch % nr == 0 and tm % nr == 0
    assert rows_per_batch % er == 0 and tm % er == 0
    rb = FFN_ROW_BLOCK if rows_per_batch % FFN_ROW_BLOCK == 0 and tm % FFN_ROW_BLOCK == 0 else 0
    body = functools.partial(_ffn_body, nr=nr, er=er, rb=rb, row0=row0,
                             batch_shift=rows_per_batch.bit_length() - 1, tile0=tile0,
                             final_norm=final_norm, emit=emit)

    row = pl.BlockSpec((1, D_MODEL), lambda i, j: (0, 0))
    tile_major = pl.BlockSpec((None, D_MODEL, tf), lambda i, j: (j // sub, 0, j % sub))
    row_major = pl.BlockSpec((tf, D_MODEL), lambda i, j: (j, 0))
    if emit:
        assert ntiles == 1, "bf16 weight tiles are written once, by a single token tile"
        wspecs = [pl.BlockSpec((D_MODEL, tf), lambda i, j: (0, j)),
                  pl.BlockSpec((D_MODEL, tf), lambda i, j: (0, j)),
                  row_major]
    else:
        wspecs = [tile_major, tile_major, row_major]
    xmode = dict(pipeline_mode=pl.Buffered(1)) if ntiles == 1 else {}
    in_specs = [pl.BlockSpec((tm, D_MODEL), lambda i, j: (i + tile0, 0), **xmode),
                _mod_spec(chunks[0]), _mod_spec(chunks[1]), _mod_spec(chunks[2]), row] + wspecs + [row]
    args = [x, mod, mod, mod, g.reshape(1, D_MODEL), wg, wu, wd, gf.reshape(1, D_MODEL)]
    out_specs = [pl.BlockSpec((tm, D_MODEL), lambda i, j: (i + tile0, 0))]
    out_shape = [jax.ShapeDtypeStruct((t, D_MODEL), F32)]
    if emit:
        out_specs += [tile_major, tile_major, row_major]
        out_shape += [jax.ShapeDtypeStruct((nj_store, D_MODEL, FFN_TF), BF16),
                      jax.ShapeDtypeStruct((nj_store, D_MODEL, FFN_TF), BF16),
                      jax.ShapeDtypeStruct((nj_store * FFN_TF, D_MODEL), BF16)]
    return pl.pallas_call(
        body,
        grid=(ntiles, nj),
        in_specs=in_specs,
        out_specs=out_specs,
        out_shape=out_shape,
        input_output_aliases={0: 0} if inplace else {},
        scratch_shapes=[pltpu.VMEM((tm, D_MODEL), BF16 if nr % BF16_ROWS == 0 else F32)],
        compiler_params=_params(("parallel", "arbitrary")),
        name=name,
    )(*args)


def _inproj_body(x_ref, sh_ref, sc_ref, g_ref, w_ref, o_ref, wb_ref, h_ref, *, nr, row0, batch_shift):
    j = pl.program_id(1)
    tm = x_ref.shape[0]

    @pl.when(j == 0)
    def _():
        g = g_ref[...]

        def body(r, carry):
            rows = pl.ds(pl.multiple_of(r * nr, nr), nr)
            tok = r * nr
            h = _normmod(x_ref[rows, :], g, _mod_row(sc_ref, tok, row0, batch_shift),
                         _mod_row(sh_ref, tok, row0, batch_shift))
            h_ref[rows, :] = h
            return carry

        lax.fori_loop(0, tm // nr, body, 0, unroll=4)

    w = w_ref[...].astype(BF16)
    wb_ref[...] = w
    o_ref[...] = jnp.dot(h_ref[...].astype(BF16), w, preferred_element_type=F32)


def _inproj(x, mod, chunks, g, w_in, *, tn, nr, row0, rows_per_batch, name):
    tm = x.shape[0]
    assert rows_per_batch & (rows_per_batch - 1) == 0 and rows_per_batch % nr == 0 and tm % nr == 0
    body = functools.partial(_inproj_body, nr=nr, row0=row0, batch_shift=rows_per_batch.bit_length() - 1)
    wspec = pl.BlockSpec((D_MODEL, tn), lambda i, j: (0, j))
    return pl.pallas_call(
        body,
        grid=(1, IN_COLS // tn),
        in_specs=[
            pl.BlockSpec((tm, D_MODEL), lambda i, j: (0, 0)),
            _mod_spec(chunks[0]), _mod_spec(chunks[1]),
            pl.BlockSpec((1, D_MODEL), lambda i, j: (0, 0)),
            wspec,
        ],
        out_specs=[pl.BlockSpec((tm, tn), lambda i, j: (0, j)), wspec],
        out_shape=[jax.ShapeDtypeStruct((tm, IN_COLS), F32), jax.ShapeDtypeStruct((D_MODEL, IN_COLS), BF16)],
        scratch_shapes=[pltpu.VMEM((tm, D_MODEL), F32)],
        compiler_params=_params(("arbitrary", "arbitrary")),
        name=name,
    )(x, mod, mod, g.reshape(1, D_MODEL), w_in)


def _outproj_body(m_ref, x_ref, gt_ref, w_ref, o_ref, wb_ref, *, nr, row0, batch_shift):
    tm = x_ref.shape[0]
    w = w_ref[...].astype(BF16)
    wb_ref[...] = w
    o_ref[...] = jnp.dot(m_ref[...].astype(BF16), w, preferred_element_type=F32)

    def body(r, carry):
        rows = pl.ds(pl.multiple_of(r * nr, nr), nr)
        gt = _mod_row(gt_ref, r * nr, row0, batch_shift)
        o_ref[rows, :] = x_ref[rows, :] + gt * o_ref[rows, :]
        return carry

    lax.fori_loop(0, tm // nr, body, 0)


def _outproj(mix, x, mod, gt_c, w_out, *, tn, nr, row0, rows_per_batch, name):
    tm = x.shape[0]
    nn = D_MODEL // tn
    assert rows_per_batch & (rows_per_batch - 1) == 0 and rows_per_batch % nr == 0 and tm % nr == 0
    body = functools.partial(_outproj_body, nr=nr, row0=row0, batch_shift=rows_per_batch.bit_length() - 1)
    wspec = pl.BlockSpec((D_MODEL, tn), lambda i, j: (0, j))
    tile = pl.BlockSpec((tm, tn), lambda i, j: (0, j))
    return pl.pallas_call(
        body,
        grid=(1, nn),
        in_specs=[
            pl.BlockSpec((tm, D_MODEL), lambda i, j: (0, 0)),
            tile,
            pl.BlockSpec((MOD_ROWS, tn), lambda i, j: (0, gt_c * nn + j)),
            wspec,
        ],
        out_specs=[tile, wspec],
        out_shape=[jax.ShapeDtypeStruct((tm, D_MODEL), F32), jax.ShapeDtypeStruct((D_MODEL, D_MODEL), BF16)],
        compiler_params=_params(("arbitrary", "arbitrary")),
        name=name,
    )(mix, x, mod, w_out)


def _rope(x, cos, sin_signed):
    return x * cos + pltpu.roll(x, HEAD_DIM // 2, 1) * sin_signed


def _lru_gates(xc_ref, a_ref, wa_ref, ba_ref, wx_ref, bx_ref, lam_ref, midway=None):
    nlam = -lam_ref[...]
    softplus = jnp.maximum(nlam, 0.0) + jnp.log1p(jnp.exp(-jnp.abs(nlam)))
    rate = -LRU_C * softplus
    for hh in range(LRU_HEADS):
        if midway is not None:
            midway()
        cols = slice(hh * LRU_BLOCK, (hh + 1) * LRU_BLOCK)
        xc = xc_ref[:, cols]
        xcb = xc.astype(BF16)
        ra = jnp.dot(xcb, wa_ref[hh].astype(BF16), preferred_element_type=F32) + ba_ref[:, cols]
        rx = jnp.dot(xcb, wx_ref[hh].astype(BF16), preferred_element_type=F32) + bx_ref[:, cols]
        r = _sigmoid(ra)
        gi = _sigmoid(rx)
        a = jnp.exp(r * rate[:, cols])
        a_ref[:, cols] = a
        xc_ref[:, cols] = jnp.sqrt(1.0 - a * a) * (gi * xc)


def _scan_group(a, u, carry, row):
    for s in (1, 2, 4):
        a_sh = pltpu.roll(a, s, 0)
        u_sh = pltpu.roll(u, s, 0)
        m = row >= s
        u = jnp.where(m, a * u_sh + u, u)
        a = jnp.where(m, a * a_sh, a)
    return a * carry + u


def _softmax_pv(s, mask, sink, v):
    s = jnp.where(mask, s, NEG)
    m = jnp.maximum(jnp.max(s, axis=-1, keepdims=True), sink)
    p = jnp.exp(s - m)
    den = jnp.sum(p, axis=-1, keepdims=True) + jnp.exp(sink - m)
    return jnp.dot(p.astype(BF16), v, preferred_element_type=F32) / den


def _sink_column(sink_ref, kh, rows_per_head):
    n = GROUP * rows_per_head
    ri = lax.broadcasted_iota(jnp.int32, (n, 1), 0)
    col = jnp.full((n, 1), sink_ref[kh * GROUP + GROUP - 1], F32)
    for g in range(GROUP - 2, -1, -1):
        col = jnp.where(ri < (g + 1) * rows_per_head, sink_ref[kh * GROUP + g], col)
    return col


def _mix_tile(proj, mix, cs_ref, sn_ref, cw_ref, cb_ref, wa_ref, ba_ref, wx_ref, bx_ref, lam_ref, sink_ref,
              xpad, hc, kpad, vpad, a_s, u_s, klast, *, ts, pos0, mxu_fill):
    fill = iter(mxu_fill)

    def take(n):
        for _ in range(n):
            f = next(fill, None)
            if f is not None:
                f()

    w = LRU_WIDTH
    xpad[SUBLANES:SUBLANES + ts, :] = proj[:, 0:w]
    half = ts // 2
    for r0 in (0, half):
        take(1)
        xc = cb_ref[...]
        for jj in range(CONV_W):
            off = r0 + SUBLANES - (CONV_W - 1) + jj
            xc = xc + xpad[off:off + half, :] * cw_ref[jj:jj + 1, :]
        u_s[r0:r0 + half, :] = xc
    xpad[0:SUBLANES, :] = xpad[ts:ts + SUBLANES, :]

    _lru_gates(u_s, a_s, wa_ref, ba_ref, wx_ref, bx_ref, lam_ref, midway=lambda: take(1))

    row = lax.broadcasted_iota(jnp.int32, (SUBLANES, w), 0)
    carry = hc[...]
    ngroups = ts // SUBLANES
    for g in range(ngroups):
        if g % (ngroups // 8) == 0:
            take(1)
        rows = slice(g * SUBLANES, (g + 1) * SUBLANES)
        h = _scan_group(a_s[rows, :], u_s[rows, :], carry, row)
        a_s[rows, :] = h
        carry = jnp.broadcast_to(h[SUBLANES - 1:SUBLANES, :], (SUBLANES, w))
    hc[...] = carry
    for r0 in (0, half):
        take(1)
        rows = slice(r0, r0 + half)
        mix[rows, 0:w] = (a_s[rows, :] * jax.nn.gelu(proj[rows, w:2 * w])).astype(BF16)

    take(1)
    cos = cs_ref[...]
    sin = sn_ref[...]
    for kh in range(N_KV_HEADS):
        cols = slice(kh * HEAD_DIM, (kh + 1) * HEAD_DIM)
        kr = _rope(proj[:, K_COL + kh * HEAD_DIM:K_COL + (kh + 1) * HEAD_DIM], cos, sin)
        kpad[WINDOW:WINDOW + ts, cols] = kr.astype(BF16)
        klast[:, cols] = kr[ts - WINDOW:, :]
    vpad[WINDOW:WINDOW + ts, :] = proj[:, V_COL:V_COL + KV_WIDTH].astype(BF16)

    nq = GROUP * WINDOW
    r_i = lax.broadcasted_iota(jnp.int32, (nq, 2 * WINDOW), 0) & (WINDOW - 1)
    c_i = lax.broadcasted_iota(jnp.int32, (nq, 2 * WINDOW), 1)
    band = (c_i >= r_i) & (c_i <= r_i + WINDOW)
    for n in range(ts // WINDOW):
        qrows = slice(n * WINDOW, (n + 1) * WINDOW)
        kpos0 = pos0 + (n - 1) * WINDOW
        mask = band & (c_i + kpos0 >= 0)
        for kh in range(N_KV_HEADS):
            cols = slice(kh * HEAD_DIM, (kh + 1) * HEAD_DIM)
            qs = []
            for g in range(GROUP):
                hd = kh * GROUP + g
                qh = proj[qrows, Q_COL + hd * HEAD_DIM:Q_COL + (hd + 1) * HEAD_DIM]
                qs.append(_rope(qh, cos[qrows, :], sin[qrows, :]).astype(BF16))
            q4 = jnp.concatenate(qs, axis=0)
            kk = kpad[n * WINDOW:(n + 2) * WINDOW, cols]
            vv = vpad[n * WINDOW:(n + 2) * WINDOW, cols]
            s = lax.dot_general(q4, kk, (((1,), (1,)), ((), ())), preferred_element_type=F32) * SCALE
            take(1)
            o = _softmax_pv(s, mask, _sink_column(sink_ref, kh, WINDOW), vv)
            for g in range(GROUP):
                hd = kh * GROUP + g
                mix[qrows, w + hd * HEAD_DIM:w + (hd + 1) * HEAD_DIM] = (
                    o[g * WINDOW:(g + 1) * WINDOW, :].astype(BF16))

    take(len(mxu_fill))
    kpad[0:WINDOW, :] = kpad[ts:ts + WINDOW, :]
    vpad[0:WINDOW, :] = vpad[ts:ts + WINDOW, :]


def _mixer_body(xa_ref, xc_ref, sh_ref, sc_ref, gt_ref, gm_ref, win_ref, wout_ref, cs_ref, sn_ref,
                cw_ref, cb_ref, wa_ref, ba_ref, wx_ref, bx_ref, lam_ref, sink_ref,
                o_ref, lru_ref, k_ref, v_ref, conv_ref,
                proj0, proj1, mix0, mix1, h_s, xpad, hc, kpad, vpad, a_s, u_s, klast,
                *, ts, nt, ntiles, batch_shift):
    s = pl.program_id(0)
    tile_a = jnp.minimum(s, ntiles - 1)
    tile_b = jnp.clip(s - 1, 0, ntiles - 1)
    tile_c = jnp.clip(s - 2, 0, ntiles - 1)
    t_in = tile_b & (nt - 1)
    proj = (proj0, proj1)
    mix = (mix0, mix1)

    def out_chunks(mix_r):
        gt = _mod_row(gt_ref, tile_c * ts, 0, batch_shift)

        def chunk(k):
            cols = slice(k * MIXER_TN, (k + 1) * MIXER_TN)

            def run():
                d = jnp.dot(mix_r[...], wout_ref[:, cols], preferred_element_type=F32)
                o_ref[:, cols] = xc_ref[:, cols] + gt[:, cols] * d
            return run
        return [chunk(k) for k in range(D_MODEL // MIXER_TN)]

    def in_chunks(proj_w):
        def chunk(k):
            cols = slice(k * MIXER_TN, (k + 1) * MIXER_TN)

            def run():
                if k == 0:
                    g = gm_ref[...]
                    sc = _mod_row(sc_ref, tile_a * ts, 0, batch_shift)
                    sh = _mod_row(sh_ref, tile_a * ts, 0, batch_shift)
                    for r in range(ts // FFN_ROWS):
                        rows = slice(r * FFN_ROWS, (r + 1) * FFN_ROWS)
                        h_s[rows, :] = _normmod(xa_ref[rows, :], g, sc, sh).astype(BF16)
                proj_w[:, cols] = jnp.dot(h_s[...], win_ref[:, cols], preferred_element_type=F32)
            return run
        return [chunk(k) for k in range(IN_COLS // MIXER_TN)]

    @pl.when(s == 0)
    def _():
        mix1[...] = jnp.zeros(mix1.shape, BF16)
        for run in in_chunks(proj[0]):
            run()

    @pl.when(s == ntiles + 1)
    def _():
        for run in out_chunks(mix[(ntiles - 1) & 1]):
            run()

    @pl.when(t_in == 0)
    def _():
        xpad[0:SUBLANES, :] = jnp.zeros((SUBLANES, LRU_WIDTH), F32)
        hc[...] = jnp.zeros((SUBLANES, LRU_WIDTH), F32)
        kpad[0:WINDOW, :] = jnp.zeros((WINDOW, KV_WIDTH), BF16)
        vpad[0:WINDOW, :] = jnp.zeros((WINDOW, KV_WIDTH), BF16)

    def step(par):
        proj_r, mix_w = proj[1 - par], mix[1 - par]
        _mix_tile(proj_r, mix_w, cs_ref, sn_ref, cw_ref, cb_ref, wa_ref, ba_ref, wx_ref, bx_ref, lam_ref,
                  sink_ref, xpad, hc, kpad, vpad, a_s, u_s, klast, ts=ts, pos0=t_in * ts,
                  mxu_fill=in_chunks(proj[par]) + out_chunks(mix[par]))
        lru_ref[0] = hc[0:1, :]
        k_ref[0] = klast[...]
        v_ref[0] = proj_r[ts - WINDOW:, V_COL:V_COL + KV_WIDTH]
        conv_ref[0] = xpad[0:SUBLANES, :]

    for par in (0, 1):
        @pl.when((s >= 1) & (s <= ntiles) & ((s & 1) == par))
        def _(par=par):
            step(par)


def _mixer_prompt(x, mod, g_mix, w_in_b, w_out_b, cos, sin, p, *, batch, seq):
    ts = MIX_TS
    nt = seq // ts
    ntiles = batch * nt
    assert nt & (nt - 1) == 0 and seq & (seq - 1) == 0
    w = LRU_WIDTH
    body = functools.partial(_mixer_body, ts=ts, nt=nt, ntiles=ntiles, batch_shift=seq.bit_length() - 1)

    def tile_a(s):
        return jnp.minimum(s, ntiles - 1)

    def tile_b(s):
        return jnp.clip(s - 1, 0, ntiles - 1)

    def tile_c(s):
        return jnp.clip(s - 2, 0, ntiles - 1)

    def whole(shape, **kw):
        return pl.BlockSpec(shape, lambda s: (0,) * len(shape), **kw)

    def mspec(c):
        return pl.BlockSpec((MOD_ROWS, D_MODEL), lambda s: (0, c))

    def state(shape):
        return pl.BlockSpec((1,) + shape, lambda s: (tile_b(s) // nt, 0, 0))

    once = dict(pipeline_mode=pl.Buffered(1))
    return pl.pallas_call(
        body,
        grid=(ntiles + 2,),
        in_specs=[
            pl.BlockSpec((ts, D_MODEL), lambda s: (tile_a(s), 0)),
            pl.BlockSpec((ts, D_MODEL), lambda s: (tile_c(s), 0)),
            mspec(3), mspec(4), mspec(5),
            whole((1, D_MODEL)),
            whole((D_MODEL, IN_COLS), **once),
            whole((D_MODEL, D_MODEL), **once),
            pl.BlockSpec((ts, HEAD_DIM), lambda s: (tile_b(s) % nt, 0)),
            pl.BlockSpec((ts, HEAD_DIM), lambda s: (tile_b(s) % nt, 0)),
            whole((CONV_W, w)), whole((1, w)),
            whole((LRU_HEADS, LRU_BLOCK, LRU_BLOCK)), whole((1, w)),
            whole((LRU_HEADS, LRU_BLOCK, LRU_BLOCK)), whole((1, w)),
            whole((1, w)),
            pl.BlockSpec(memory_space=pltpu.SMEM),
        ],
        out_specs=[
            pl.BlockSpec((ts, D_MODEL), lambda s: (tile_c(s), 0)),
            state((1, w)), state((WINDOW, KV_WIDTH)), state((WINDOW, KV_WIDTH)), state((SUBLANES, w)),
        ],
        out_shape=[
            jax.ShapeDtypeStruct((batch * seq, D_MODEL), F32),
            jax.ShapeDtypeStruct((batch, 1, w), F32),
            jax.ShapeDtypeStruct((batch, WINDOW, KV_WIDTH), F32),
            jax.ShapeDtypeStruct((batch, WINDOW, KV_WIDTH), F32),
            jax.ShapeDtypeStruct((batch, SUBLANES, w), F32),
        ],
        scratch_shapes=[
            pltpu.VMEM((ts, IN_COLS), F32), pltpu.VMEM((ts, IN_COLS), F32),
            pltpu.VMEM((ts, D_MODEL), BF16), pltpu.VMEM((ts, D_MODEL), BF16),
            pltpu.VMEM((ts, D_MODEL), BF16),
            pltpu.VMEM((SUBLANES + ts, w), F32),
            pltpu.VMEM((SUBLANES, w), F32),
            pltpu.VMEM((WINDOW + ts, KV_WIDTH), BF16),
            pltpu.VMEM((WINDOW + ts, KV_WIDTH), BF16),
            pltpu.VMEM((ts, w), F32),
            pltpu.VMEM((ts, w), F32),
            pltpu.VMEM((WINDOW, KV_WIDTH), F32),
        ],
        compiler_params=_params(("arbitrary",)),
        name="mixer_prompt",
    )(x, x, mod, mod, mod, g_mix.reshape(1, D_MODEL), w_in_b, w_out_b, cos, sin,
      p["conv_w"], p["conv_b"], p["wa"], p["ba"], p["wx"], p["bx"], p["lam"], p["sinks"])


def _mix_sample_body(proj_ref, prev_ref, h0_ref, ck_ref, cv_ref, cs_ref, sn_ref, cw_ref, cb_ref,
                     wa_ref, ba_ref, wx_ref, bx_ref, lam_ref, sink_ref,
                     mix_ref, lru_ref, knew_ref,
                     a_s, u_s, q_s, *, batch, seq):
    w = LRU_WIDTH
    row = lax.broadcasted_iota(jnp.int32, (SUBLANES, w), 0)

    def group(b):
        return pl.ds(pl.multiple_of(b * SUBLANES, SUBLANES), SUBLANES)

    def conv_body(b, carry):
        rows = group(b)
        cur = proj_ref[rows, 0:w]
        prev = prev_ref[rows, :]
        xc = cb_ref[...]
        for jj in range(CONV_W):
            d = CONV_W - 1 - jj
            if d == 0:
                term = cur
            else:
                term = jnp.where(row >= d, pltpu.roll(cur, d, 0), pltpu.roll(prev, d, 0))
            xc = xc + term * cw_ref[jj:jj + 1, :]
        u_s[rows, :] = xc
        return carry

    lax.fori_loop(0, batch, conv_body, 0, unroll=4)

    _lru_gates(u_s, a_s, wa_ref, ba_ref, wx_ref, bx_ref, lam_ref)

    def scan_body(b, carry):
        rows = group(b)
        h0 = jnp.broadcast_to(h0_ref[pl.ds(b, 1), :], (SUBLANES, w))
        h = _scan_group(a_s[rows, :], u_s[rows, :], h0, row)
        a_s[rows, :] = h
        lru_ref[pl.ds(b, 1), :] = h[SUBLANES - 1:SUBLANES, :]
        return carry

    lax.fori_loop(0, batch, scan_body, 0, unroll=4)
    mix_ref[:, 0:w] = a_s[...] * jax.nn.gelu(proj_ref[:, w:2 * w])

    cos = cs_ref[...]
    sin = sn_ref[...]
    for hd in range(N_HEADS):
        cols = slice(hd * HEAD_DIM, (hd + 1) * HEAD_DIM)
        q_s[:, cols] = _rope(proj_ref[:, Q_COL + hd * HEAD_DIM:Q_COL + (hd + 1) * HEAD_DIM], cos, sin)
    for kh in range(N_KV_HEADS):
        cols = slice(kh * HEAD_DIM, (kh + 1) * HEAD_DIM)
        knew_ref[:, cols] = _rope(proj_ref[:, K_COL + kh * HEAD_DIM:K_COL + (kh + 1) * HEAD_DIM], cos, sin)

    nq = GROUP * seq
    nk = 2 * WINDOW
    r_i = lax.broadcasted_iota(jnp.int32, (nq, nk), 0) & (seq - 1)
    c_i = lax.broadcasted_iota(jnp.int32, (nq, nk), 1)
    mask = (c_i >= r_i) & (c_i <= r_i + WINDOW)
    pad = jnp.zeros((nk - WINDOW - seq, HEAD_DIM), F32)

    def attn_body(bg, carry):
        work = [(bg * SAMPLE_ATTN_BATCHES + k, kh) for k in range(SAMPLE_ATTN_BATCHES)
                for kh in range(N_KV_HEADS)]
        scores = []
        for b, kh in work:
            rows = group(b)
            cols = slice(kh * HEAD_DIM, (kh + 1) * HEAD_DIM)
            q4 = jnp.concatenate(
                [q_s[rows, (kh * GROUP + g) * HEAD_DIM:(kh * GROUP + g + 1) * HEAD_DIM] for g in range(GROUP)],
                axis=0).astype(BF16)
            cached = pl.ds(kh, WINDOW, stride=N_KV_HEADS)
            kk = jnp.concatenate([ck_ref[b, cached, :], knew_ref[rows, cols], pad], axis=0).astype(BF16)
            scores.append(lax.dot_general(q4, kk, (((1,), (1,)), ((), ())), preferred_element_type=F32) * SCALE)
        for (b, kh), s in zip(work, scores):
            rows = group(b)
            cached = pl.ds(kh, WINDOW, stride=N_KV_HEADS)
            vv = jnp.concatenate(
                [cv_ref[b, cached, :], proj_ref[rows, V_COL + kh * HEAD_DIM:V_COL + (kh + 1) * HEAD_DIM], pad],
                axis=0).astype(BF16)
            o = _softmax_pv(s, mask, _sink_column(sink_ref, kh, seq), vv)
            for g in range(GROUP):
                hd = kh * GROUP + g
                mix_ref[rows, w + hd * HEAD_DIM:w + (hd + 1) * HEAD_DIM] = o[g * seq:(g + 1) * seq, :]
        return carry

    lax.fori_loop(0, batch // SAMPLE_ATTN_BATCHES, attn_body, 0)


def _mix_sample(proj, prev, h0, cache_k, cache_v, cos, sin, p, *, batch, seq):
    assert seq == SUBLANES, "each sample batch must be exactly one sublane group"
    assert batch % SAMPLE_ATTN_BATCHES == 0
    t = batch * seq
    body = functools.partial(_mix_sample_body, batch=batch, seq=seq)
    vmem = pl.BlockSpec(memory_space=pltpu.VMEM)
    return pl.pallas_call(
        body,
        in_specs=[vmem] * 14 + [pl.BlockSpec(memory_space=pltpu.SMEM)],
        out_specs=[vmem, vmem, vmem],
        out_shape=[
            jax.ShapeDtypeStruct((t, D_MODEL), F32),
            jax.ShapeDtypeStruct((batch, LRU_WIDTH), F32),
            jax.ShapeDtypeStruct((t, KV_WIDTH), F32),
        ],
        scratch_shapes=[
            pltpu.VMEM((t, LRU_WIDTH), F32),
            pltpu.VMEM((t, LRU_WIDTH), F32),
            pltpu.VMEM((t, ATTN_WIDTH), F32),
        ],
        compiler_params=pltpu.CompilerParams(vmem_limit_bytes=V7X_VMEM_LIMIT_BYTES),
        name="mix_sample",
    )(proj, prev, h0, cache_k, cache_v, cos, sin, p["conv_w"], p["conv_b"], p["wa"], p["ba"], p["wx"], p["bx"],
      p["lam"], p["sinks"])


def _layer(w, l):
    return w.reshape(w.shape[1:]) if w.shape[0] == 1 else w[l]


def _rope_tables(pos):
    half = HEAD_DIM // 2
    inv = ROPE_THETA ** (-jnp.arange(half, dtype=F32) / half)
    ang = pos.astype(F32)[:, None] * inv[None, :]
    cos = jnp.cos(ang)
    sin = jnp.sin(ang)
    return jnp.concatenate([cos, cos], axis=-1), jnp.concatenate([-sin, sin], axis=-1)


def kernel(x_prompt, x_sample, c_prompt, c_sample, state_conv, state_lru, cache_k_win, cache_v_win, ada_w, ada_b, norm_ffn1, norm_mix, norm_ffn2, ffn1_w_gate, ffn1_w_up, ffn1_w_down, w_in, conv_w, conv_b, lru_w_a, lru_b_a, lru_w_x, lru_b_x, lru_lambda, attn_sinks, w_out, ffn2_w_gate, ffn2_w_up, ffn2_w_down, norm_final):
    bp, sp, _ = x_prompt.shape
    bs, ss, _ = x_sample.shape
    depth = ada_w.shape[0]
    tp, tsn = bp * sp, bs * ss
    assert bp <= MOD_PROMPT_ROWS and bp + bs <= MOD_ROWS
    tm_p = 1024
    assert sp % tm_p == 0

    xp = x_prompt.reshape(tp, D_MODEL)
    xs = x_sample.reshape(tsn, D_MODEL)
    c_all = jnp.concatenate(
        [c_prompt, jnp.zeros((MOD_PROMPT_ROWS - bp, D_MODEL), F32), c_sample,
         jnp.zeros((MOD_ROWS - MOD_PROMPT_ROWS - bs, D_MODEL), F32)], axis=0)
    cos_p, sin_p = _rope_tables(jnp.arange(sp))
    cos_s, sin_s = _rope_tables(PAST_LEN + jnp.arange(ss))
    cos_s = jnp.tile(cos_s, (bs, 1))
    sin_s = jnp.tile(sin_s, (bs, 1))

    outs_p, outs_s = [], []
    for l in range(depth):
        last = l == depth - 1
        L = functools.partial(_layer, l=l)
        cache_k, cache_v = L(cache_k_win), L(cache_v_win)
        mod = _adaln(c_all, L(ada_w), L(ada_b))
        p = dict(conv_w=L(conv_w), conv_b=L(conv_b).reshape(1, LRU_WIDTH), wa=L(lru_w_a),
                 ba=L(lru_b_a).reshape(1, LRU_WIDTH), wx=L(lru_w_x), bx=L(lru_b_x).reshape(1, LRU_WIDTH),
                 lam=L(lru_lambda).reshape(1, LRU_WIDTH), sinks=L(attn_sinks))
        ffn1 = (L(norm_ffn1), L(ffn1_w_gate), L(ffn1_w_up), L(ffn1_w_down), norm_final)
        ffn2 = (L(norm_ffn2), L(ffn2_w_gate), L(ffn2_w_up), L(ffn2_w_down), norm_final)
        ffn_p = dict(tm=tm_p, nr=FFN_ROWS, row0=0, rows_per_batch=sp)
        ffn_s = dict(tm=tsn, nr=ss, row0=MOD_PROMPT_ROWS, rows_per_batch=ss)
        proj_s_kw = dict(nr=ss, row0=MOD_PROMPT_ROWS, rows_per_batch=ss)

        g, wg, wu, wd, gf = ffn1
        xs, wg_b, wu_b, wd_b = _ffn(xs, mod, (0, 1, 2), g, wg, wu, wd, gf, emit=True, final_norm=False,
                                    name="ffn1_sample", **ffn_s)
        xp, = _ffn(xp, mod, (0, 1, 2), g, wg_b, wu_b, wd_b, gf, emit=False, final_norm=False,
                   name="ffn1_prompt", **ffn_p)

        proj_s, w_in_b = _inproj(xs, mod, (3, 4), L(norm_mix), L(w_in), tn=INPROJ_TN, name="inproj_sample",
                                 **proj_s_kw)
        prev_s = jnp.pad(L(state_conv), ((0, 0), (SUBLANES - (CONV_W - 1), 0), (0, 0))).reshape(tsn, LRU_WIDTH)
        mix_s, lru_s, knew_s = _mix_sample(
            proj_s, prev_s, L(state_lru), cache_k.reshape(bs, WINDOW * N_KV_HEADS, HEAD_DIM),
            cache_v.reshape(bs, WINDOW * N_KV_HEADS, HEAD_DIM), cos_s, sin_s, p, batch=bs, seq=ss)
        xs, w_out_b = _outproj(mix_s, xs, mod, 5, L(w_out), tn=OUTPROJ_TN, name="outproj_sample", **proj_s_kw)
        xp, lru_p, k_p, v_p, conv_p = _mixer_prompt(xp, mod, L(norm_mix), w_in_b, w_out_b, cos_p, sin_p, p,
                                                    batch=bp, seq=sp)

        g, wg, wu, wd, gf = ffn2
        xp, wg_b, wu_b, wd_b = _ffn(xp, mod, (6, 7, 8), g, wg, wu, wd, gf, emit=True, tf=FFN_EMIT_TF, ntiles=1,
                                    inplace=True, final_norm=last, name="ffn2_head", **ffn_p)
        xp, = _ffn(xp, mod, (6, 7, 8), g, wg_b, wu_b, wd_b, gf, emit=False, tile0=1, ntiles=tp // tm_p - 1,
                   inplace=True, final_norm=last, name="ffn2_prompt", **ffn_p)
        xs, = _ffn(xs, mod, (6, 7, 8), g, wg_b, wu_b, wd_b, gf, emit=False, final_norm=last,
                   name="ffn2_sample", **ffn_s)

        ps = proj_s.reshape(bs, ss, IN_COLS)
        outs_p.append((
            conv_p[:, SUBLANES - (CONV_W - 1):, :],
            lru_p.reshape(bp, LRU_WIDTH),
            k_p.reshape(bp, WINDOW, N_KV_HEADS, HEAD_DIM),
            v_p.reshape(bp, WINDOW, N_KV_HEADS, HEAD_DIM),
        ))
        k_all = jnp.concatenate([cache_k, knew_s.reshape(bs, ss, N_KV_HEADS, HEAD_DIM)], axis=1)
        v_all = jnp.concatenate([cache_v, ps[:, :, V_COL:].reshape(bs, ss, N_KV_HEADS, HEAD_DIM)], axis=1)
        outs_s.append((
            ps[:, ss - (CONV_W - 1):, 0:LRU_WIDTH],
            lru_s,
            k_all[:, -WINDOW:],
            v_all[:, -WINDOW:],
        ))

    y_prompt = xp.reshape(bp, sp, D_MODEL)
    y_sample = xs.reshape(bs, ss, D_MODEL)
    st_p = [jnp.stack([o[i] for o in outs_p]) for i in range(4)]
    st_s = [jnp.stack([o[i] for o in outs_s]) for i in range(4)]
    return (y_prompt, y_sample, st_p[0], st_p[1], st_p[2], st_p[3], st_s[0], st_s[1], st_s[2], st_s[3])
```

```python
import functools

import jax
import jax.numpy as jnp
from jax import lax
from jax.experimental import pallas as pl
from jax.experimental.pallas import tpu as pltpu

F32 = jnp.float32
BF16 = jnp.bfloat16

D_MODEL = 2048
D_FF = 5504
LRU_WIDTH = 1024
LRU_HEADS = 8
LRU_BLOCK = 128
CONV_W = 4
LRU_C = 8.0
HEAD_DIM = 128
N_HEADS = 8
N_KV_HEADS = 2
GROUP = N_HEADS // N_KV_HEADS
ATTN_WIDTH = N_HEADS * HEAD_DIM
KV_WIDTH = N_KV_HEADS * HEAD_DIM
WINDOW = 128
ROPE_THETA = 10000.0
EPS = 1e-6
NEG = -1e30
PAST_LEN = 16384
IN_COLS = 2 * LRU_WIDTH + ATTN_WIDTH + 2 * KV_WIDTH
Q_COL = 2 * LRU_WIDTH
K_COL = Q_COL + ATTN_WIDTH
V_COL = K_COL + KV_WIDTH
SCALE = HEAD_DIM ** -0.5

SUBLANES = 8
V7X_VMEM_LIMIT_BYTES = 56 * 1024 * 1024

MOD_ROWS = 48
MOD_PROMPT_ROWS = 8
ADALN_TN = 1024
MIX_TS = 256
BF16_ROWS = 16
FFN_TF = 512
FFN_EMIT_TF = 256
FFN_ROWS = 16
FFN_EPILOGUE_ROWS = 128
FFN_ROW_BLOCK = 256
FFN_FINISH_ROWS = 32
INPROJ_TN = IN_COLS // 2
OUTPROJ_TN = D_MODEL // 2
SAMPLE_ATTN_BATCHES = 4
MIXER_TN = 256


def _params(sem, vmem=V7X_VMEM_LIMIT_BYTES):
    return pltpu.CompilerParams(dimension_semantics=sem, vmem_limit_bytes=vmem)


def _sigmoid(x):
    return 1.0 / (1.0 + jnp.exp(-x))


def _normmod(x, g, sc, sh):
    ms = jnp.mean(x * x, axis=-1, keepdims=True)
    xn = x * lax.rsqrt(ms + EPS) * g
    return xn * (1.0 + sc) + sh


def _mod_row(ref, token, row0, batch_shift):
    return ref[pl.ds(row0 + lax.shift_right_logical(token, batch_shift), 1), :]


def _mod_spec(chunk):
    return pl.BlockSpec((MOD_ROWS, D_MODEL), lambda i, j: (0, chunk))


def _adaln_body(c_ref, w_ref, b_ref, o_ref):
    c = c_ref[...]
    s = (c * _sigmoid(c)).astype(BF16)
    o_ref[...] = jnp.dot(s, w_ref[...].astype(BF16), preferred_element_type=F32) + b_ref[...]


def _adaln(c_all, ada_w, ada_b):
    n = ada_w.shape[1]
    return pl.pallas_call(
        _adaln_body,
        grid=(n // ADALN_TN,),
        in_specs=[
            pl.BlockSpec((MOD_ROWS, D_MODEL), lambda j: (0, 0)),
            pl.BlockSpec((D_MODEL, ADALN_TN), lambda j: (0, j)),
            pl.BlockSpec((1, ADALN_TN), lambda j: (0, j)),
        ],
        out_specs=pl.BlockSpec((MOD_ROWS, ADALN_TN), lambda j: (0, j)),
        out_shape=jax.ShapeDtypeStruct((MOD_ROWS, n), F32),
        compiler_params=_params(("arbitrary",)),
        name="adaln",
    )(c_all, ada_w, ada_b.reshape(1, n))


def _ffn_body(x_ref, sh_ref, sc_ref, gt_ref, g_ref, wg_ref, wu_ref, wd_ref, gf_ref, *rest,
              main, rider, tile0, final_norm, emit):
    rest = list(rest)
    x2_ref = rest.pop(0) if rider else None
    o_ref = rest.pop(0)
    if emit:
        wgb_ref, wub_ref, wdb_ref = rest.pop(0), rest.pop(0), rest.pop(0)
    o2_ref = rest.pop(0) if rider else None
    h_ref = rest.pop(0)
    h2_ref = rest.pop(0) if rider else None
    i = pl.program_id(0) + tile0
    j = pl.program_id(1)
    nj = pl.num_programs(1)
    tf = wd_ref.shape[0]

    if emit:
        valid = D_FF - j * tf
        cmask = lax.broadcasted_iota(jnp.int32, (1, tf), 1) < valid
        rmask = lax.broadcasted_iota(jnp.int32, (tf, 1), 0) < valid
        wgb_ref[...] = jnp.where(cmask, wg_ref[...], 0.0).astype(BF16)
        wub_ref[...] = jnp.where(cmask, wu_ref[...], 0.0).astype(BF16)
        wdb_ref[...] = jnp.where(rmask, wd_ref[...], 0.0).astype(BF16)
        wg_ref, wu_ref, wd_ref = wgb_ref, wub_ref, wdb_ref

    def run_group(xr, orf, hr, token0, nr, er, rb, row0, batch_shift):
        tm = xr.shape[0]

        def swiglu(rows):
            h = hr[rows, :].astype(BF16)
            g = jnp.dot(h, wg_ref[...], preferred_element_type=F32)
            u = jnp.dot(h, wu_ref[...], preferred_element_type=F32)
            a = (g * _sigmoid(g) * u).astype(BF16)
            return jnp.dot(a, wd_ref[...], preferred_element_type=F32)

        def prologue(rows, token):
            h = _normmod(xr[rows, :], g_ref[...], _mod_row(sc_ref, token, row0, batch_shift),
                         _mod_row(sh_ref, token, row0, batch_shift))
            hr[rows, :] = h.astype(hr.dtype)

        def epilogue(rows, token, acc):
            y = xr[rows, :] + (0.5 * _mod_row(gt_ref, token, row0, batch_shift)) * acc
            if final_norm:
                ms = jnp.mean(y * y, axis=-1, keepdims=True)
                y = y * lax.rsqrt(ms + EPS) * gf_ref[...]
            orf[rows, :] = y

        if rb:
            blocks = range(0, tm, rb)

            def block_prologue(r0):
                for c in range(r0, r0 + rb, nr):
                    prologue(slice(c, c + nr), token0 + c)

            @pl.when(j == 0)
            def _():
                block_prologue(0)
                for r0 in blocks:
                    orf[r0:r0 + rb, :] = swiglu(slice(r0, r0 + rb))
                    if r0 + rb < tm:
                        block_prologue(r0 + rb)

            @pl.when((j > 0) & (j < nj - 1))
            def _():
                orf[...] += swiglu(slice(None))

            @pl.when(j == nj - 1)
            def _():
                for r0 in blocks:
                    acc = orf[r0:r0 + rb, :] + swiglu(slice(r0, r0 + rb))
                    for c in range(0, rb, FFN_FINISH_ROWS):
                        epilogue(slice(r0 + c, r0 + c + FFN_FINISH_ROWS), token0 + r0 + c,
                                 acc[c:c + FFN_FINISH_ROWS, :])
        else:
            @pl.when(j == 0)
            def _():
                def body(r, carry):
                    rows = pl.ds(pl.multiple_of(r * nr, nr), nr)
                    prologue(rows, token0 + r * nr)
                    orf[rows, :] = jnp.zeros((nr, D_MODEL), F32)
                    return carry

                lax.fori_loop(0, tm // nr, body, 0, unroll=4)

            orf[...] += swiglu(slice(None))

            @pl.when(j == nj - 1)
            def _():
                def body(r, carry):
                    rows = pl.ds(pl.multiple_of(r * er, er), er)
                    epilogue(rows, token0 + r * er, orf[rows, :])
                    return carry

                lax.fori_loop(0, tm // er, body, 0)

    run_group(x_ref, o_ref, h_ref, i * x_ref.shape[0], **main)
    if rider:
        run_group(x2_ref, o2_ref, h2_ref, 0, **rider)


def _ffn_group(tm, nr, row0, rows_per_batch):
    er = min(FFN_EPILOGUE_ROWS, rows_per_batch)
    assert rows_per_batch & (rows_per_batch - 1) == 0 and rows_per_batch % nr == 0 and tm % nr == 0
    assert rows_per_batch % er == 0 and tm % er == 0
    rb = FFN_ROW_BLOCK if rows_per_batch % FFN_ROW_BLOCK == 0 and tm % FFN_ROW_BLOCK == 0 else 0
    return dict(nr=nr, er=er, rb=rb, row0=row0, batch_shift=rows_per_batch.bit_length() - 1)


def _ffn(x, mod, chunks, g, wg, wu, wd, gf, *, tm, nr, row0, rows_per_batch, final_norm, emit, name,
         tf=FFN_TF, tile0=0, ntiles=None, inplace=False, rider=None):
    t = x.shape[0]
    ntiles = t // tm if ntiles is None else ntiles
    assert tf == FFN_TF or emit
    assert rider is None or ntiles == 1
    sub = FFN_TF // tf
    nj_store = pl.cdiv(D_FF, FFN_TF)
    nj = nj_store * sub
    main = _ffn_group(tm, nr, row0, rows_per_batch)
    body = functools.partial(_ffn_body, main=main, rider=None if rider is None else _ffn_group(**rider[1]),
                             tile0=tile0, final_norm=final_norm, emit=emit)

    row = pl.BlockSpec((1, D_MODEL), lambda i, j: (0, 0))
    tile_major = pl.BlockSpec((None, D_MODEL, tf), lambda i, j: (j // sub, 0, j % sub))
    row_major = pl.BlockSpec((tf, D_MODEL), lambda i, j: (j, 0))
    if emit:
        assert ntiles == 1, "bf16 weight tiles are written once, by a single token tile"
        wspecs = [pl.BlockSpec((D_MODEL, tf), lambda i, j: (0, j)),
                  pl.BlockSpec((D_MODEL, tf), lambda i, j: (0, j)),
                  row_major]
    else:
        wspecs = [tile_major, tile_major, row_major]
    xmode = dict(pipeline_mode=pl.Buffered(1)) if ntiles == 1 else {}
    in_specs = [pl.BlockSpec((tm, D_MODEL), lambda i, j: (i + tile0, 0), **xmode),
                _mod_spec(chunks[0]), _mod_spec(chunks[1]), _mod_spec(chunks[2]), row] + wspecs + [row]
    args = [x, mod, mod, mod, g.reshape(1, D_MODEL), wg, wu, wd, gf.reshape(1, D_MODEL)]
    out_specs = [pl.BlockSpec((tm, D_MODEL), lambda i, j: (i + tile0, 0))]
    out_shape = [jax.ShapeDtypeStruct((t, D_MODEL), F32)]
    if emit:
        out_specs += [tile_major, tile_major, row_major]
        out_shape += [jax.ShapeDtypeStruct((nj_store, D_MODEL, FFN_TF), BF16),
                      jax.ShapeDtypeStruct((nj_store, D_MODEL, FFN_TF), BF16),
                      jax.ShapeDtypeStruct((nj_store * FFN_TF, D_MODEL), BF16)]

    def h_scratch(rows, chunk):
        return pltpu.VMEM((rows, D_MODEL), BF16 if chunk % BF16_ROWS == 0 else F32)

    scratch = [h_scratch(tm, nr)]
    if rider is not None:
        x2, grp = rider
        whole = pl.BlockSpec(x2.shape, lambda i, j: (0, 0))
        in_specs.append(whole)
        args.append(x2)
        out_specs.append(whole)
        out_shape.append(jax.ShapeDtypeStruct(x2.shape, F32))
        scratch.append(h_scratch(grp["tm"], grp["nr"]))
    return pl.pallas_call(
        body,
        grid=(ntiles, nj),
        in_specs=in_specs,
        out_specs=out_specs,
        out_shape=out_shape,
        input_output_aliases={0: 0} if inplace else {},
        scratch_shapes=scratch,
        compiler_params=_params(("parallel", "arbitrary")),
        name=name,
    )(*args)


def _inproj_body(x_ref, sh_ref, sc_ref, g_ref, w_ref, o_ref, wb_ref, h_ref, *, nr, row0, batch_shift):
    j = pl.program_id(1)
    tm = x_ref.shape[0]

    @pl.when(j == 0)
    def _():
        g = g_ref[...]

        def body(r, carry):
            rows = pl.ds(pl.multiple_of(r * nr, nr), nr)
            tok = r * nr
            h = _normmod(x_ref[rows, :], g, _mod_row(sc_ref, tok, row0, batch_shift),
                         _mod_row(sh_ref, tok, row0, batch_shift))
            h_ref[rows, :] = h
            return carry

        lax.fori_loop(0, tm // nr, body, 0, unroll=4)

    w = w_ref[...].astype(BF16)
    wb_ref[...] = w
    o_ref[...] = jnp.dot(h_ref[...].astype(BF16), w, preferred_element_type=F32)


def _inproj(x, mod, chunks, g, w_in, *, tn, nr, row0, rows_per_batch, name):
    tm = x.shape[0]
    assert rows_per_batch & (rows_per_batch - 1) == 0 and rows_per_batch % nr == 0 and tm % nr == 0
    body = functools.partial(_inproj_body, nr=nr, row0=row0, batch_shift=rows_per_batch.bit_length() - 1)
    wspec = pl.BlockSpec((D_MODEL, tn), lambda i, j: (0, j))
    return pl.pallas_call(
        body,
        grid=(1, IN_COLS // tn),
        in_specs=[
            pl.BlockSpec((tm, D_MODEL), lambda i, j: (0, 0)),
            _mod_spec(chunks[0]), _mod_spec(chunks[1]),
            pl.BlockSpec((1, D_MODEL), lambda i, j: (0, 0)),
            wspec,
        ],
        out_specs=[pl.BlockSpec((tm, tn), lambda i, j: (0, j)), wspec],
        out_shape=[jax.ShapeDtypeStruct((tm, IN_COLS), F32), jax.ShapeDtypeStruct((D_MODEL, IN_COLS), BF16)],
        scratch_shapes=[pltpu.VMEM((tm, D_MODEL), F32)],
        compiler_params=_params(("arbitrary", "arbitrary")),
        name=name,
    )(x, mod, mod, g.reshape(1, D_MODEL), w_in)


def _outproj_body(m_ref, x_ref, gt_ref, w_ref, o_ref, wb_ref, *, nr, row0, batch_shift):
    tm = x_ref.shape[0]
    w = w_ref[...].astype(BF16)
    wb_ref[...] = w
    o_ref[...] = jnp.dot(m_ref[...].astype(BF16), w, preferred_element_type=F32)

    def body(r, carry):
        rows = pl.ds(pl.multiple_of(r * nr, nr), nr)
        gt = _mod_row(gt_ref, r * nr, row0, batch_shift)
        o_ref[rows, :] = x_ref[rows, :] + gt * o_ref[rows, :]
        return carry

    lax.fori_loop(0, tm // nr, body, 0)


def _outproj(mix, x, mod, gt_c, w_out, *, tn, nr, row0, rows_per_batch, name):
    tm = x.shape[0]
    nn = D_MODEL // tn
    assert rows_per_batch & (rows_per_batch - 1) == 0 and rows_per_batch % nr == 0 and tm % nr == 0
    body = functools.partial(_outproj_body, nr=nr, row0=row0, batch_shift=rows_per_batch.bit_length() - 1)
    wspec = pl.BlockSpec((D_MODEL, tn), lambda i, j: (0, j))
    tile = pl.BlockSpec((tm, tn), lambda i, j: (0, j))
    return pl.pallas_call(
        body,
        grid=(1, nn),
        in_specs=[
            pl.BlockSpec((tm, D_MODEL), lambda i, j: (0, 0)),
            tile,
            pl.BlockSpec((MOD_ROWS, tn), lambda i, j: (0, gt_c * nn + j)),
            wspec,
        ],
        out_specs=[tile, wspec],
        out_shape=[jax.ShapeDtypeStruct((tm, D_MODEL), F32), jax.ShapeDtypeStruct((D_MODEL, D_MODEL), BF16)],
        compiler_params=_params(("arbitrary", "arbitrary")),
        name=name,
    )(mix, x, mod, w_out)


def _rope(x, cos, sin_signed):
    return x * cos + pltpu.roll(x, HEAD_DIM // 2, 1) * sin_signed


def _lru_gates(xc_ref, a_ref, wa_ref, ba_ref, wx_ref, bx_ref, lam_ref, midway=None):
    nlam = -lam_ref[...]
    softplus = jnp.maximum(nlam, 0.0) + jnp.log1p(jnp.exp(-jnp.abs(nlam)))
    rate = -LRU_C * softplus
    for hh in range(LRU_HEADS):
        if midway is not None:
            midway()
        cols = slice(hh * LRU_BLOCK, (hh + 1) * LRU_BLOCK)
        xc = xc_ref[:, cols]
        xcb = xc.astype(BF16)
        ra = jnp.dot(xcb, wa_ref[hh].astype(BF16), preferred_element_type=F32) + ba_ref[:, cols]
        rx = jnp.dot(xcb, wx_ref[hh].astype(BF16), preferred_element_type=F32) + bx_ref[:, cols]
        r = _sigmoid(ra)
        gi = _sigmoid(rx)
        a = jnp.exp(r * rate[:, cols])
        a_ref[:, cols] = a
        xc_ref[:, cols] = jnp.sqrt(1.0 - a * a) * (gi * xc)


def _scan_group(a, u, carry, row):
    for s in (1, 2, 4):
        a_sh = pltpu.roll(a, s, 0)
        u_sh = pltpu.roll(u, s, 0)
        m = row >= s
        u = jnp.where(m, a * u_sh + u, u)
        a = jnp.where(m, a * a_sh, a)
    return a * carry + u


def _softmax_pv(s, mask, sink, v):
    s = jnp.where(mask, s, NEG)
    m = jnp.maximum(jnp.max(s, axis=-1, keepdims=True), sink)
    p = jnp.exp(s - m)
    den = jnp.sum(p, axis=-1, keepdims=True) + jnp.exp(sink - m)
    return jnp.dot(p.astype(BF16), v, preferred_element_type=F32) / den


def _sink_column(sink_ref, kh, rows_per_head):
    n = GROUP * rows_per_head
    ri = lax.broadcasted_iota(jnp.int32, (n, 1), 0)
    col = jnp.full((n, 1), sink_ref[kh * GROUP + GROUP - 1], F32)
    for g in range(GROUP - 2, -1, -1):
        col = jnp.where(ri < (g + 1) * rows_per_head, sink_ref[kh * GROUP + g], col)
    return col


def _mix_tile(proj, mix, cs_ref, sn_ref, cw_ref, cb_ref, wa_ref, ba_ref, wx_ref, bx_ref, lam_ref, sink_ref,
              xpad, hc, kpad, vpad, a_s, u_s, klast, *, ts, pos0, mxu_fill):
    fill = iter(mxu_fill)

    def take(n):
        for _ in range(n):
            f = next(fill, None)
            if f is not None:
                f()

    w = LRU_WIDTH
    xpad[SUBLANES:SUBLANES + ts, :] = proj[:, 0:w]
    half = ts // 2
    for r0 in (0, half):
        take(1)
        xc = cb_ref[...]
        for jj in range(CONV_W):
            off = r0 + SUBLANES - (CONV_W - 1) + jj
            xc = xc + xpad[off:off + half, :] * cw_ref[jj:jj + 1, :]
        u_s[r0:r0 + half, :] = xc
    xpad[0:SUBLANES, :] = xpad[ts:ts + SUBLANES, :]

    _lru_gates(u_s, a_s, wa_ref, ba_ref, wx_ref, bx_ref, lam_ref, midway=lambda: take(1))

    row = lax.broadcasted_iota(jnp.int32, (SUBLANES, w), 0)
    carry = hc[...]
    ngroups = ts // SUBLANES
    for g in range(ngroups):
        if g % (ngroups // 8) == 0:
            take(1)
        rows = slice(g * SUBLANES, (g + 1) * SUBLANES)
        h = _scan_group(a_s[rows, :], u_s[rows, :], carry, row)
        a_s[rows, :] = h
        carry = jnp.broadcast_to(h[SUBLANES - 1:SUBLANES, :], (SUBLANES, w))
    hc[...] = carry
    for r0 in (0, half):
        take(1)
        rows = slice(r0, r0 + half)
        mix[rows, 0:w] = (a_s[rows, :] * jax.nn.gelu(proj[rows, w:2 * w])).astype(BF16)

    take(1)
    cos = cs_ref[...]
    sin = sn_ref[...]
    for kh in range(N_KV_HEADS):
        cols = slice(kh * HEAD_DIM, (kh + 1) * HEAD_DIM)
        kr = _rope(proj[:, K_COL + kh * HEAD_DIM:K_COL + (kh + 1) * HEAD_DIM], cos, sin)
        kpad[WINDOW:WINDOW + ts, cols] = kr.astype(BF16)
        klast[:, cols] = kr[ts - WINDOW:, :]
    vpad[WINDOW:WINDOW + ts, :] = proj[:, V_COL:V_COL + KV_WIDTH].astype(BF16)

    nq = GROUP * WINDOW
    r_i = lax.broadcasted_iota(jnp.int32, (nq, 2 * WINDOW), 0) & (WINDOW - 1)
    c_i = lax.broadcasted_iota(jnp.int32, (nq, 2 * WINDOW), 1)
    band = (c_i >= r_i) & (c_i <= r_i + WINDOW)
    for n in range(ts // WINDOW):
        qrows = slice(n * WINDOW, (n + 1) * WINDOW)
        kpos0 = pos0 + (n - 1) * WINDOW
        mask = band & (c_i + kpos0 >= 0)
        for kh in range(N_KV_HEADS):
            cols = slice(kh * HEAD_DIM, (kh + 1) * HEAD_DIM)
            qs = []
            for g in range(GROUP):
                hd = kh * GROUP + g
                qh = proj[qrows, Q_COL + hd * HEAD_DIM:Q_COL + (hd + 1) * HEAD_DIM]
                qs.append(_rope(qh, cos[qrows, :], sin[qrows, :]).astype(BF16))
            q4 = jnp.concatenate(qs, axis=0)
            kk = kpad[n * WINDOW:(n + 2) * WINDOW, cols]
            vv = vpad[n * WINDOW:(n + 2) * WINDOW, cols]
            s = lax.dot_general(q4, kk, (((1,), (1,)), ((), ())), preferred_element_type=F32) * SCALE
            take(1)
            o = _softmax_pv(s, mask, _sink_column(sink_ref, kh, WINDOW), vv)
            for g in range(GROUP):
                hd = kh * GROUP + g
                mix[qrows, w + hd * HEAD_DIM:w + (hd + 1) * HEAD_DIM] = (
                    o[g * WINDOW:(g + 1) * WINDOW, :].astype(BF16))

    take(len(mxu_fill))
    kpad[0:WINDOW, :] = kpad[ts:ts + WINDOW, :]
    vpad[0:WINDOW, :] = vpad[ts:ts + WINDOW, :]


def _mixer_body(xa_ref, xc_ref, sh_ref, sc_ref, gt_ref, gm_ref, win_ref, wout_ref, cs_ref, sn_ref,
                cw_ref, cb_ref, wa_ref, ba_ref, wx_ref, bx_ref, lam_ref, sink_ref,
                o_ref, lru_ref, k_ref, v_ref, conv_ref,
                proj0, proj1, mix0, mix1, h_s, xpad, hc, kpad, vpad, a_s, u_s, klast,
                *, ts, nt, ntiles, batch_shift):
    s = pl.program_id(0)
    tile_a = jnp.minimum(s, ntiles - 1)
    tile_b = jnp.clip(s - 1, 0, ntiles - 1)
    tile_c = jnp.clip(s - 2, 0, ntiles - 1)
    t_in = tile_b & (nt - 1)
    proj = (proj0, proj1)
    mix = (mix0, mix1)

    def out_chunks(mix_r):
        gt = _mod_row(gt_ref, tile_c * ts, 0, batch_shift)

        def chunk(k):
            cols = slice(k * MIXER_TN, (k + 1) * MIXER_TN)

            def run():
                d = jnp.dot(mix_r[...], wout_ref[:, cols], preferred_element_type=F32)
                o_ref[:, cols] = xc_ref[:, cols] + gt[:, cols] * d
            return run
        return [chunk(k) for k in range(D_MODEL // MIXER_TN)]

    def in_chunks(proj_w):
        def chunk(k):
            cols = slice(k * MIXER_TN, (k + 1) * MIXER_TN)

            def run():
                if k == 0:
                    g = gm_ref[...]
                    sc = _mod_row(sc_ref, tile_a * ts, 0, batch_shift)
                    sh = _mod_row(sh_ref, tile_a * ts, 0, batch_shift)
                    for r in range(ts // FFN_ROWS):
                        rows = slice(r * FFN_ROWS, (r + 1) * FFN_ROWS)
                        h_s[rows, :] = _normmod(xa_ref[rows, :], g, sc, sh).astype(BF16)
                proj_w[:, cols] = jnp.dot(h_s[...], win_ref[:, cols], preferred_element_type=F32)
            return run
        return [chunk(k) for k in range(IN_COLS // MIXER_TN)]

    @pl.when(s == 0)
    def _():
        mix1[...] = jnp.zeros(mix1.shape, BF16)
        for run in in_chunks(proj[0]):
            run()

    @pl.when(s == ntiles + 1)
    def _():
        for run in out_chunks(mix[(ntiles - 1) & 1]):
            run()

    @pl.when(t_in == 0)
    def _():
        xpad[0:SUBLANES, :] = jnp.zeros((SUBLANES, LRU_WIDTH), F32)
        hc[...] = jnp.zeros((SUBLANES, LRU_WIDTH), F32)
        kpad[0:WINDOW, :] = jnp.zeros((WINDOW, KV_WIDTH), BF16)
        vpad[0:WINDOW, :] = jnp.zeros((WINDOW, KV_WIDTH), BF16)

    def step(par):
        proj_r, mix_w = proj[1 - par], mix[1 - par]
        _mix_tile(proj_r, mix_w, cs_ref, sn_ref, cw_ref, cb_ref, wa_ref, ba_ref, wx_ref, bx_ref, lam_ref,
                  sink_ref, xpad, hc, kpad, vpad, a_s, u_s, klast, ts=ts, pos0=t_in * ts,
                  mxu_fill=in_chunks(proj[par]) + out_chunks(mix[par]))
        lru_ref[0] = hc[0:1, :]
        k_ref[0] = klast[...]
        v_ref[0] = proj_r[ts - WINDOW:, V_COL:V_COL + KV_WIDTH]
        conv_ref[0] = xpad[0:SUBLANES, :]

    for par in (0, 1):
        @pl.when((s >= 1) & (s <= ntiles) & ((s & 1) == par))
        def _(par=par):
            step(par)


def _mixer_prompt(x, mod, g_mix, w_in_b, w_out_b, cos, sin, p, *, batch, seq):
    ts = MIX_TS
    nt = seq // ts
    ntiles = batch * nt
    assert nt & (nt - 1) == 0 and seq & (seq - 1) == 0
    w = LRU_WIDTH
    body = functools.partial(_mixer_body, ts=ts, nt=nt, ntiles=ntiles, batch_shift=seq.bit_length() - 1)

    def tile_a(s):
        return jnp.minimum(s, ntiles - 1)

    def tile_b(s):
        return jnp.clip(s - 1, 0, ntiles - 1)

    def tile_c(s):
        return jnp.clip(s - 2, 0, ntiles - 1)

    def whole(shape, **kw):
        return pl.BlockSpec(shape, lambda s: (0,) * len(shape), **kw)

    def mspec(c):
        return pl.BlockSpec((MOD_ROWS, D_MODEL), lambda s: (0, c))

    def state(shape):
        return pl.BlockSpec((1,) + shape, lambda s: (tile_b(s) // nt, 0, 0))

    once = dict(pipeline_mode=pl.Buffered(1))
    return pl.pallas_call(
        body,
        grid=(ntiles + 2,),
        in_specs=[
            pl.BlockSpec((ts, D_MODEL), lambda s: (tile_a(s), 0)),
            pl.BlockSpec((ts, D_MODEL), lambda s: (tile_c(s), 0)),
            mspec(3), mspec(4), mspec(5),
            whole((1, D_MODEL)),
            whole((D_MODEL, IN_COLS), **once),
            whole((D_MODEL, D_MODEL), **once),
            pl.BlockSpec((ts, HEAD_DIM), lambda s: (tile_b(s) % nt, 0)),
            pl.BlockSpec((ts, HEAD_DIM), lambda s: (tile_b(s) % nt, 0)),
            whole((CONV_W, w)), whole((1, w)),
            whole((LRU_HEADS, LRU_BLOCK, LRU_BLOCK)), whole((1, w)),
            whole((LRU_HEADS, LRU_BLOCK, LRU_BLOCK)), whole((1, w)),
            whole((1, w)),
            pl.BlockSpec(memory_space=pltpu.SMEM),
        ],
        out_specs=[
            pl.BlockSpec((ts, D_MODEL), lambda s: (tile_c(s), 0)),
            state((1, w)), state((WINDOW, KV_WIDTH)), state((WINDOW, KV_WIDTH)), state((SUBLANES, w)),
        ],
        out_shape=[
            jax.ShapeDtypeStruct((batch * seq, D_MODEL), F32),
            jax.ShapeDtypeStruct((batch, 1, w), F32),
            jax.ShapeDtypeStruct((batch, WINDOW, KV_WIDTH), F32),
            jax.ShapeDtypeStruct((batch, WINDOW, KV_WIDTH), F32),
            jax.ShapeDtypeStruct((batch, SUBLANES, w), F32),
        ],
        scratch_shapes=[
            pltpu.VMEM((ts, IN_COLS), F32), pltpu.VMEM((ts, IN_COLS), F32),
            pltpu.VMEM((ts, D_MODEL), BF16), pltpu.VMEM((ts, D_MODEL), BF16),
            pltpu.VMEM((ts, D_MODEL), BF16),
            pltpu.VMEM((SUBLANES + ts, w), F32),
            pltpu.VMEM((SUBLANES, w), F32),
            pltpu.VMEM((WINDOW + ts, KV_WIDTH), BF16),
            pltpu.VMEM((WINDOW + ts, KV_WIDTH), BF16),
            pltpu.VMEM((ts, w), F32),
            pltpu.VMEM((ts, w), F32),
            pltpu.VMEM((WINDOW, KV_WIDTH), F32),
        ],
        compiler_params=_params(("arbitrary",)),
        name="mixer_prompt",
    )(x, x, mod, mod, mod, g_mix.reshape(1, D_MODEL), w_in_b, w_out_b, cos, sin,
      p["conv_w"], p["conv_b"], p["wa"], p["ba"], p["wx"], p["bx"], p["lam"], p["sinks"])


def _mix_sample_body(proj_ref, prev_ref, h0_ref, ck_ref, cv_ref, cs_ref, sn_ref, cw_ref, cb_ref,
                     wa_ref, ba_ref, wx_ref, bx_ref, lam_ref, sink_ref,
                     mix_ref, lru_ref, knew_ref,
                     a_s, u_s, q_s, *, batch, seq):
    w = LRU_WIDTH
    row = lax.broadcasted_iota(jnp.int32, (SUBLANES, w), 0)

    def group(b):
        return pl.ds(pl.multiple_of(b * SUBLANES, SUBLANES), SUBLANES)

    def conv_body(b, carry):
        rows = group(b)
        cur = proj_ref[rows, 0:w]
        prev = prev_ref[rows, :]
        xc = cb_ref[...]
        for jj in range(CONV_W):
            d = CONV_W - 1 - jj
            if d == 0:
                term = cur
            else:
                term = jnp.where(row >= d, pltpu.roll(cur, d, 0), pltpu.roll(prev, d, 0))
            xc = xc + term * cw_ref[jj:jj + 1, :]
        u_s[rows, :] = xc
        return carry

    lax.fori_loop(0, batch, conv_body, 0, unroll=4)

    _lru_gates(u_s, a_s, wa_ref, ba_ref, wx_ref, bx_ref, lam_ref)

    def scan_body(b, carry):
        rows = group(b)
        h0 = jnp.broadcast_to(h0_ref[pl.ds(b, 1), :], (SUBLANES, w))
        h = _scan_group(a_s[rows, :], u_s[rows, :], h0, row)
        a_s[rows, :] = h
        lru_ref[pl.ds(b, 1), :] = h[SUBLANES - 1:SUBLANES, :]
        return carry

    lax.fori_loop(0, batch, scan_body, 0, unroll=4)
    mix_ref[:, 0:w] = a_s[...] * jax.nn.gelu(proj_ref[:, w:2 * w])

    cos = cs_ref[...]
    sin = sn_ref[...]
    for hd in range(N_HEADS):
        cols = slice(hd * HEAD_DIM, (hd + 1) * HEAD_DIM)
        q_s[:, cols] = _rope(proj_ref[:, Q_COL + hd * HEAD_DIM:Q_COL + (hd + 1) * HEAD_DIM], cos, sin)
    for kh in range(N_KV_HEADS):
        cols = slice(kh * HEAD_DIM, (kh + 1) * HEAD_DIM)
        knew_ref[:, cols] = _rope(proj_ref[:, K_COL + kh * HEAD_DIM:K_COL + (kh + 1) * HEAD_DIM], cos, sin)

    nq = GROUP * seq
    nk = 2 * WINDOW
    r_i = lax.broadcasted_iota(jnp.int32, (nq, nk), 0) & (seq - 1)
    c_i = lax.broadcasted_iota(jnp.int32, (nq, nk), 1)
    mask = (c_i >= r_i) & (c_i <= r_i + WINDOW)
    pad = jnp.zeros((nk - WINDOW - seq, HEAD_DIM), F32)

    def attn_body(bg, carry):
        work = [(bg * SAMPLE_ATTN_BATCHES + k, kh) for k in range(SAMPLE_ATTN_BATCHES)
                for kh in range(N_KV_HEADS)]
        scores = []
        for b, kh in work:
            rows = group(b)
            cols = slice(kh * HEAD_DIM, (kh + 1) * HEAD_DIM)
            q4 = jnp.concatenate(
                [q_s[rows, (kh * GROUP + g) * HEAD_DIM:(kh * GROUP + g + 1) * HEAD_DIM] for g in range(GROUP)],
                axis=0).astype(BF16)
            cached = pl.ds(kh, WINDOW, stride=N_KV_HEADS)
            kk = jnp.concatenate([ck_ref[b, cached, :], knew_ref[rows, cols], pad], axis=0).astype(BF16)
            scores.append(lax.dot_general(q4, kk, (((1,), (1,)), ((), ())), preferred_element_type=F32) * SCALE)
        for (b, kh), s in zip(work, scores):
            rows = group(b)
            cached = pl.ds(kh, WINDOW, stride=N_KV_HEADS)
            vv = jnp.concatenate(
                [cv_ref[b, cached, :], proj_ref[rows, V_COL + kh * HEAD_DIM:V_COL + (kh + 1) * HEAD_DIM], pad],
                axis=0).astype(BF16)
            o = _softmax_pv(s, mask, _sink_column(sink_ref, kh, seq), vv)
            for g in range(GROUP):
                hd = kh * GROUP + g
                mix_ref[rows, w + hd * HEAD_DIM:w + (hd + 1) * HEAD_DIM] = o[g * seq:(g + 1) * seq, :]
        return carry

    lax.fori_loop(0, batch // SAMPLE_ATTN_BATCHES, attn_body, 0)


def _mix_sample(proj, prev, h0, cache_k, cache_v, cos, sin, p, *, batch, seq):
    assert seq == SUBLANES, "each sample batch must be exactly one sublane group"
    assert batch % SAMPLE_ATTN_BATCHES == 0
    t = batch * seq
    body = functools.partial(_mix_sample_body, batch=batch, seq=seq)
    vmem = pl.BlockSpec(memory_space=pltpu.VMEM)
    return pl.pallas_call(
        body,
        in_specs=[vmem] * 14 + [pl.BlockSpec(memory_space=pltpu.SMEM)],
        out_specs=[vmem, vmem, vmem],
        out_shape=[
            jax.ShapeDtypeStruct((t, D_MODEL), F32),
            jax.ShapeDtypeStruct((batch, LRU_WIDTH), F32),
            jax.ShapeDtypeStruct((t, KV_WIDTH), F32),
        ],
        scratch_shapes=[
            pltpu.VMEM((t, LRU_WIDTH), F32),
            pltpu.VMEM((t, LRU_WIDTH), F32),
            pltpu.VMEM((t, ATTN_WIDTH), F32),
        ],
        compiler_params=pltpu.CompilerParams(vmem_limit_bytes=V7X_VMEM_LIMIT_BYTES),
        name="mix_sample",
    )(proj, prev, h0, cache_k, cache_v, cos, sin, p["conv_w"], p["conv_b"], p["wa"], p["ba"], p["wx"], p["bx"],
      p["lam"], p["sinks"])


def _layer(w, l):
    return w.reshape(w.shape[1:]) if w.shape[0] == 1 else w[l]


def _rope_tables(pos):
    half = HEAD_DIM // 2
    inv = ROPE_THETA ** (-jnp.arange(half, dtype=F32) / half)
    ang = pos.astype(F32)[:, None] * inv[None, :]
    cos = jnp.cos(ang)
    sin = jnp.sin(ang)
    return jnp.concatenate([cos, cos], axis=-1), jnp.concatenate([-sin, sin], axis=-1)


def kernel(x_prompt, x_sample, c_prompt, c_sample, state_conv, state_lru, cache_k_win, cache_v_win, ada_w, ada_b, norm_ffn1, norm_mix, norm_ffn2, ffn1_w_gate, ffn1_w_up, ffn1_w_down, w_in, conv_w, conv_b, lru_w_a, lru_b_a, lru_w_x, lru_b_x, lru_lambda, attn_sinks, w_out, ffn2_w_gate, ffn2_w_up, ffn2_w_down, norm_final):
    bp, sp, _ = x_prompt.shape
    bs, ss, _ = x_sample.shape
    depth = ada_w.shape[0]
    tp, tsn = bp * sp, bs * ss
    assert bp <= MOD_PROMPT_ROWS and bp + bs <= MOD_ROWS
    tm_p = 1024
    assert sp % tm_p == 0

    xp = x_prompt.reshape(tp, D_MODEL)
    xs = x_sample.reshape(tsn, D_MODEL)
    c_all = jnp.concatenate(
        [c_prompt, jnp.zeros((MOD_PROMPT_ROWS - bp, D_MODEL), F32), c_sample,
         jnp.zeros((MOD_ROWS - MOD_PROMPT_ROWS - bs, D_MODEL), F32)], axis=0)
    cos_p, sin_p = _rope_tables(jnp.arange(sp))
    cos_s, sin_s = _rope_tables(PAST_LEN + jnp.arange(ss))
    cos_s = jnp.tile(cos_s, (bs, 1))
    sin_s = jnp.tile(sin_s, (bs, 1))

    outs_p, outs_s = [], []
    for l in range(depth):
        last = l == depth - 1
        L = functools.partial(_layer, l=l)
        cache_k, cache_v = L(cache_k_win), L(cache_v_win)
        mod = _adaln(c_all, L(ada_w), L(ada_b))
        p = dict(conv_w=L(conv_w), conv_b=L(conv_b).reshape(1, LRU_WIDTH), wa=L(lru_w_a),
                 ba=L(lru_b_a).reshape(1, LRU_WIDTH), wx=L(lru_w_x), bx=L(lru_b_x).reshape(1, LRU_WIDTH),
                 lam=L(lru_lambda).reshape(1, LRU_WIDTH), sinks=L(attn_sinks))
        ffn1 = (L(norm_ffn1), L(ffn1_w_gate), L(ffn1_w_up), L(ffn1_w_down), norm_final)
        ffn2 = (L(norm_ffn2), L(ffn2_w_gate), L(ffn2_w_up), L(ffn2_w_down), norm_final)
        ffn_p = dict(tm=tm_p, nr=FFN_ROWS, row0=0, rows_per_batch=sp)
        ffn_s = dict(tm=tsn, nr=ss, row0=MOD_PROMPT_ROWS, rows_per_batch=ss)
        proj_s_kw = dict(nr=ss, row0=MOD_PROMPT_ROWS, rows_per_batch=ss)

        g, wg, wu, wd, gf = ffn1
        xs, wg_b, wu_b, wd_b = _ffn(xs, mod, (0, 1, 2), g, wg, wu, wd, gf, emit=True, final_norm=False,
                                    name="ffn1_sample", **ffn_s)
        xp, = _ffn(xp, mod, (0, 1, 2), g, wg_b, wu_b, wd_b, gf, emit=False, final_norm=False,
                   name="ffn1_prompt", **ffn_p)

        proj_s, w_in_b = _inproj(xs, mod, (3, 4), L(norm_mix), L(w_in), tn=INPROJ_TN, name="inproj_sample",
                                 **proj_s_kw)
        prev_s = jnp.pad(L(state_conv), ((0, 0), (SUBLANES - (CONV_W - 1), 0), (0, 0))).reshape(tsn, LRU_WIDTH)
        mix_s, lru_s, knew_s = _mix_sample(
            proj_s, prev_s, L(state_lru), cache_k.reshape(bs, WINDOW * N_KV_HEADS, HEAD_DIM),
            cache_v.reshape(bs, WINDOW * N_KV_HEADS, HEAD_DIM), cos_s, sin_s, p, batch=bs, seq=ss)
        xs, w_out_b = _outproj(mix_s, xs, mod, 5, L(w_out), tn=OUTPROJ_TN, name="outproj_sample", **proj_s_kw)
        xp, lru_p, k_p, v_p, conv_p = _mixer_prompt(xp, mod, L(norm_mix), w_in_b, w_out_b, cos_p, sin_p, p,
                                                    batch=bp, seq=sp)

        g, wg, wu, wd, gf = ffn2
        xp, wg_b, wu_b, wd_b, xs = _ffn(xp, mod, (6, 7, 8), g, wg, wu, wd, gf, emit=True, tf=FFN_EMIT_TF,
                                        ntiles=1, inplace=True, rider=(xs, ffn_s), final_norm=last,
                                        name="ffn2_head", **ffn_p)
        xp, = _ffn(xp, mod, (6, 7, 8), g, wg_b, wu_b, wd_b, gf, emit=False, tile0=1, ntiles=tp // tm_p - 1,
                   inplace=True, final_norm=last, name="ffn2_prompt", **ffn_p)

        ps = proj_s.reshape(bs, ss, IN_COLS)
        outs_p.append((
            conv_p[:, SUBLANES - (CONV_W - 1):, :],
            lru_p.reshape(bp, LRU_WIDTH),
            k_p.reshape(bp, WINDOW, N_KV_HEADS, HEAD_DIM),
            v_p.reshape(bp, WINDOW, N_KV_HEADS, HEAD_DIM),
        ))
        k_all = jnp.concatenate([cache_k, knew_s.reshape(bs, ss, N_KV_HEADS, HEAD_DIM)], axis=1)
        v_all = jnp.concatenate([cache_v, ps[:, :, V_COL:].reshape(bs, ss, N_KV_HEADS, HEAD_DIM)], axis=1)
        outs_s.append((
            ps[:, ss - (CONV_W - 1):, 0:LRU_WIDTH],
            lru_s,
            k_all[:, -WINDOW:],
            v_all[:, -WINDOW:],
        ))

    y_prompt = xp.reshape(bp, sp, D_MODEL)
    y_sample = xs.reshape(bs, ss, D_MODEL)
    st_p = [jnp.stack([o[i] for o in outs_p]) for i in range(4)]
    st_s = [jnp.stack([o[i] for o in outs_s]) for i in range(4)]
    return (y_prompt, y_sample, st_p[0], st_p[1], st_p[2], st_p[3], st_s[0], st_s[1], st_s[2], st_s[3])
```

```python
import functools

import jax
import jax.numpy as jnp
from jax import lax
from jax.experimental import pallas as pl
from jax.experimental.pallas import tpu as pltpu

F32 = jnp.float32
BF16 = jnp.bfloat16

D_MODEL = 2048
D_FF = 5504
LRU_WIDTH = 1024
LRU_HEADS = 8
LRU_BLOCK = 128
CONV_W = 4
LRU_C = 8.0
HEAD_DIM = 128
N_HEADS = 8
N_KV_HEADS = 2
GROUP = N_HEADS // N_KV_HEADS
ATTN_WIDTH = N_HEADS * HEAD_DIM
KV_WIDTH = N_KV_HEADS * HEAD_DIM
WINDOW = 128
ROPE_THETA = 10000.0
EPS = 1e-6
NEG = -1e30
PAST_LEN = 16384
IN_COLS = 2 * LRU_WIDTH + ATTN_WIDTH + 2 * KV_WIDTH
Q_COL = 2 * LRU_WIDTH
K_COL = Q_COL + ATTN_WIDTH
V_COL = K_COL + KV_WIDTH
SCALE = HEAD_DIM ** -0.5

SUBLANES = 8
V7X_VMEM_LIMIT_BYTES = 56 * 1024 * 1024

MOD_ROWS = 48
MOD_PROMPT_ROWS = 8
ADALN_TN = 1024
ADALN_BUFS = 3
MIX_TS = 256
BF16_ROWS = 16
FFN_TF = 512
FFN_EMIT_TF = 256
FFN_ROWS = 16
FFN_EPILOGUE_ROWS = 128
FFN_ROW_BLOCK = 256
FFN_FINISH_ROWS = 32
INPROJ_TN = IN_COLS // 2
OUTPROJ_TN = D_MODEL // 2
SAMPLE_ATTN_BATCHES = 4
MIXER_TN = 256


def _params(sem, vmem=V7X_VMEM_LIMIT_BYTES):
    return pltpu.CompilerParams(dimension_semantics=sem, vmem_limit_bytes=vmem)


def _sigmoid(x):
    return 1.0 / (1.0 + jnp.exp(-x))


def _normmod(x, g, sc, sh):
    ms = jnp.mean(x * x, axis=-1, keepdims=True)
    xn = x * lax.rsqrt(ms + EPS) * g
    return xn * (1.0 + sc) + sh


def _mod_row(ref, token, row0, batch_shift):
    return ref[pl.ds(row0 + lax.shift_right_logical(token, batch_shift), 1), :]


def _mod_spec(chunk):
    return pl.BlockSpec((MOD_ROWS, D_MODEL), lambda i, j: (0, chunk))


def _adaln_body(c_ref, w_hbm, b_ref, o_ref, wbuf, sem, *, ntiles):
    c = c_ref[...]
    s = (c * _sigmoid(c)).astype(BF16)

    def columns(t):
        start = t * ADALN_TN
        return pl.ds(start if isinstance(t, int) else pl.multiple_of(start, ADALN_TN), ADALN_TN)

    def tile_copy(t):
        slot = t % ADALN_BUFS
        return pltpu.make_async_copy(w_hbm.at[:, columns(t)], wbuf.at[slot], sem.at[slot])

    for t in range(ADALN_BUFS - 1):
        tile_copy(t).start()

    def body(t, carry):
        @pl.when(t + ADALN_BUFS - 1 < ntiles)
        def _():
            tile_copy(t + ADALN_BUFS - 1).start()

        tile_copy(t).wait()
        cols = columns(t)
        w = wbuf[t % ADALN_BUFS].astype(BF16)
        o_ref[:, cols] = jnp.dot(s, w, preferred_element_type=F32) + b_ref[:, cols]
        return carry

    lax.fori_loop(0, ntiles, body, 0)


def _adaln(c_all, ada_w, ada_b):
    n = ada_w.shape[1]
    ntiles = n // ADALN_TN
    assert n % ADALN_TN == 0 and ntiles >= ADALN_BUFS
    vmem = pl.BlockSpec(memory_space=pltpu.VMEM)
    return pl.pallas_call(
        functools.partial(_adaln_body, ntiles=ntiles),
        in_specs=[vmem, pl.BlockSpec(memory_space=pl.ANY), vmem],
        out_specs=vmem,
        out_shape=jax.ShapeDtypeStruct((MOD_ROWS, n), F32),
        scratch_shapes=[pltpu.VMEM((ADALN_BUFS, D_MODEL, ADALN_TN), F32),
                        pltpu.SemaphoreType.DMA((ADALN_BUFS,))],
        compiler_params=pltpu.CompilerParams(vmem_limit_bytes=V7X_VMEM_LIMIT_BYTES),
        name="adaln",
    )(c_all, ada_w, ada_b.reshape(1, n))


def _ffn_body(x_ref, sh_ref, sc_ref, gt_ref, g_ref, wg_ref, wu_ref, wd_ref, gf_ref, *rest,
              nr, er, rb, row0, batch_shift, tile0, final_norm, emit):
    if emit:
        o_ref, wgb_ref, wub_ref, wdb_ref, h_ref = rest
    else:
        o_ref, h_ref = rest
    i = pl.program_id(0) + tile0
    j = pl.program_id(1)
    nj = pl.num_programs(1)
    tm = x_ref.shape[0]
    tf = wd_ref.shape[0]

    if emit:
        valid = D_FF - j * tf
        cmask = lax.broadcasted_iota(jnp.int32, (1, tf), 1) < valid
        rmask = lax.broadcasted_iota(jnp.int32, (tf, 1), 0) < valid
        wgb_ref[...] = jnp.where(cmask, wg_ref[...], 0.0).astype(BF16)
        wub_ref[...] = jnp.where(cmask, wu_ref[...], 0.0).astype(BF16)
        wdb_ref[...] = jnp.where(rmask, wd_ref[...], 0.0).astype(BF16)
        wg_ref, wu_ref, wd_ref = wgb_ref, wub_ref, wdb_ref

    def swiglu(rows):
        h = h_ref[rows, :].astype(BF16)
        g = jnp.dot(h, wg_ref[...], preferred_element_type=F32)
        u = jnp.dot(h, wu_ref[...], preferred_element_type=F32)
        a = (g * _sigmoid(g) * u).astype(BF16)
        return jnp.dot(a, wd_ref[...], preferred_element_type=F32)

    def prologue(rows, token):
        h = _normmod(x_ref[rows, :], g_ref[...], _mod_row(sc_ref, token, row0, batch_shift),
                     _mod_row(sh_ref, token, row0, batch_shift))
        h_ref[rows, :] = h.astype(h_ref.dtype)

    def epilogue(rows, token, acc):
        y = x_ref[rows, :] + (0.5 * _mod_row(gt_ref, token, row0, batch_shift)) * acc
        if final_norm:
            ms = jnp.mean(y * y, axis=-1, keepdims=True)
            y = y * lax.rsqrt(ms + EPS) * gf_ref[...]
        o_ref[rows, :] = y

    if rb:
        blocks = range(0, tm, rb)

        def block_prologue(r0):
            for c in range(r0, r0 + rb, nr):
                prologue(slice(c, c + nr), i * tm + c)

        @pl.when(j == 0)
        def _():
            block_prologue(0)
            for r0 in blocks:
                o_ref[r0:r0 + rb, :] = swiglu(slice(r0, r0 + rb))
                if r0 + rb < tm:
                    block_prologue(r0 + rb)

        @pl.when((j > 0) & (j < nj - 1))
        def _():
            o_ref[...] += swiglu(slice(None))

        @pl.when(j == nj - 1)
        def _():
            for r0 in blocks:
                acc = o_ref[r0:r0 + rb, :] + swiglu(slice(r0, r0 + rb))
                for c in range(0, rb, FFN_FINISH_ROWS):
                    epilogue(slice(r0 + c, r0 + c + FFN_FINISH_ROWS), i * tm + r0 + c,
                             acc[c:c + FFN_FINISH_ROWS, :])
    else:
        @pl.when(j == 0)
        def _():
            def body(r, carry):
                rows = pl.ds(pl.multiple_of(r * nr, nr), nr)
                prologue(rows, i * tm + r * nr)
                o_ref[rows, :] = jnp.zeros((nr, D_MODEL), F32)
                return carry

            lax.fori_loop(0, tm // nr, body, 0, unroll=4)

        o_ref[...] += swiglu(slice(None))

        @pl.when(j == nj - 1)
        def _():
            def body(r, carry):
                rows = pl.ds(pl.multiple_of(r * er, er), er)
                epilogue(rows, i * tm + r * er, o_ref[rows, :])
                return carry

            lax.fori_loop(0, tm // er, body, 0)


def _ffn(x, mod, chunks, g, wg, wu, wd, gf, *, tm, nr, row0, rows_per_batch, final_norm, emit, name,
         tf=FFN_TF, tile0=0, ntiles=None, inplace=False):
    t = x.shape[0]
    ntiles = t // tm if ntiles is None else ntiles
    assert tf == FFN_TF or emit
    sub = FFN_TF // tf
    nj_store = pl.cdiv(D_FF, FFN_TF)
    nj = nj_store * sub
    er = min(FFN_EPILOGUE_ROWS, rows_per_batch)
    assert rows_per_batch & (rows_per_batch - 1) == 0 and rows_per_batch % nr == 0 and tm % nr == 0
    assert rows_per_batch % er == 0 and tm % er == 0
    rb = FFN_ROW_BLOCK if rows_per_batch % FFN_ROW_BLOCK == 0 and tm % FFN_ROW_BLOCK == 0 else 0
    body = functools.partial(_ffn_body, nr=nr, er=er, rb=rb, row0=row0,
                             batch_shift=rows_per_batch.bit_length() - 1, tile0=tile0,
                             final_norm=final_norm, emit=emit)

    row = pl.BlockSpec((1, D_MODEL), lambda i, j: (0, 0))
    tile_major = pl.BlockSpec((None, D_MODEL, tf), lambda i, j: (j // sub, 0, j % sub))
    row_major = pl.BlockSpec((tf, D_MODEL), lambda i, j: (j, 0))
    if emit:
        assert ntiles == 1, "bf16 weight tiles are written once, by a single token tile"
        wspecs = [pl.BlockSpec((D_MODEL, tf), lambda i, j: (0, j)),
                  pl.BlockSpec((D_MODEL, tf), lambda i, j: (0, j)),
                  row_major]
    else:
        wspecs = [tile_major, tile_major, row_major]
    xmode = dict(pipeline_mode=pl.Buffered(1)) if ntiles == 1 else {}
    in_specs = [pl.BlockSpec((tm, D_MODEL), lambda i, j: (i + tile0, 0), **xmode),
                _mod_spec(chunks[0]), _mod_spec(chunks[1]), _mod_spec(chunks[2]), row] + wspecs + [row]
    args = [x, mod, mod, mod, g.reshape(1, D_MODEL), wg, wu, wd, gf.reshape(1, D_MODEL)]
    out_specs = [pl.BlockSpec((tm, D_MODEL), lambda i, j: (i + tile0, 0))]
    out_shape = [jax.ShapeDtypeStruct((t, D_MODEL), F32)]
    if emit:
        out_specs += [tile_major, tile_major, row_major]
        out_shape += [jax.ShapeDtypeStruct((nj_store, D_MODEL, FFN_TF), BF16),
                      jax.ShapeDtypeStruct((nj_store, D_MODEL, FFN_TF), BF16),
                      jax.ShapeDtypeStruct((nj_store * FFN_TF, D_MODEL), BF16)]
    return pl.pallas_call(
        body,
        grid=(ntiles, nj),
        in_specs=in_specs,
        out_specs=out_specs,
        out_shape=out_shape,
        input_output_aliases={0: 0} if inplace else {},
        scratch_shapes=[pltpu.VMEM((tm, D_MODEL), BF16 if nr % BF16_ROWS == 0 else F32)],
        compiler_params=_params(("parallel", "arbitrary")),
        name=name,
    )(*args)


def _inproj_body(x_ref, sh_ref, sc_ref, g_ref, w_ref, o_ref, wb_ref, h_ref, *, nr, row0, batch_shift):
    j = pl.program_id(1)
    tm = x_ref.shape[0]

    @pl.when(j == 0)
    def _():
        g = g_ref[...]

        def body(r, carry):
            rows = pl.ds(pl.multiple_of(r * nr, nr), nr)
            tok = r * nr
            h = _normmod(x_ref[rows, :], g, _mod_row(sc_ref, tok, row0, batch_shift),
                         _mod_row(sh_ref, tok, row0, batch_shift))
            h_ref[rows, :] = h
            return carry

        lax.fori_loop(0, tm // nr, body, 0, unroll=4)

    w = w_ref[...].astype(BF16)
    wb_ref[...] = w
    o_ref[...] = jnp.dot(h_ref[...].astype(BF16), w, preferred_element_type=F32)


def _inproj(x, mod, chunks, g, w_in, *, tn, nr, row0, rows_per_batch, name):
    tm = x.shape[0]
    assert rows_per_batch & (rows_per_batch - 1) == 0 and rows_per_batch % nr == 0 and tm % nr == 0
    body = functools.partial(_inproj_body, nr=nr, row0=row0, batch_shift=rows_per_batch.bit_length() - 1)
    wspec = pl.BlockSpec((D_MODEL, tn), lambda i, j: (0, j))
    return pl.pallas_call(
        body,
        grid=(1, IN_COLS // tn),
        in_specs=[
            pl.BlockSpec((tm, D_MODEL), lambda i, j: (0, 0)),
            _mod_spec(chunks[0]), _mod_spec(chunks[1]),
            pl.BlockSpec((1, D_MODEL), lambda i, j: (0, 0)),
            wspec,
        ],
        out_specs=[pl.BlockSpec((tm, tn), lambda i, j: (0, j)), wspec],
        out_shape=[jax.ShapeDtypeStruct((tm, IN_COLS), F32), jax.ShapeDtypeStruct((D_MODEL, IN_COLS), BF16)],
        scratch_shapes=[pltpu.VMEM((tm, D_MODEL), F32)],
        compiler_params=_params(("arbitrary", "arbitrary")),
        name=name,
    )(x, mod, mod, g.reshape(1, D_MODEL), w_in)


def _outproj_body(m_ref, x_ref, gt_ref, w_ref, o_ref, wb_ref, *, nr, row0, batch_shift):
    tm = x_ref.shape[0]
    w = w_ref[...].astype(BF16)
    wb_ref[...] = w
    o_ref[...] = jnp.dot(m_ref[...].astype(BF16), w, preferred_element_type=F32)

    def body(r, carry):
        rows = pl.ds(pl.multiple_of(r * nr, nr), nr)
        gt = _mod_row(gt_ref, r * nr, row0, batch_shift)
        o_ref[rows, :] = x_ref[rows, :] + gt * o_ref[rows, :]
        return carry

    lax.fori_loop(0, tm // nr, body, 0)


def _outproj(mix, x, mod, gt_c, w_out, *, tn, nr, row0, rows_per_batch, name):
    tm = x.shape[0]
    nn = D_MODEL // tn
    assert rows_per_batch & (rows_per_batch - 1) == 0 and rows_per_batch % nr == 0 and tm % nr == 0
    body = functools.partial(_outproj_body, nr=nr, row0=row0, batch_shift=rows_per_batch.bit_length() - 1)
    wspec = pl.BlockSpec((D_MODEL, tn), lambda i, j: (0, j))
    tile = pl.BlockSpec((tm, tn), lambda i, j: (0, j))
    return pl.pallas_call(
        body,
        grid=(1, nn),
        in_specs=[
            pl.BlockSpec((tm, D_MODEL), lambda i, j: (0, 0)),
            tile,
            pl.BlockSpec((MOD_ROWS, tn), lambda i, j: (0, gt_c * nn + j)),
            wspec,
        ],
        out_specs=[tile, wspec],
        out_shape=[jax.ShapeDtypeStruct((tm, D_MODEL), F32), jax.ShapeDtypeStruct((D_MODEL, D_MODEL), BF16)],
        compiler_params=_params(("arbitrary", "arbitrary")),
        name=name,
    )(mix, x, mod, w_out)


def _rope(x, cos, sin_signed):
    return x * cos + pltpu.roll(x, HEAD_DIM // 2, 1) * sin_signed


def _lru_gates(xc_ref, a_ref, wa_ref, ba_ref, wx_ref, bx_ref, lam_ref, midway=None):
    nlam = -lam_ref[...]
    softplus = jnp.maximum(nlam, 0.0) + jnp.log1p(jnp.exp(-jnp.abs(nlam)))
    rate = -LRU_C * softplus
    for hh in range(LRU_HEADS):
        if midway is not None:
            midway()
        cols = slice(hh * LRU_BLOCK, (hh + 1) * LRU_BLOCK)
        xc = xc_ref[:, cols]
        xcb = xc.astype(BF16)
        ra = jnp.dot(xcb, wa_ref[hh].astype(BF16), preferred_element_type=F32) + ba_ref[:, cols]
        rx = jnp.dot(xcb, wx_ref[hh].astype(BF16), preferred_element_type=F32) + bx_ref[:, cols]
        r = _sigmoid(ra)
        gi = _sigmoid(rx)
        a = jnp.exp(r * rate[:, cols])
        a_ref[:, cols] = a
        xc_ref[:, cols] = jnp.sqrt(1.0 - a * a) * (gi * xc)


def _scan_group(a, u, carry, row):
    for s in (1, 2, 4):
        a_sh = pltpu.roll(a, s, 0)
        u_sh = pltpu.roll(u, s, 0)
        m = row >= s
        u = jnp.where(m, a * u_sh + u, u)
        a = jnp.where(m, a * a_sh, a)
    return a * carry + u


def _softmax_pv(s, mask, sink, v):
    s = jnp.where(mask, s, NEG)
    m = jnp.maximum(jnp.max(s, axis=-1, keepdims=True), sink)
    p = jnp.exp(s - m)
    den = jnp.sum(p, axis=-1, keepdims=True) + jnp.exp(sink - m)
    return jnp.dot(p.astype(BF16), v, preferred_element_type=F32) / den


def _sink_column(sink_ref, kh, rows_per_head):
    n = GROUP * rows_per_head
    ri = lax.broadcasted_iota(jnp.int32, (n, 1), 0)
    col = jnp.full((n, 1), sink_ref[kh * GROUP + GROUP - 1], F32)
    for g in range(GROUP - 2, -1, -1):
        col = jnp.where(ri < (g + 1) * rows_per_head, sink_ref[kh * GROUP + g], col)
    return col


def _mix_tile(proj, mix, cs_ref, sn_ref, cw_ref, cb_ref, wa_ref, ba_ref, wx_ref, bx_ref, lam_ref, sink_ref,
              xpad, hc, kpad, vpad, a_s, u_s, klast, *, ts, pos0, mxu_fill):
    fill = iter(mxu_fill)

    def take(n):
        for _ in range(n):
            f = next(fill, None)
            if f is not None:
                f()

    w = LRU_WIDTH
    xpad[SUBLANES:SUBLANES + ts, :] = proj[:, 0:w]
    half = ts // 2
    for r0 in (0, half):
        take(1)
        xc = cb_ref[...]
        for jj in range(CONV_W):
            off = r0 + SUBLANES - (CONV_W - 1) + jj
            xc = xc + xpad[off:off + half, :] * cw_ref[jj:jj + 1, :]
        u_s[r0:r0 + half, :] = xc
    xpad[0:SUBLANES, :] = xpad[ts:ts + SUBLANES, :]

    _lru_gates(u_s, a_s, wa_ref, ba_ref, wx_ref, bx_ref, lam_ref, midway=lambda: take(1))

    row = lax.broadcasted_iota(jnp.int32, (SUBLANES, w), 0)
    carry = hc[...]
    ngroups = ts // SUBLANES
    for g in range(ngroups):
        if g % (ngroups // 8) == 0:
            take(1)
        rows = slice(g * SUBLANES, (g + 1) * SUBLANES)
        h = _scan_group(a_s[rows, :], u_s[rows, :], carry, row)
        a_s[rows, :] = h
        carry = jnp.broadcast_to(h[SUBLANES - 1:SUBLANES, :], (SUBLANES, w))
    hc[...] = carry
    for r0 in (0, half):
        take(1)
        rows = slice(r0, r0 + half)
        mix[rows, 0:w] = (a_s[rows, :] * jax.nn.gelu(proj[rows, w:2 * w])).astype(BF16)

    take(1)
    cos = cs_ref[...]
    sin = sn_ref[...]
    for kh in range(N_KV_HEADS):
        cols = slice(kh * HEAD_DIM, (kh + 1) * HEAD_DIM)
        kr = _rope(proj[:, K_COL + kh * HEAD_DIM:K_COL + (kh + 1) * HEAD_DIM], cos, sin)
        kpad[WINDOW:WINDOW + ts, cols] = kr.astype(BF16)
        klast[:, cols] = kr[ts - WINDOW:, :]
    vpad[WINDOW:WINDOW + ts, :] = proj[:, V_COL:V_COL + KV_WIDTH].astype(BF16)

    nq = GROUP * WINDOW
    r_i = lax.broadcasted_iota(jnp.int32, (nq, 2 * WINDOW), 0) & (WINDOW - 1)
    c_i = lax.broadcasted_iota(jnp.int32, (nq, 2 * WINDOW), 1)
    band = (c_i >= r_i) & (c_i <= r_i + WINDOW)
    for n in range(ts // WINDOW):
        qrows = slice(n * WINDOW, (n + 1) * WINDOW)
        kpos0 = pos0 + (n - 1) * WINDOW
        mask = band & (c_i + kpos0 >= 0)
        for kh in range(N_KV_HEADS):
            cols = slice(kh * HEAD_DIM, (kh + 1) * HEAD_DIM)
            qs = []
            for g in range(GROUP):
                hd = kh * GROUP + g
                qh = proj[qrows, Q_COL + hd * HEAD_DIM:Q_COL + (hd + 1) * HEAD_DIM]
                qs.append(_rope(qh, cos[qrows, :], sin[qrows, :]).astype(BF16))
            q4 = jnp.concatenate(qs, axis=0)
            kk = kpad[n * WINDOW:(n + 2) * WINDOW, cols]
            vv = vpad[n * WINDOW:(n + 2) * WINDOW, cols]
            s = lax.dot_general(q4, kk, (((1,), (1,)), ((), ())), preferred_element_type=F32) * SCALE
            take(1)
            o = _softmax_pv(s, mask, _sink_column(sink_ref, kh, WINDOW), vv)
            for g in range(GROUP):
                hd = kh * GROUP + g
                mix[qrows, w + hd * HEAD_DIM:w + (hd + 1) * HEAD_DIM] = (
                    o[g * WINDOW:(g + 1) * WINDOW, :].astype(BF16))

    take(len(mxu_fill))
    kpad[0:WINDOW, :] = kpad[ts:ts + WINDOW, :]
    vpad[0:WINDOW, :] = vpad[ts:ts + WINDOW, :]


def _mixer_body(xa_ref, xc_ref, sh_ref, sc_ref, gt_ref, gm_ref, win_ref, wout_ref, cs_ref, sn_ref,
                cw_ref, cb_ref, wa_ref, ba_ref, wx_ref, bx_ref, lam_ref, sink_ref,
                o_ref, lru_ref, k_ref, v_ref, conv_ref,
                proj0, proj1, mix0, mix1, h_s, xpad, hc, kpad, vpad, a_s, u_s, klast,
                *, ts, nt, ntiles, batch_shift):
    s = pl.program_id(0)
    tile_a = jnp.minimum(s, ntiles - 1)
    tile_b = jnp.clip(s - 1, 0, ntiles - 1)
    tile_c = jnp.clip(s - 2, 0, ntiles - 1)
    t_in = tile_b & (nt - 1)
    proj = (proj0, proj1)
    mix = (mix0, mix1)

    def out_chunks(mix_r):
        gt = _mod_row(gt_ref, tile_c * ts, 0, batch_shift)

        def chunk(k):
            cols = slice(k * MIXER_TN, (k + 1) * MIXER_TN)

            def run():
                d = jnp.dot(mix_r[...], wout_ref[:, cols], preferred_element_type=F32)
                o_ref[:, cols] = xc_ref[:, cols] + gt[:, cols] * d
            return run
        return [chunk(k) for k in range(D_MODEL // MIXER_TN)]

    def in_chunks(proj_w):
        def chunk(k):
            cols = slice(k * MIXER_TN, (k + 1) * MIXER_TN)

            def run():
                if k == 0:
                    g = gm_ref[...]
                    sc = _mod_row(sc_ref, tile_a * ts, 0, batch_shift)
                    sh = _mod_row(sh_ref, tile_a * ts, 0, batch_shift)
                    for r in range(ts // FFN_ROWS):
                        rows = slice(r * FFN_ROWS, (r + 1) * FFN_ROWS)
                        h_s[rows, :] = _normmod(xa_ref[rows, :], g, sc, sh).astype(BF16)
                proj_w[:, cols] = jnp.dot(h_s[...], win_ref[:, cols], preferred_element_type=F32)
            return run
        return [chunk(k) for k in range(IN_COLS // MIXER_TN)]

    @pl.when(s == 0)
    def _():
        mix1[...] = jnp.zeros(mix1.shape, BF16)
        for run in in_chunks(proj[0]):
            run()

    @pl.when(s == ntiles + 1)
    def _():
        for run in out_chunks(mix[(ntiles - 1) & 1]):
            run()

    @pl.when(t_in == 0)
    def _():
        xpad[0:SUBLANES, :] = jnp.zeros((SUBLANES, LRU_WIDTH), F32)
        hc[...] = jnp.zeros((SUBLANES, LRU_WIDTH), F32)
        kpad[0:WINDOW, :] = jnp.zeros((WINDOW, KV_WIDTH), BF16)
        vpad[0:WINDOW, :] = jnp.zeros((WINDOW, KV_WIDTH), BF16)

    def step(par):
        proj_r, mix_w = proj[1 - par], mix[1 - par]
        _mix_tile(proj_r, mix_w, cs_ref, sn_ref, cw_ref, cb_ref, wa_ref, ba_ref, wx_ref, bx_ref, lam_ref,
                  sink_ref, xpad, hc, kpad, vpad, a_s, u_s, klast, ts=ts, pos0=t_in * ts,
                  mxu_fill=in_chunks(proj[par]) + out_chunks(mix[par]))
        lru_ref[0] = hc[0:1, :]
        k_ref[0] = klast[...]
        v_ref[0] = proj_r[ts - WINDOW:, V_COL:V_COL + KV_WIDTH]
        conv_ref[0] = xpad[0:SUBLANES, :]

    for par in (0, 1):
        @pl.when((s >= 1) & (s <= ntiles) & ((s & 1) == par))
        def _(par=par):
            step(par)


def _mixer_prompt(x, mod, g_mix, w_in_b, w_out_b, cos, sin, p, *, batch, seq):
    ts = MIX_TS
    nt = seq // ts
    ntiles = batch * nt
    assert nt & (nt - 1) == 0 and seq & (seq - 1) == 0
    w = LRU_WIDTH
    body = functools.partial(_mixer_body, ts=ts, nt=nt, ntiles=ntiles, batch_shift=seq.bit_length() - 1)

    def tile_a(s):
        return jnp.minimum(s, ntiles - 1)

    def tile_b(s):
        return jnp.clip(s - 1, 0, ntiles - 1)

    def tile_c(s):
        return jnp.clip(s - 2, 0, ntiles - 1)

    def whole(shape, **kw):
        return pl.BlockSpec(shape, lambda s: (0,) * len(shape), **kw)

    def mspec(c):
        return pl.BlockSpec((MOD_ROWS, D_MODEL), lambda s: (0, c))

    def state(shape):
        return pl.BlockSpec((1,) + shape, lambda s: (tile_b(s) // nt, 0, 0))

    once = dict(pipeline_mode=pl.Buffered(1))
    return pl.pallas_call(
        body,
        grid=(ntiles + 2,),
        in_specs=[
            pl.BlockSpec((ts, D_MODEL), lambda s: (tile_a(s), 0)),
            pl.BlockSpec((ts, D_MODEL), lambda s: (tile_c(s), 0)),
            mspec(3), mspec(4), mspec(5),
            whole((1, D_MODEL)),
            whole((D_MODEL, IN_COLS), **once),
            whole((D_MODEL, D_MODEL), **once),
            pl.BlockSpec((ts, HEAD_DIM), lambda s: (tile_b(s) % nt, 0)),
            pl.BlockSpec((ts, HEAD_DIM), lambda s: (tile_b(s) % nt, 0)),
            whole((CONV_W, w)), whole((1, w)),
            whole((LRU_HEADS, LRU_BLOCK, LRU_BLOCK)), whole((1, w)),
            whole((LRU_HEADS, LRU_BLOCK, LRU_BLOCK)), whole((1, w)),
            whole((1, w)),
            pl.BlockSpec(memory_space=pltpu.SMEM),
        ],
        out_specs=[
            pl.BlockSpec((ts, D_MODEL), lambda s: (tile_c(s), 0)),
            state((1, w)), state((WINDOW, KV_WIDTH)), state((WINDOW, KV_WIDTH)), state((SUBLANES, w)),
        ],
        out_shape=[
            jax.ShapeDtypeStruct((batch * seq, D_MODEL), F32),
            jax.ShapeDtypeStruct((batch, 1, w), F32),
            jax.ShapeDtypeStruct((batch, WINDOW, KV_WIDTH), F32),
            jax.ShapeDtypeStruct((batch, WINDOW, KV_WIDTH), F32),
            jax.ShapeDtypeStruct((batch, SUBLANES, w), F32),
        ],
        scratch_shapes=[
            pltpu.VMEM((ts, IN_COLS), F32), pltpu.VMEM((ts, IN_COLS), F32),
            pltpu.VMEM((ts, D_MODEL), BF16), pltpu.VMEM((ts, D_MODEL), BF16),
            pltpu.VMEM((ts, D_MODEL), BF16),
            pltpu.VMEM((SUBLANES + ts, w), F32),
            pltpu.VMEM((SUBLANES, w), F32),
            pltpu.VMEM((WINDOW + ts, KV_WIDTH), BF16),
            pltpu.VMEM((WINDOW + ts, KV_WIDTH), BF16),
            pltpu.VMEM((ts, w), F32),
            pltpu.VMEM((ts, w), F32),
            pltpu.VMEM((WINDOW, KV_WIDTH), F32),
        ],
        compiler_params=_params(("arbitrary",)),
        name="mixer_prompt",
    )(x, x, mod, mod, mod, g_mix.reshape(1, D_MODEL), w_in_b, w_out_b, cos, sin,
      p["conv_w"], p["conv_b"], p["wa"], p["ba"], p["wx"], p["bx"], p["lam"], p["sinks"])


def _mix_sample_body(proj_ref, prev_ref, h0_ref, ck_ref, cv_ref, cs_ref, sn_ref, cw_ref, cb_ref,
                     wa_ref, ba_ref, wx_ref, bx_ref, lam_ref, sink_ref,
                     mix_ref, lru_ref, knew_ref,
                     a_s, u_s, q_s, *, batch, seq):
    w = LRU_WIDTH
    row = lax.broadcasted_iota(jnp.int32, (SUBLANES, w), 0)

    def group(b):
        return pl.ds(pl.multiple_of(b * SUBLANES, SUBLANES), SUBLANES)

    def conv_body(b, carry):
        rows = group(b)
        cur = proj_ref[rows, 0:w]
        prev = prev_ref[rows, :]
        xc = cb_ref[...]
        for jj in range(CONV_W):
            d = CONV_W - 1 - jj
            if d == 0:
                term = cur
            else:
                term = jnp.where(row >= d, pltpu.roll(cur, d, 0), pltpu.roll(prev, d, 0))
            xc = xc + term * cw_ref[jj:jj + 1, :]
        u_s[rows, :] = xc
        return carry

    lax.fori_loop(0, batch, conv_body, 0, unroll=4)

    _lru_gates(u_s, a_s, wa_ref, ba_ref, wx_ref, bx_ref, lam_ref)

    def scan_body(b, carry):
        rows = group(b)
        h0 = jnp.broadcast_to(h0_ref[pl.ds(b, 1), :], (SUBLANES, w))
        h = _scan_group(a_s[rows, :], u_s[rows, :], h0, row)
        a_s[rows, :] = h
        lru_ref[pl.ds(b, 1), :] = h[SUBLANES - 1:SUBLANES, :]
        return carry

    lax.fori_loop(0, batch, scan_body, 0, unroll=4)
    mix_ref[:, 0:w] = a_s[...] * jax.nn.gelu(proj_ref[:, w:2 * w])

    cos = cs_ref[...]
    sin = sn_ref[...]
    for hd in range(N_HEADS):
        cols = slice(hd * HEAD_DIM, (hd + 1) * HEAD_DIM)
        q_s[:, cols] = _rope(proj_ref[:, Q_COL + hd * HEAD_DIM:Q_COL + (hd + 1) * HEAD_DIM], cos, sin)
    for kh in range(N_KV_HEADS):
        cols = slice(kh * HEAD_DIM, (kh + 1) * HEAD_DIM)
        knew_ref[:, cols] = _rope(proj_ref[:, K_COL + kh * HEAD_DIM:K_COL + (kh + 1) * HEAD_DIM], cos, sin)

    nq = GROUP * seq
    nk = 2 * WINDOW
    r_i = lax.broadcasted_iota(jnp.int32, (nq, nk), 0) & (seq - 1)
    c_i = lax.broadcasted_iota(jnp.int32, (nq, nk), 1)
    mask = (c_i >= r_i) & (c_i <= r_i + WINDOW)
    pad = jnp.zeros((nk - WINDOW - seq, HEAD_DIM), F32)

    def attn_body(bg, carry):
        work = [(bg * SAMPLE_ATTN_BATCHES + k, kh) for k in range(SAMPLE_ATTN_BATCHES)
                for kh in range(N_KV_HEADS)]
        scores = []
        for b, kh in work:
            rows = group(b)
            cols = slice(kh * HEAD_DIM, (kh + 1) * HEAD_DIM)
            q4 = jnp.concatenate(
                [q_s[rows, (kh * GROUP + g) * HEAD_DIM:(kh * GROUP + g + 1) * HEAD_DIM] for g in range(GROUP)],
                axis=0).astype(BF16)
            cached = pl.ds(kh, WINDOW, stride=N_KV_HEADS)
            kk = jnp.concatenate([ck_ref[b, cached, :], knew_ref[rows, cols], pad], axis=0).astype(BF16)
            scores.append(lax.dot_general(q4, kk, (((1,), (1,)), ((), ())), preferred_element_type=F32) * SCALE)
        for (b, kh), s in zip(work, scores):
            rows = group(b)
            cached = pl.ds(kh, WINDOW, stride=N_KV_HEADS)
            vv = jnp.concatenate(
                [cv_ref[b, cached, :], proj_ref[rows, V_COL + kh * HEAD_DIM:V_COL + (kh + 1) * HEAD_DIM], pad],
                axis=0).astype(BF16)
            o = _softmax_pv(s, mask, _sink_column(sink_ref, kh, seq), vv)
            for g in range(GROUP):
                hd = kh * GROUP + g
                mix_ref[rows, w + hd * HEAD_DIM:w + (hd + 1) * HEAD_DIM] = o[g * seq:(g + 1) * seq, :]
        return carry

    lax.fori_loop(0, batch // SAMPLE_ATTN_BATCHES, attn_body, 0)


def _mix_sample(proj, prev, h0, cache_k, cache_v, cos, sin, p, *, batch, seq):
    assert seq == SUBLANES, "each sample batch must be exactly one sublane group"
    assert batch % SAMPLE_ATTN_BATCHES == 0
    t = batch * seq
    body = functools.partial(_mix_sample_body, batch=batch, seq=seq)
    vmem = pl.BlockSpec(memory_space=pltpu.VMEM)
    return pl.pallas_call(
        body,
        in_specs=[vmem] * 14 + [pl.BlockSpec(memory_space=pltpu.SMEM)],
        out_specs=[vmem, vmem, vmem],
        out_shape=[
            jax.ShapeDtypeStruct((t, D_MODEL), F32),
            jax.ShapeDtypeStruct((batch, LRU_WIDTH), F32),
            jax.ShapeDtypeStruct((t, KV_WIDTH), F32),
        ],
        scratch_shapes=[
            pltpu.VMEM((t, LRU_WIDTH), F32),
            pltpu.VMEM((t, LRU_WIDTH), F32),
            pltpu.VMEM((t, ATTN_WIDTH), F32),
        ],
        compiler_params=pltpu.CompilerParams(vmem_limit_bytes=V7X_VMEM_LIMIT_BYTES),
        name="mix_sample",
    )(proj, prev, h0, cache_k, cache_v, cos, sin, p["conv_w"], p["conv_b"], p["wa"], p["ba"], p["wx"], p["bx"],
      p["lam"], p["sinks"])


def _layer(w, l):
    return w.reshape(w.shape[1:]) if w.shape[0] == 1 else w[l]


def _rope_tables(pos):
    half = HEAD_DIM // 2
    inv = ROPE_THETA ** (-jnp.arange(half, dtype=F32) / half)
    ang = pos.astype(F32)[:, None] * inv[None, :]
    cos = jnp.cos(ang)
    sin = jnp.sin(ang)
    return jnp.concatenate([cos, cos], axis=-1), jnp.concatenate([-sin, sin], axis=-1)


def kernel(x_prompt, x_sample, c_prompt, c_sample, state_conv, state_lru, cache_k_win, cache_v_win, ada_w, ada_b, norm_ffn1, norm_mix, norm_ffn2, ffn1_w_gate, ffn1_w_up, ffn1_w_down, w_in, conv_w, conv_b, lru_w_a, lru_b_a, lru_w_x, lru_b_x, lru_lambda, attn_sinks, w_out, ffn2_w_gate, ffn2_w_up, ffn2_w_down, norm_final):
    bp, sp, _ = x_prompt.shape
    bs, ss, _ = x_sample.shape
    depth = ada_w.shape[0]
    tp, tsn = bp * sp, bs * ss
    assert bp <= MOD_PROMPT_ROWS and bp + bs <= MOD_ROWS
    tm_p = 1024
    assert sp % tm_p == 0

    xp = x_prompt.reshape(tp, D_MODEL)
    xs = x_sample.reshape(tsn, D_MODEL)
    c_all = jnp.concatenate(
        [c_prompt, jnp.zeros((MOD_PROMPT_ROWS - bp, D_MODEL), F32), c_sample,
         jnp.zeros((MOD_ROWS - MOD_PROMPT_ROWS - bs, D_MODEL), F32)], axis=0)
    cos_p, sin_p = _rope_tables(jnp.arange(sp))
    cos_s, sin_s = _rope_tables(PAST_LEN + jnp.arange(ss))
    cos_s = jnp.tile(cos_s, (bs, 1))
    sin_s = jnp.tile(sin_s, (bs, 1))

    outs_p, outs_s = [], []
    for l in range(depth):
        last = l == depth - 1
        L = functools.partial(_layer, l=l)
        cache_k, cache_v = L(cache_k_win), L(cache_v_win)
        mod = _adaln(c_all, L(ada_w), L(ada_b))
        p = dict(conv_w=L(conv_w), conv_b=L(conv_b).reshape(1, LRU_WIDTH), wa=L(lru_w_a),
                 ba=L(lru_b_a).reshape(1, LRU_WIDTH), wx=L(lru_w_x), bx=L(lru_b_x).reshape(1, LRU_WIDTH),
                 lam=L(lru_lambda).reshape(1, LRU_WIDTH), sinks=L(attn_sinks))
        ffn1 = (L(norm_ffn1), L(ffn1_w_gate), L(ffn1_w_up), L(ffn1_w_down), norm_final)
        ffn2 = (L(norm_ffn2), L(ffn2_w_gate), L(ffn2_w_up), L(ffn2_w_down), norm_final)
        ffn_p = dict(tm=tm_p, nr=FFN_ROWS, row0=0, rows_per_batch=sp)
        ffn_s = dict(tm=tsn, nr=ss, row0=MOD_PROMPT_ROWS, rows_per_batch=ss)
        proj_s_kw = dict(nr=ss, row0=MOD_PROMPT_ROWS, rows_per_batch=ss)

        g, wg, wu, wd, gf = ffn1
        xs, wg_b, wu_b, wd_b = _ffn(xs, mod, (0, 1, 2), g, wg, wu, wd, gf, emit=True, final_norm=False,
                                    name="ffn1_sample", **ffn_s)
        xp, = _ffn(xp, mod, (0, 1, 2), g, wg_b, wu_b, wd_b, gf, emit=False, final_norm=False,
                   name="ffn1_prompt", **ffn_p)

        proj_s, w_in_b = _inproj(xs, mod, (3, 4), L(norm_mix), L(w_in), tn=INPROJ_TN, name="inproj_sample",
                                 **proj_s_kw)
        prev_s = jnp.pad(L(state_conv), ((0, 0), (SUBLANES - (CONV_W - 1), 0), (0, 0))).reshape(tsn, LRU_WIDTH)
        mix_s, lru_s, knew_s = _mix_sample(
            proj_s, prev_s, L(state_lru), cache_k.reshape(bs, WINDOW * N_KV_HEADS, HEAD_DIM),
            cache_v.reshape(bs, WINDOW * N_KV_HEADS, HEAD_DIM), cos_s, sin_s, p, batch=bs, seq=ss)
        xs, w_out_b = _outproj(mix_s, xs, mod, 5, L(w_out), tn=OUTPROJ_TN, name="outproj_sample", **proj_s_kw)
        xp, lru_p, k_p, v_p, conv_p = _mixer_prompt(xp, mod, L(norm_mix), w_in_b, w_out_b, cos_p, sin_p, p,
                                                    batch=bp, seq=sp)

        g, wg, wu, wd, gf = ffn2
        xp, wg_b, wu_b, wd_b = _ffn(xp, mod, (6, 7, 8), g, wg, wu, wd, gf, emit=True, tf=FFN_EMIT_TF, ntiles=1,
                                    inplace=True, final_norm=last, name="ffn2_head", **ffn_p)
        xp, = _ffn(xp, mod, (6, 7, 8), g, wg_b, wu_b, wd_b, gf, emit=False, tile0=1, ntiles=tp // tm_p - 1,
                   inplace=True, final_norm=last, name="ffn2_prompt", **ffn_p)
        xs, = _ffn(xs, mod, (6, 7, 8), g, wg_b, wu_b, wd_b, gf, emit=False, final_norm=last,
                   name="ffn2_sample", **ffn_s)

        ps = proj_s.reshape(bs, ss, IN_COLS)
        outs_p.append((
            conv_p[:, SUBLANES - (CONV_W - 1):, :],
            lru_p.reshape(bp, LRU_WIDTH),
            k_p.reshape(bp, WINDOW, N_KV_HEADS, HEAD_DIM),
            v_p.reshape(bp, WINDOW, N_KV_HEADS, HEAD_DIM),
        ))
        k_all = jnp.concatenate([cache_k, knew_s.reshape(bs, ss, N_KV_HEADS, HEAD_DIM)], axis=1)
        v_all = jnp.concatenate([cache_v, ps[:, :, V_COL:].reshape(bs, ss, N_KV_HEADS, HEAD_DIM)], axis=1)
        outs_s.append((
            ps[:, ss - (CONV_W - 1):, 0:LRU_WIDTH],
            lru_s,
            k_all[:, -WINDOW:],
            v_all[:, -WINDOW:],
        ))

    y_prompt = xp.reshape(bp, sp, D_MODEL)
    y_sample = xs.reshape(bs, ss, D_MODEL)
    st_p = [jnp.stack([o[i] for o in outs_p]) for i in range(4)]
    st_s = [jnp.stack([o[i] for o in outs_s]) for i in range(4)]
    return (y_prompt, y_sample, st_p[0], st_p[1], st_p[2], st_p[3], st_s[0], st_s[1], st_s[2], st_s[3])
```
